```python
import math
import jax
import jax.numpy as jnp
from jax import lax
import numpy as np

D_MODEL = 4096
BATCH = 4
SEQ = 2048
DEPTH = 2
DEC_BATCH = 128
DEC_SEQ = 4
PAST_LEN = 16384
PAGE_SIZE = 128

N_EVEN = (DEPTH + 1) // 2
N_ODD = DEPTH // 2

MIX_A = D_MODEL // 2
MIX_B = D_MODEL - MIX_A
RWKV_HEAD = 64
RWKV_HEADS = MIX_A // RWKV_HEAD
RWKV_DECAY_RANK = max(32, int(round(1.8 * MIX_A ** 0.5 / 32)) * 32)
RWKV_ICL_RANK = max(32, int(round(1.8 * MIX_A ** 0.5 / 32)) * 32)
RWKV_GATE_RANK = max(32, int(round(0.6 * MIX_A ** 0.8 / 32)) * 32)
RWKV_IN = 3 * MIX_A + RWKV_DECAY_RANK + RWKV_ICL_RANK + RWKV_GATE_RANK
RWKV_LNX_EPS = 1e-5 * RWKV_HEAD

S5_GROUP = 16
S5_GROUPS = MIX_B // S5_GROUP
S5_STATE = 64

IN_E = RWKV_IN + MIX_B

MLSTM_HEADS = 8
MLSTM_DQK = D_MODEL // 16
MLSTM_DV = D_MODEL // 8
MLSTM_CHUNK = 64
GATE_CAP = 15.0
IN_O = 2 * MLSTM_HEADS * MLSTM_DQK + 2 * MLSTM_HEADS * MLSTM_DV + 2 * MLSTM_HEADS

FFN_DIM = ((8 * D_MODEL // 3 + 255) // 256) * 256
N_EXPERTS = 8
TOP_K = 2
EXPERT_FF = D_MODEL // 2
RMS_EPS = 1e-6

kernel_name = 'rwkv7_s5_mlstm_hybrid_step'


def _rmsnorm(x, g):
    xf = x.astype(jnp.float32)
    y = xf * lax.rsqrt(jnp.mean(xf * xf, axis=-1, keepdims=True) + RMS_EPS)
    return (y * g.astype(jnp.float32)).astype(x.dtype)


def _softcap(x):
    return GATE_CAP * jnp.tanh(x / GATE_CAP)


def _swiglu(h, w1, w3, w2):
    return (jax.nn.silu(h @ w1) * (h @ w3)) @ w2


def _rwkv7(z, shift_prev, wkv0, mu, w0, w_up, a0, a_up, g_up, k_k, k_a, r_k, lnx_w, lnx_b):
    bsz, t_len, _ = z.shape
    f32 = jnp.float32
    z = z.astype(f32)
    z_prev = jnp.concatenate([shift_prev[:, None].astype(f32), z[:, :-1]], axis=1)
    zs = z + mu * (z_prev - z)
    cuts = [MIX_A, 2 * MIX_A, 3 * MIX_A, 3 * MIX_A + RWKV_DECAY_RANK,
            3 * MIX_A + RWKV_DECAY_RANK + RWKV_ICL_RANK]
    r, k, v, dw, da, dg = jnp.split(zs, cuts, axis=-1)
    w_log = -jax.nn.softplus(-(w0 + jnp.tanh(dw) @ w_up)) - 0.5
    decay = jnp.exp(-jnp.exp(w_log))
    a = jax.nn.sigmoid(a0 + da @ a_up)
    g = jax.nn.sigmoid(dg) @ g_up
    kk = k * k_k
    k = k * (1.0 + (a - 1.0) * k_a)
    heads = lambda t: t.reshape(bsz, t_len, RWKV_HEADS, RWKV_HEAD)
    r, decay, k, v, a, kk = map(heads, (r, decay, k, v, a, kk))
    kk = kk / jnp.maximum(jnp.linalg.norm(kk, axis=-1, keepdims=True), 1e-12)

    def step(s, inp):
        r_t, w_t, k_t, v_t, ka_t, kb_t = inp
        sa = jnp.einsum('bhij,bhj->bhi', s, ka_t)
        s = s * w_t[:, :, None, :] + sa[..., None] * kb_t[:, :, None, :] + v_t[..., None] * k_t[:, :, None, :]
        return s, jnp.einsum('bhij,bhj->bhi', s, r_t)

    xs = tuple(jnp.moveaxis(t, 1, 0) for t in (r, decay, k, v, -kk, kk * a))
    wkv_t, y = lax.scan(step, wkv0.astype(f32), xs)
    y = jnp.moveaxis(y, 0, 1)
    mean = jnp.mean(y, axis=-1, keepdims=True)
    var = jnp.mean(jnp.square(y - mean), axis=-1, keepdims=True)
    y = ((y - mean) * lax.rsqrt(var + RWKV_LNX_EPS)).reshape(bsz, t_len, MIX_A) * lnx_w + lnx_b
    bonus = (jnp.sum(r * k * r_k, axis=-1, keepdims=True) * v).reshape(bsz, t_len, MIX_A)
    return (y + bonus) * g, wkv_t, z[:, -1]


def _s5(u, h0_re, h0_im, a_re, a_im, log_step, b_re, b_im, c_re, c_im, d_skip, glu_w, glu_b):
    bsz, t_len, _ = u.shape
    f32 = jnp.float32
    u = u.astype(f32)
    ug = u.reshape(bsz, t_len, S5_GROUPS, S5_GROUP)
    lam_re = a_re.astype(f32)
    lam_im = a_im.astype(f32)
    dt = jnp.exp(log_step.astype(f32))[:, None]
    mag = jnp.exp(lam_re * dt)
    ab_re = mag * jnp.cos(lam_im * dt)
    ab_im = mag * jnp.sin(lam_im * dt)
    inv = 1.0 / (lam_re * lam_re + lam_im * lam_im)
    zr = ((ab_re - 1.0) * lam_re + ab_im * lam_im) * inv
    zi = (ab_im * lam_re - (ab_re - 1.0) * lam_im) * inv
    b_re = b_re.astype(f32)
    b_im = b_im.astype(f32)
    bb_re = zr[..., None] * b_re - zi[..., None] * b_im
    bb_im = zr[..., None] * b_im + zi[..., None] * b_re
    bu_re = jnp.einsum('gnc,btgc->btgn', bb_re, ug)
    bu_im = jnp.einsum('gnc,btgc->btgn', bb_im, ug)
    h0_re = h0_re.astype(f32)
    h0_im = h0_im.astype(f32)
    bu_re = bu_re.at[:, 0].add(ab_re * h0_re - ab_im * h0_im)
    bu_im = bu_im.at[:, 0].add(ab_re * h0_im + ab_im * h0_re)

    def combine(e1, e2):
        a1r, a1i, b1r, b1i = e1
        a2r, a2i, b2r, b2i = e2
        return (a2r * a1r - a2i * a1i, a2r * a1i + a2i * a1r,
                a2r * b1r - a2i * b1i + b2r, a2r * b1i + a2i * b1r + b2i)

    elems = (jnp.broadcast_to(ab_re, bu_re.shape), jnp.broadcast_to(ab_im, bu_im.shape), bu_re, bu_im)
    _, _, h_re, h_im = lax.associative_scan(combine, elems, axis=1)
    y = (jnp.einsum('gcn,btgn->btgc', c_re.astype(f32), h_re)
         - jnp.einsum('gcn,btgn->btgc', c_im.astype(f32), h_im))
    y = jax.nn.gelu(y.reshape(bsz, t_len, MIX_B) + d_skip * u)
    y = y * jax.nn.sigmoid(y @ glu_w + glu_b)
    return y, h_re[:, -1], h_im[:, -1]


def _mlstm(z, c0, n0, m0, b_i, b_f, norm_w):
    bsz, t_len, _ = z.shape
    f32 = jnp.float32
    nh, dqk, dv = MLSTM_HEADS, MLSTM_DQK, MLSTM_DV
    z = z.astype(f32)
    cuts = [nh * dqk, 2 * nh * dqk, 2 * nh * dqk + nh * dv,
            2 * nh * dqk + nh * dv + nh, 2 * nh * dqk + nh * dv + 2 * nh]
    q, k, v, ig, fg, og = jnp.split(z, cuts, axis=-1)
    heads = lambda t, d: t.reshape(bsz, t_len, nh, d).transpose(0, 2, 1, 3)
    q = heads(q, dqk)
    k = heads(k, dqk) * (dqk ** -0.5)
    v = heads(v, dv)
    ig = _softcap(ig + b_i).transpose(0, 2, 1)
    lf = jax.nn.log_sigmoid(_softcap(fg + b_f)).transpose(0, 2, 1)
    blk = math.gcd(t_len, MLSTM_CHUNK)
    n_blk = t_len // blk
    to_blocks = lambda t: jnp.moveaxis(t.reshape(bsz, nh, n_blk, blk, *t.shape[3:]), 2, 0)
    causal = jnp.tril(jnp.ones((blk, blk), dtype=bool))

    def step(carry, inp):
        c, n, m = carry
        q_b, k_b, v_b, ig_b, lf_b = inp
        b = jnp.cumsum(lf_b, axis=-1)
        log_d = jnp.where(causal, b[..., :, None] - b[..., None, :] + ig_b[..., None, :], -jnp.inf)
        log_p = b + m[..., None]
        m_tok = jnp.maximum(log_p, jnp.max(log_d, axis=-1))
        d = jnp.exp(log_d - m_tok[..., None])
        p = jnp.exp(log_p - m_tok)
        s = jnp.einsum('bhtd,bhsd->bhts', q_b, k_b) * d
        num = jnp.einsum('bhts,bhsv->bhtv', s, v_b) + p[..., None] * jnp.einsum('bhvd,bhtd->bhtv', c, q_b)
        den = jnp.sum(s, axis=-1) + p * jnp.einsum('bhd,bhtd->bht', n, q_b)
        h = num / jnp.maximum(jnp.abs(den), jnp.exp(-m_tok))[..., None]
        m_new = m_tok[..., -1]
        w_s = jnp.exp(b[..., -1:] - b + ig_b - m_new[..., None])
        carry_scale = jnp.exp(b[..., -1] + m - m_new)
        c_new = carry_scale[..., None, None] * c + jnp.einsum('bhs,bhsv,bhsd->bhvd', w_s, v_b, k_b)
        n_new = carry_scale[..., None] * n + jnp.einsum('bhs,bhsd->bhd', w_s, k_b)
        return (c_new, n_new, m_new), h

    init = (c0.astype(f32), n0.astype(f32), m0.astype(f32))
    (c_t, n_t, m_t), h = lax.scan(step, init, tuple(map(to_blocks, (q, k, v, ig, lf))))
    h = jnp.moveaxis(h, 0, 2).reshape(bsz, nh, t_len, dv).transpose(0, 2, 1, 3)
    h = h * lax.rsqrt(jnp.mean(h * h, axis=-1, keepdims=True) + RMS_EPS)
    h = h.reshape(bsz, t_len, nh * dv) * norm_w * jax.nn.sigmoid(og)
    return h, c_t, n_t, m_t


def _moe(h, router_w, router_b, w1, w3, w2):
    logits = h.astype(jnp.float32) @ router_w.astype(jnp.float32) + router_b.astype(jnp.float32)
    top_v, top_i = lax.top_k(logits, TOP_K)
    gates = jax.nn.softmax(top_v, axis=-1)
    comb = jnp.sum(jax.nn.one_hot(top_i, N_EXPERTS, dtype=jnp.float32) * gates[..., None], axis=-2)
    hid = jax.nn.silu(jnp.einsum('btd,edf->btef', h, w1)) * jnp.einsum('btd,edf->btef', h, w3)
    hid = hid * comb[..., None].astype(hid.dtype)
    return jnp.einsum('btef,efd->btd', hid, w2)


def setup_inputs(seed: int = 0) -> dict:
    key = jax.random.key(seed)
    ks = jax.random.split(key, 50)
    f32 = jnp.float32

    def nrm(i, shape, scale=1.0, shift=0.0):
        return shift + scale * jax.random.normal(ks[i], shape, f32)

    def uni(i, shape, lo, hi):
        return jax.random.uniform(ks[i], shape, f32, lo, hi)

    E, O, D = N_EVEN, N_ODD, D_MODEL
    return {
        'x_prompt': nrm(0, (BATCH, SEQ, D)),
        'x_sample': nrm(1, (DEC_BATCH, DEC_SEQ, D)),
        'state_rwkv_wkv': nrm(2, (E, DEC_BATCH, RWKV_HEADS, RWKV_HEAD, RWKV_HEAD), 0.1),
        'state_rwkv_shift': nrm(3, (E, DEC_BATCH, RWKV_IN)),
        'state_s5_re': nrm(4, (E, DEC_BATCH, S5_GROUPS, S5_STATE), 0.3),
        'state_s5_im': nrm(5, (E, DEC_BATCH, S5_GROUPS, S5_STATE), 0.3),
        'state_mlstm_c': nrm(6, (O, DEC_BATCH, MLSTM_HEADS, MLSTM_DV, MLSTM_DQK), 0.05),
        'state_mlstm_n': nrm(7, (O, DEC_BATCH, MLSTM_HEADS, MLSTM_DQK), 0.05),
        'state_mlstm_m': uni(8, (O, DEC_BATCH, MLSTM_HEADS), 0.0, 4.0),
        'ln_mix_e': nrm(9, (E, D), 0.02, 1.0),
        'w_in_e': nrm(10, (E, D, IN_E), D ** -0.5),
        'rwkv_mu': uni(11, (E, RWKV_IN), 0.0, 1.0),
        'rwkv_w0': uni(12, (E, MIX_A), -6.0, -1.0),
        'rwkv_w_up': nrm(13, (E, RWKV_DECAY_RANK, MIX_A), 0.5 * RWKV_DECAY_RANK ** -0.5),
        'rwkv_a0': nrm(14, (E, MIX_A), 0.5),
        'rwkv_a_up': nrm(15, (E, RWKV_ICL_RANK, MIX_A), 0.5 * RWKV_ICL_RANK ** -0.5),
        'rwkv_g_up': nrm(16, (E, RWKV_GATE_RANK, MIX_A), RWKV_GATE_RANK ** -0.5),
        'rwkv_k_k': nrm(17, (E, MIX_A), 0.1, 0.85),
        'rwkv_k_a': nrm(18, (E, MIX_A), 0.1, 1.0),
        'rwkv_r_k': nrm(19, (E, RWKV_HEADS, RWKV_HEAD), 0.1),
        'rwkv_lnx_w': nrm(20, (E, MIX_A), 0.02, 1.0),
        'rwkv_lnx_b': nrm(21, (E, MIX_A), 0.02),
        's5_a_re': nrm(22, (E, S5_GROUPS, S5_STATE), 0.01, -0.5),
        's5_a_im': jnp.pi * jnp.arange(S5_STATE, dtype=f32) + nrm(23, (E, S5_GROUPS, S5_STATE), 0.01),
        's5_log_step': uni(24, (E, S5_GROUPS), math.log(1e-3), math.log(1e-1)),
        's5_b_re': nrm(25, (E, S5_GROUPS, S5_STATE, S5_GROUP), (2 * S5_GROUP) ** -0.5),
        's5_b_im': nrm(26, (E, S5_GROUPS, S5_STATE, S5_GROUP), (2 * S5_GROUP) ** -0.5),
        's5_c_re': nrm(27, (E, S5_GROUPS, S5_GROUP, S5_STATE), (2 * S5_STATE) ** -0.5),
        's5_c_im': nrm(28, (E, S5_GROUPS, S5_GROUP, S5_STATE), (2 * S5_STATE) ** -0.5),
        's5_d': nrm(29, (E, MIX_B)),
        's5_glu_w': nrm(30, (E, MIX_B, MIX_B), MIX_B ** -0.5),
        's5_glu_b': nrm(31, (E, MIX_B), 0.02),
        'w_out_e': nrm(32, (E, MIX_A + MIX_B, D), (MIX_A + MIX_B) ** -0.5),
        'ln_ffn_e': nrm(33, (E, D), 0.02, 1.0),
        'ffn_w1': nrm(34, (E, D, FFN_DIM), D ** -0.5),
        'ffn_w3': nrm(35, (E, D, FFN_DIM), D ** -0.5),
        'ffn_w2': nrm(36, (E, FFN_DIM, D), FFN_DIM ** -0.5),
        'ln_mix_o': nrm(37, (O, D), 0.02, 1.0),
        'w_in_o': nrm(38, (O, D, IN_O), D ** -0.5),
        'mlstm_b_i': nrm(39, (O, MLSTM_HEADS), 0.1),
        'mlstm_b_f': nrm(40, (O, MLSTM_HEADS), 0.5, 3.0),
        'mlstm_norm_w': nrm(41, (O, MLSTM_HEADS * MLSTM_DV), 0.02, 1.0),
        'w_out_o': nrm(42, (O, MLSTM_HEADS * MLSTM_DV, D), (MLSTM_HEADS * MLSTM_DV) ** -0.5),
        'ln_ffn_o': nrm(43, (O, D), 0.02, 1.0),
        'router_w': nrm(44, (O, D, N_EXPERTS), D ** -0.5),
        'router_b': nrm(45, (O, N_EXPERTS), 0.01),
        'exp_w1': nrm(46, (O, N_EXPERTS, D, EXPERT_FF), D ** -0.5),
        'exp_w3': nrm(47, (O, N_EXPERTS, D, EXPERT_FF), D ** -0.5),
        'exp_w2': nrm(48, (O, N_EXPERTS, EXPERT_FF, D), EXPERT_FF ** -0.5),
        'final_norm': nrm(49, (D,), 0.02, 1.0),
    }


def reference(x_prompt, x_sample, state_rwkv_wkv, state_rwkv_shift, state_s5_re, state_s5_im,
              state_mlstm_c, state_mlstm_n, state_mlstm_m,
              ln_mix_e, w_in_e, rwkv_mu, rwkv_w0, rwkv_w_up, rwkv_a0, rwkv_a_up, rwkv_g_up,
              rwkv_k_k, rwkv_k_a, rwkv_r_k, rwkv_lnx_w, rwkv_lnx_b,
              s5_a_re, s5_a_im, s5_log_step, s5_b_re, s5_b_im, s5_c_re, s5_c_im, s5_d,
              s5_glu_w, s5_glu_b, w_out_e, ln_ffn_e, ffn_w1, ffn_w3, ffn_w2,
              ln_mix_o, w_in_o, mlstm_b_i, mlstm_b_f, mlstm_norm_w, w_out_o, ln_ffn_o,
              router_w, router_b, exp_w1, exp_w3, exp_w2, final_norm):

    def trunk(x, wkv, shift, s5_re, s5_im, mem_c, mem_n, mem_m):
        o_wkv, o_shift, o_re, o_im, o_c, o_n, o_m = [], [], [], [], [], [], []
        for layer in range(DEPTH):
            j = layer // 2
            if layer % 2 == 0:
                z = _rmsnorm(x, ln_mix_e[j]) @ w_in_e[j]
                y_a, n_wkv, n_shift = _rwkv7(z[..., :RWKV_IN], shift[j], wkv[j], rwkv_mu[j], rwkv_w0[j],
                                             rwkv_w_up[j], rwkv_a0[j], rwkv_a_up[j], rwkv_g_up[j],
                                             rwkv_k_k[j], rwkv_k_a[j], rwkv_r_k[j], rwkv_lnx_w[j], rwkv_lnx_b[j])
                y_b, n_re, n_im = _s5(z[..., RWKV_IN:], s5_re[j], s5_im[j], s5_a_re[j], s5_a_im[j],
                                      s5_log_step[j], s5_b_re[j], s5_b_im[j], s5_c_re[j], s5_c_im[j],
                                      s5_d[j], s5_glu_w[j], s5_glu_b[j])
                x = x + jnp.concatenate([y_a, y_b], axis=-1).astype(x.dtype) @ w_out_e[j]
                x = x + _swiglu(_rmsnorm(x, ln_ffn_e[j]), ffn_w1[j], ffn_w3[j], ffn_w2[j])
                o_wkv.append(n_wkv)
                o_shift.append(n_shift)
                o_re.append(n_re)
                o_im.append(n_im)
            else:
                z = _rmsnorm(x, ln_mix_o[j]) @ w_in_o[j]
                y_c, n_c, n_n, n_m = _mlstm(z, mem_c[j], mem_n[j], mem_m[j], mlstm_b_i[j], mlstm_b_f[j],
                                            mlstm_norm_w[j])
                x = x + y_c.astype(x.dtype) @ w_out_o[j]
                x = x + _moe(_rmsnorm(x, ln_ffn_o[j]), router_w[j], router_b[j], exp_w1[j], exp_w3[j], exp_w2[j])
                o_c.append(n_c)
                o_n.append(n_n)
                o_m.append(n_m)
        return (_rmsnorm(x, final_norm), jnp.stack(o_wkv), jnp.stack(o_shift), jnp.stack(o_re),
                jnp.stack(o_im), jnp.stack(o_c), jnp.stack(o_n), jnp.stack(o_m))

    f32 = jnp.float32
    bp = x_prompt.shape[0]
    y_p, p_wkv, p_shift, p_re, p_im, p_c, p_n, p_m = trunk(
        x_prompt,
        jnp.zeros((N_EVEN, bp, RWKV_HEADS, RWKV_HEAD, RWKV_HEAD), f32),
        jnp.zeros((N_EVEN, bp, RWKV_IN), f32),
        jnp.zeros((N_EVEN, bp, S5_GROUPS, S5_STATE), f32),
        jnp.zeros((N_EVEN, bp, S5_GROUPS, S5_STATE), f32),
        jnp.zeros((N_ODD, bp, MLSTM_HEADS, MLSTM_DV, MLSTM_DQK), f32),
        jnp.zeros((N_ODD, bp, MLSTM_HEADS, MLSTM_DQK), f32),
        jnp.zeros((N_ODD, bp, MLSTM_HEADS), f32))
    y_s, s_wkv, s_shift, s_re, s_im, s_c, s_n, s_m = trunk(
        x_sample, state_rwkv_wkv, state_rwkv_shift, state_s5_re, state_s5_im,
        state_mlstm_c, state_mlstm_n, state_mlstm_m)
    return (y_p, y_s, p_wkv, p_shift, p_re, p_im, p_c, p_n, p_m,
            s_wkv, s_shift, s_re, s_im, s_c, s_n, s_m)
```

```python
import functools
import math

import jax
import jax.numpy as jnp
from jax import lax
from jax.experimental import pallas as pl
from jax.experimental.pallas import tpu as pltpu

F32 = jnp.float32
BF16 = jnp.bfloat16

D_MODEL = 4096
MIX_A = D_MODEL // 2
MIX_B = D_MODEL - MIX_A
RWKV_HEAD = 64
RWKV_HEADS = MIX_A // RWKV_HEAD
RWKV_DECAY_RANK = 96
RWKV_ICL_RANK = 96
RWKV_GATE_RANK = 256
RWKV_IN = 3 * MIX_A + RWKV_DECAY_RANK + RWKV_ICL_RANK + RWKV_GATE_RANK
RWKV_LNX_EPS = 1e-5 * RWKV_HEAD
S5_GROUP = 16
S5_GROUPS = MIX_B // S5_GROUP
S5_STATE = 64
MLSTM_HEADS = 8
MLSTM_DQK = D_MODEL // 16
MLSTM_DV = D_MODEL // 8
MLSTM_CHUNK = 64
GATE_CAP = 15.0
FFN_DIM = 11008
N_EXPERTS = 8
EXPERT_FF = D_MODEL // 2
RMS_EPS = 1e-6

LANES = 128
SUBLANES = 8
VMEM_LIMIT = 56 * 1024 * 1024

RANK_PAD = LANES
ZE_DW = 3 * MIX_A
ZE_DA = ZE_DW + RANK_PAD
ZE_DG = ZE_DA + RANK_PAD
ZE_U = ZE_DG + RWKV_GATE_RANK
ZE_W = ZE_U + MIX_B
RW_CAT = 3 * LANES + 2 * RANK_PAD + RWKV_GATE_RANK
ZO_OG = 2 * MLSTM_HEADS * MLSTM_DQK + MLSTM_HEADS * MLSTM_DV
ZO_G = ZO_OG + MLSTM_HEADS * MLSTM_DV
ZO_W = ZO_G + 2 * LANES

S5_SLAB_GROUPS = 8
S5_SLABS = S5_GROUPS // S5_SLAB_GROUPS
S5_SLAB_STATE = S5_SLAB_GROUPS * S5_STATE

NN = (((1,), (0,)), ((), ()))
NT = (((1,), (1,)), ((), ()))
TN = (((0,), (0,)), ((), ()))
NEG_BIG = -1e30


def _cparams(sem):
    return pltpu.CompilerParams(dimension_semantics=sem, vmem_limit_bytes=VMEM_LIMIT)


def _pick(n, cands):
    for c in cands:
        if n % c == 0:
            return c
    raise ValueError(f"no tile for {n}")


def _dot(a, b, dims=NN):
    return lax.dot_general(a, b, dims, preferred_element_type=F32)


def _split2(a):
    hi = a.astype(BF16)
    lo = (a - hi.astype(F32)).astype(BF16)
    return hi, lo


def _dot_hp(a, b, dims=NN):
    ah, al = _split2(a)
    bh, bl = _split2(b)
    return _dot(ah, bh, dims) + (_dot(ah, bl, dims) + _dot(al, bh, dims))


def _dot_const(c_bf16, a):
    h0 = a.astype(BF16)
    r1 = a - h0.astype(F32)
    h1 = r1.astype(BF16)
    h2 = (r1 - h1.astype(F32)).astype(BF16)
    return _dot(c_bf16, h0) + (_dot(c_bf16, h1) + _dot(c_bf16, h2))


def _sigmoid(x):
    return 1.0 / (1.0 + jnp.exp(-x))


def _softplus(x):
    return jnp.maximum(x, 0.0) + jnp.log(1.0 + jnp.exp(-jnp.abs(x)))


def _iota(shape, dim):
    return lax.broadcasted_iota(jnp.int32, shape, dim)


def _rmsnorm_body(x_ref, g_ref, o_ref):
    x = x_ref[...]
    ms = jnp.mean(x * x, axis=-1, keepdims=True)
    o_ref[...] = (x * lax.rsqrt(ms + RMS_EPS) * g_ref[...]).astype(o_ref.dtype)


def _rmsnorm(x, g, out_dtype):
    m, d = x.shape
    tm = _pick(m, (256, 128, 64, 32, 16, 8))
    return pl.pallas_call(
        _rmsnorm_body,
        grid=(m // tm,),
        in_specs=[pl.BlockSpec((tm, d), lambda i: (i, 0)), pl.BlockSpec((1, d), lambda i: (0, 0))],
        out_specs=pl.BlockSpec((tm, d), lambda i: (i, 0)),
        out_shape=jax.ShapeDtypeStruct((m, d), out_dtype),
        compiler_params=_cparams(("parallel",)),
        name="rmsnorm",
    )(x, g.reshape(1, d))


def _router_body(x_ref, g_ref, rw_ref, rb_ref, hn_ref, comb_ref):
    x = x_ref[...]
    ms = jnp.mean(x * x, axis=-1, keepdims=True)
    h = x * lax.rsqrt(ms + RMS_EPS) * g_ref[...]
    hn_ref[...] = h.astype(hn_ref.dtype)
    logits = _dot_hp(h, rw_ref[...]) + rb_ref[...]
    lane = _iota(logits.shape, 1)
    logits = jnp.where(lane < N_EXPERTS, logits, NEG_BIG)
    m1 = jnp.max(logits, axis=-1, keepdims=True)
    i1 = jnp.min(jnp.where(logits == m1, lane, LANES), axis=-1, keepdims=True)
    rest = jnp.where(lane == i1, NEG_BIG, logits)
    m2 = jnp.max(rest, axis=-1, keepdims=True)
    i2 = jnp.min(jnp.where(rest == m2, lane, LANES), axis=-1, keepdims=True)
    e = jnp.exp(m2 - m1)
    g1 = 1.0 / (1.0 + e)
    g2 = e / (1.0 + e)
    comb_ref[...] = jnp.where(lane == i1, g1, 0.0) + jnp.where(lane == i2, g2, 0.0)


def _rmsnorm_router(x, g, router_w, router_b):
    m, d = x.shape
    tm = _pick(m, (256, 128, 64, 32, 16, 8))
    rw = jnp.pad(router_w, ((0, 0), (0, LANES - N_EXPERTS)))
    rb = jnp.pad(router_b.reshape(1, N_EXPERTS), ((0, 0), (0, LANES - N_EXPERTS)))
    return pl.pallas_call(
        _router_body,
        grid=(m // tm,),
        in_specs=[pl.BlockSpec((tm, d), lambda i: (i, 0)), pl.BlockSpec((1, d), lambda i: (0, 0)),
                  pl.BlockSpec((d, LANES), lambda i: (0, 0)), pl.BlockSpec((1, LANES), lambda i: (0, 0))],
        out_specs=[pl.BlockSpec((tm, d), lambda i: (i, 0)), pl.BlockSpec((tm, LANES), lambda i: (i, 0))],
        out_shape=[jax.ShapeDtypeStruct((m, d), BF16), jax.ShapeDtypeStruct((m, LANES), F32)],
        compiler_params=_cparams(("parallel",)),
        name="rmsnorm_router",
    )(x, g.reshape(1, d), rw, rb)


def _mm_body(a_ref, w_ref, o_ref):
    o_ref[...] = _dot(a_ref[...], w_ref[...].astype(BF16)).astype(o_ref.dtype)


def _mm(a, w, n_out=None):
    m, k = a.shape
    n = w.shape[1] if n_out is None else n_out
    tm = _pick(m, (1024, 512, 256, 128, 64, 32, 16, 8))
    tn = _pick(n, (512, 256, 128))
    return pl.pallas_call(
        _mm_body,
        grid=(m // tm, n // tn),
        in_specs=[pl.BlockSpec((tm, k), lambda i, j: (i, 0)), pl.BlockSpec((k, tn), lambda i, j: (0, j))],
        out_specs=pl.BlockSpec((tm, tn), lambda i, j: (i, j)),
        out_shape=jax.ShapeDtypeStruct((m, n), F32),
        compiler_params=_cparams(("parallel", "arbitrary")),
        name="proj_in",
    )(a, w)


def _mm2_res_body(a0_ref, a1_ref, w0_ref, w1_ref, r_ref, o_ref):
    acc = _dot(a0_ref[...], w0_ref[...].astype(BF16)) + _dot(a1_ref[...], w1_ref[...].astype(BF16))
    o_ref[...] = r_ref[...] + acc


def _mm2_res(a0, a1, w, res):
    m, k0 = a0.shape
    k1 = a1.shape[1]
    assert k0 == k1
    n = w.shape[1]
    tm = _pick(m, (1024, 512, 256, 128, 64, 32, 16, 8))
    tn = _pick(n, (512, 256, 128))
    return pl.pallas_call(
        _mm2_res_body,
        grid=(m // tm, n // tn),
        in_specs=[pl.BlockSpec((tm, k0), lambda i, j: (i, 0)), pl.BlockSpec((tm, k1), lambda i, j: (i, 0)),
                  pl.BlockSpec((k0, tn), lambda i, j: (0, j)), pl.BlockSpec((k1, tn), lambda i, j: (1, j)),
                  pl.BlockSpec((tm, tn), lambda i, j: (i, j))],
        out_specs=pl.BlockSpec((tm, tn), lambda i, j: (i, j)),
        out_shape=jax.ShapeDtypeStruct((m, n), F32),
        compiler_params=_cparams(("parallel", "arbitrary")),
        name="proj_out",
    )(a0, a1, w, w, res)


def _mm_res_body(a_ref, w_ref, r_ref, o_ref):
    o_ref[...] = r_ref[...] + _dot(a_ref[...], w_ref[...].astype(BF16))


def _mm_res(a, w, res):
    m, k = a.shape
    n = w.shape[1]
    tm = _pick(m, (1024, 512, 256, 128, 64, 32, 16, 8))
    tn = _pick(n, (512, 256, 128))
    return pl.pallas_call(
        _mm_res_body,
        grid=(m // tm, n // tn),
        in_specs=[pl.BlockSpec((tm, k), lambda i, j: (i, 0)), pl.BlockSpec((k, tn), lambda i, j: (0, j)),
                  pl.BlockSpec((tm, tn), lambda i, j: (i, j))],
        out_specs=pl.BlockSpec((tm, tn), lambda i, j: (i, j)),
        out_shape=jax.ShapeDtypeStruct((m, n), F32),
        compiler_params=_cparams(("parallel", "arbitrary")),
        name="proj_out1",
    )(a, w, res)


def _glu_up_body(a_ref, w1_ref, w3_ref, o_ref):
    a = a_ref[...]
    h1 = _dot(a, w1_ref[...].astype(BF16))
    h3 = _dot(a, w3_ref[...].astype(BF16))
    o_ref[...] = (h1 * _sigmoid(h1) * h3).astype(o_ref.dtype)


def _glu_up(a, w1, w3):
    m, k = a.shape
    n = w1.shape[1]
    tm = _pick(m, (1024, 512, 256, 128, 64, 32, 16, 8))
    tn = _pick(n, (256, 128))
    return pl.pallas_call(
        _glu_up_body,
        grid=(m // tm, n // tn),
        in_specs=[pl.BlockSpec((tm, k), lambda i, j: (i, 0)), pl.BlockSpec((k, tn), lambda i, j: (0, j)),
                  pl.BlockSpec((k, tn), lambda i, j: (0, j))],
        out_specs=pl.BlockSpec((tm, tn), lambda i, j: (i, j)),
        out_shape=jax.ShapeDtypeStruct((m, n), BF16),
        compiler_params=_cparams(("parallel", "arbitrary")),
        name="ffn_up",
    )(a, w1, w3)


def _moe_up_body(a_ref, comb_ref, w1_ref, w3_ref, o_ref):
    e = pl.program_id(1)
    a = a_ref[...]
    h1 = _dot(a, w1_ref[...].astype(BF16))
    h3 = _dot(a, w3_ref[...].astype(BF16))
    comb = comb_ref[...]
    gate = jnp.sum(jnp.where(_iota(comb.shape, 1) == e, comb, 0.0), axis=-1, keepdims=True)
    o_ref[...] = (h1 * _sigmoid(h1) * h3 * gate).astype(o_ref.dtype)


def _moe_up(a, comb, w1, w3):
    m, k = a.shape
    ne, _, f = w1.shape
    tm = _pick(m, (1024, 512, 256, 128, 64, 32, 16, 8))
    tn = _pick(f, (256, 128))
    nf = f // tn
    return pl.pallas_call(
        _moe_up_body,
        grid=(m // tm, ne, nf),
        in_specs=[pl.BlockSpec((tm, k), lambda i, e, j: (i, 0)), pl.BlockSpec((tm, LANES), lambda i, e, j: (i, 0)),
                  pl.BlockSpec((None, k, tn), lambda i, e, j: (e, 0, j)),
                  pl.BlockSpec((None, k, tn), lambda i, e, j: (e, 0, j))],
        out_specs=pl.BlockSpec((tm, tn), lambda i, e, j: (i, e * nf + j)),
        out_shape=jax.ShapeDtypeStruct((m, ne * f), BF16),
        compiler_params=_cparams(("parallel", "arbitrary", "arbitrary")),
        name="moe_up",
    )(a, comb, w1, w3)


def _mm_acc_body(a_ref, w_ref, r_ref, o_ref, *, k_total, tk):
    kk = pl.program_id(2)
    a = a_ref[...]
    w = w_ref[...]
    if k_total % tk != 0:
        valid = k_total - kk * tk
        a = jnp.where(_iota(a.shape, 1) < valid, a, jnp.zeros_like(a))
        w = jnp.where(_iota(w.shape, 0) < valid, w, 0.0)
    p = _dot(a, w.astype(BF16))

    @pl.when(kk == 0)
    def _():
        o_ref[...] = r_ref[...] + p

    @pl.when(kk > 0)
    def _():
        o_ref[...] += p


def _mm_acc(a, w, res):
    m, k = a.shape
    n = w.shape[1]
    tm = _pick(m, (2048, 1024, 512, 256, 128, 64, 32, 16, 8))
    tn = _pick(n, (1024, 512, 256, 128))
    tk = 512
    nk = pl.cdiv(k, tk)
    return pl.pallas_call(
        functools.partial(_mm_acc_body, k_total=k, tk=tk),
        grid=(m // tm, n // tn, nk),
        in_specs=[pl.BlockSpec((tm, tk), lambda i, j, kk: (i, kk)), pl.BlockSpec((tk, tn), lambda i, j, kk: (kk, j)),
                  pl.BlockSpec((tm, tn), lambda i, j, kk: (i, j))],
        out_specs=pl.BlockSpec((tm, tn), lambda i, j, kk: (i, j)),
        out_shape=jax.ShapeDtypeStruct((m, n), F32),
        compiler_params=_cparams(("parallel", "parallel", "arbitrary")),
        name="proj_down",
    )(a, w, res)


def _rwkv_body(zr_ref, zk_ref, zv_ref, zdw_ref, zda_ref, zdg_ref, mu_ref, sh_ref, s0_ref,
               w0_ref, wup_ref, a0_ref, aup_ref, gup_ref, kk_ref, ka_ref, rk_ref, lw_ref, lb_ref,
               y_ref, sT_ref, s_scr, carry_scr, *, rt, chunk, valid, seq_rows):
    n = pl.program_id(1)
    L = chunk
    L2 = 2 * L
    H = RWKV_HEAD
    nchunk = rt // L
    nsq = max(1, int(math.ceil(math.log2(L))))

    lane = _iota((1, LANES), 1)
    m0 = (lane < H).astype(F32)
    m1 = 1.0 - m0
    r2 = _iota((L2, L2), 0)
    c2 = _iota((L2, L2), 1)
    rh = jnp.where(r2 >= L, 1, 0)
    ch = jnp.where(c2 >= L, 1, 0)
    same = jnp.where(rh == ch, 1.0, 0.0)
    tdiff = (r2 - L * rh) - (c2 - L * ch)
    strict = same * jnp.where(tdiff > 0, 1.0, 0.0)
    incl = same * jnp.where(tdiff >= 0, 1.0, 0.0)
    rl = _iota((L, L), 0)
    cl = _iota((L, L), 1)
    tril = jnp.where(rl >= cl, 1.0, 0.0).astype(BF16)
    ri = _iota((LANES, LANES), 0)
    ci = _iota((LANES, LANES), 1)
    bd = jnp.where((ri >= H) == (ci >= H), 1.0, 0.0)
    bones = bd.astype(BF16)
    row = _iota((L, 1), 0)

    @pl.when(n == 0)
    def _():
        s_scr[...] = jnp.zeros_like(s_scr)
        carry_scr[...] = jnp.zeros_like(carry_scr)

    mu = mu_ref[...]
    w0 = w0_ref[...]
    a0 = a0_ref[...]
    kkw = kk_ref[...]
    kaw = ka_ref[...]
    rkw = rk_ref[...]
    lnw = lw_ref[...]
    lnb = lb_ref[...]

    def block_sum(x):
        return _dot_const_rhs(x, bones)

    def chunk_step(ci_, st):
        r0 = pl.multiple_of(ci_ * L, L)
        grow = n * rt + ci_ * L
        is_start = (grow % seq_rows) == 0
        is_end = ((grow + L) % seq_rows) == 0
        b_glob = grow // seq_rows
        b_loc = (ci_ * L) // seq_rows

        zc = jnp.concatenate([zr_ref[pl.ds(r0, L), :], zk_ref[pl.ds(r0, L), :], zv_ref[pl.ds(r0, L), :],
                              zdw_ref[pl.ds(r0, L), :], zda_ref[pl.ds(r0, L), :], zdg_ref[pl.ds(r0, L), :]], axis=1)
        first = jnp.where(is_start, sh_ref[pl.ds(b_glob, 1), :], carry_scr[0:1, :])
        zprev = jnp.where(row == 0, first, pltpu.roll(zc, 1, 0))
        carry_scr[0:1, :] = zc[valid - 1:valid, :]
        zs = zc + mu * (zprev - zc)
        r = zs[:, 0:LANES]
        k = zs[:, LANES:2 * LANES]
        v = zs[:, 2 * LANES:3 * LANES]
        dw = zs[:, 3 * LANES:4 * LANES]
        da = zs[:, 4 * LANES:5 * LANES]
        dg = zs[:, 5 * LANES:]

        w_log = -_softplus(-(w0 + _dot_hp(jnp.tanh(dw), wup_ref[...]))) - 0.5
        lw = -jnp.exp(w_log)
        a = _sigmoid(a0 + _dot_hp(da, aup_ref[...]))
        g = _dot_hp(_sigmoid(dg), gup_ref[...])
        kk = k * kkw
        k = k * (1.0 + (a - 1.0) * kaw)
        kk = kk / jnp.maximum(jnp.sqrt(block_sum(kk * kk)), 1e-12)
        ka = -kk
        kb = kk * a
        if valid < L:
            ok = row < valid
            lw = jnp.where(ok, lw, 0.0)
            r = jnp.where(ok, r, 0.0)
            k = jnp.where(ok, k, 0.0)
            v = jnp.where(ok, v, 0.0)
            ka = jnp.where(ok, ka, 0.0)
            kb = jnp.where(ok, kb, 0.0)

        s0 = s0_ref[b_loc, 0]
        s1 = s0_ref[b_loc, 1]
        zz = jnp.zeros((H, H), F32)
        s_init = jnp.concatenate([jnp.concatenate([s0, zz], axis=1), jnp.concatenate([zz, s1], axis=1)], axis=0)
        st = jnp.where(is_start, s_init, st)

        cum = _dot_const(tril, lw)
        ce = jnp.exp(cum)
        cx = jnp.exp(cum - lw)
        cinv = jnp.exp(-cum)
        at = cx * ka
        rt_ = ce * r
        bt = kb * cinv
        kt = k * cinv
        c_last = ce[L - 1:L, :]
        a2 = jnp.concatenate([at * m0, at * m1], axis=0)
        r2_ = jnp.concatenate([rt_ * m0, rt_ * m1], axis=0)
        b2 = jnp.concatenate([bt * m0, bt * m1], axis=0)
        k2 = jnp.concatenate([kt * m0, kt * m1], axis=0)
        v2 = jnp.concatenate([v, v], axis=0)

        nmat = _dot_hp(a2, b2, NT) * strict
        kmat = _dot_hp(a2, k2, NT) * strict
        x = _dot_hp(a2, st, NT) + _dot_hp(kmat, v2)
        p = nmat
        for q in range(nsq):
            x = x + _dot_hp(p, x)
            if q + 1 < nsq:
                p = _dot_hp(p, p)
        u = x[0:L, :] * m0 + x[L:L2, :] * m1
        u2 = jnp.concatenate([u, u], axis=0)
        rb = _dot_hp(r2_, b2, NT) * incl
        rkm = _dot_hp(r2_, k2, NT) * incl
        y2 = _dot_hp(r2_, st, NT) + _dot_hp(rb, u2) + _dot_hp(rkm, v2)
        y = y2[0:L, :] * m0 + y2[L:L2, :] * m1

        uv = jnp.concatenate([u, v], axis=0)
        bk = jnp.concatenate([bt * c_last, kt * c_last], axis=0)
        st = st * c_last + bd * _dot_hp(uv, bk, TN)

        mean = block_sum(y) * (1.0 / H)
        yc = y - mean
        var = block_sum(yc * yc) * (1.0 / H)
        yn = yc * lax.rsqrt(var + RWKV_LNX_EPS) * lnw + lnb
        bonus = block_sum(r * k * rkw) * v
        y_ref[pl.ds(r0, L), :] = ((yn + bonus) * g).astype(y_ref.dtype)

        @pl.when(is_end)
        def _():
            sT_ref[b_loc, 0] = st[0:H, 0:H]
            sT_ref[b_loc, 1] = st[H:2 * H, H:2 * H]

        return st

    st = lax.fori_loop(0, nchunk, chunk_step, s_scr[...])
    s_scr[...] = st


def _dot_const_rhs(a, c_bf16):
    h0 = a.astype(BF16)
    r1 = a - h0.astype(F32)
    h1 = r1.astype(BF16)
    h2 = (r1 - h1.astype(F32)).astype(BF16)
    return _dot(h0, c_bf16) + (_dot(h1, c_bf16) + _dot(h2, c_bf16))


def _rwkv(z, shift_p, wkv0, mu_p, w0, wup_p, a0, aup_p, g_up, k_k, k_a, r_k, lnx_w, lnx_b,
          *, n_seq, seq_rows, chunk, valid):
    m = z.shape[0]
    rt = _pick(m, (256, 128, 64, 32, 16, 8))
    if rt > seq_rows:
        assert rt % seq_rows == 0 and chunk == seq_rows
        nb = rt // seq_rows
        sblk = lambda c, n: (n, c, 0, 0)
    else:
        assert seq_rows % rt == 0
        nb = 1
        tiles_per_seq = seq_rows // rt
        sblk = lambda c, n: (n // tiles_per_seq, c, 0, 0)
    assert rt % chunk == 0
    npair = RWKV_HEADS // 2
    nrb = LANES

    def cat(a):
        rows = a.shape[0]
        rkv = a[:, :3 * MIX_A].reshape(rows, 3, npair, LANES).transpose(2, 0, 1, 3).reshape(npair, rows, 3 * LANES)
        tail = jnp.broadcast_to(a[None, :, 3 * MIX_A:], (npair, rows, ZE_U - 3 * MIX_A))
        return jnp.concatenate([rkv, tail], axis=2)

    mu_cat = cat(mu_p.reshape(1, ZE_U))
    sh_cat = cat(shift_p)
    row2 = lambda a: a.reshape(1, MIX_A)
    vec = pl.BlockSpec((1, LANES), lambda c, n: (0, c))
    kern = functools.partial(_rwkv_body, rt=rt, chunk=chunk, valid=valid, seq_rows=seq_rows)
    y, s_t = pl.pallas_call(
        kern,
        grid=(npair, m // rt),
        in_specs=[
            pl.BlockSpec((rt, nrb), lambda c, n: (n, c)),
            pl.BlockSpec((rt, nrb), lambda c, n: (n, npair + c)),
            pl.BlockSpec((rt, nrb), lambda c, n: (n, 2 * npair + c)),
            pl.BlockSpec((rt, RANK_PAD), lambda c, n: (n, ZE_DW // RANK_PAD)),
            pl.BlockSpec((rt, RANK_PAD), lambda c, n: (n, ZE_DA // RANK_PAD)),
            pl.BlockSpec((rt, RWKV_GATE_RANK), lambda c, n: (n, ZE_DG // RWKV_GATE_RANK)),
            pl.BlockSpec((None, 1, RW_CAT), lambda c, n: (c, 0, 0)),
            pl.BlockSpec((None, n_seq, RW_CAT), lambda c, n: (c, 0, 0)),
            pl.BlockSpec((nb, 2, RWKV_HEAD, RWKV_HEAD), sblk),
            vec,
            pl.BlockSpec((RANK_PAD, LANES), lambda c, n: (0, c)),
            vec,
            pl.BlockSpec((RANK_PAD, LANES), lambda c, n: (0, c)),
            pl.BlockSpec((RWKV_GATE_RANK, LANES), lambda c, n: (0, c)),
            vec, vec, vec, vec, vec,
        ],
        out_specs=[pl.BlockSpec((rt, LANES), lambda c, n: (n, c)),
                   pl.BlockSpec((nb, 2, RWKV_HEAD, RWKV_HEAD), sblk)],
        out_shape=[jax.ShapeDtypeStruct((m, MIX_A), BF16),
                   jax.ShapeDtypeStruct((n_seq, RWKV_HEADS, RWKV_HEAD, RWKV_HEAD), F32)],
        scratch_shapes=[pltpu.VMEM((LANES, LANES), F32), pltpu.VMEM((SUBLANES, RW_CAT), F32)],
        compiler_params=_cparams(("parallel", "arbitrary")),
        name="rwkv7",
    )(z, z, z, z, z, z, mu_cat, sh_cat, wkv0,
      row2(w0), wup_p, row2(a0), aup_p, g_up, row2(k_k), row2(k_a), row2(r_k), row2(lnx_w), row2(lnx_b))
    return y, s_t


def _s5_prep_body(are_ref, aim_ref, ls_ref, abre_ref, abim_ref, zr_ref, zi_ref):
    lam_re = are_ref[...]
    lam_im = aim_ref[...]
    dt = jnp.exp(ls_ref[...])
    mag = jnp.exp(lam_re * dt)
    ab_re = mag * jnp.cos(lam_im * dt)
    ab_im = mag * jnp.sin(lam_im * dt)
    inv = 1.0 / (lam_re * lam_re + lam_im * lam_im)
    abre_ref[...] = ab_re
    abim_ref[...] = ab_im
    zr_ref[...] = ((ab_re - 1.0) * lam_re + ab_im * lam_im) * inv
    zi_ref[...] = (ab_im * lam_re - (ab_re - 1.0) * lam_im) * inv


def _s5_prep(a_re, a_im, log_step):
    g, n = a_re.shape
    sd = jax.ShapeDtypeStruct((g, n), F32)
    return pl.pallas_call(_s5_prep_body, out_shape=[sd, sd, sd, sd], name="s5_discretise")(
        a_re, a_im, jnp.broadcast_to(log_step.reshape(g, 1), (g, n)))


def _gelu_tanh(x):
    return 0.5 * x * (1.0 + jnp.tanh(math.sqrt(2.0 / math.pi) * (x + 0.044715 * (x * x * x))))


def _s5_body(u_ref, wbr_ref, wbi_ref, wcr_ref, wci_ref, abr_ref, abi_ref, d_ref, h0r_ref, h0i_ref,
             y_ref, hTr_ref, hTi_ref, hr_scr, hi_scr, cr_scr, ci_scr, *, rt, seq_rows):
    n = pl.program_id(1)
    u = u_ref[...]
    ub = u.astype(BF16)
    hr_scr[...] = _dot(ub, wbr_ref[...].astype(BF16))
    hi_scr[...] = _dot(ub, wbi_ref[...].astype(BF16))
    ar = abr_ref[...]
    ai = abi_ref[...]

    @pl.when(n == 0)
    def _():
        cr_scr[...] = jnp.zeros_like(cr_scr)
        ci_scr[...] = jnp.zeros_like(ci_scr)

    def step(r, hr, hi):
        br = hr_scr[pl.ds(r, 1), :]
        bi = hi_scr[pl.ds(r, 1), :]
        nr = ar * hr - ai * hi + br
        ni = ar * hi + ai * hr + bi
        hr_scr[pl.ds(r, 1), :] = nr
        hi_scr[pl.ds(r, 1), :] = ni
        return nr, ni

    if seq_rows >= rt:
        b = (n * rt) // seq_rows
        is_start = ((n * rt) % seq_rows) == 0
        hr = jnp.where(is_start, h0r_ref[pl.ds(b, 1), :], cr_scr[0:1, :])
        hi = jnp.where(is_start, h0i_ref[pl.ds(b, 1), :], ci_scr[0:1, :])
        hr, hi = lax.fori_loop(0, rt, lambda r, c: step(r, c[0], c[1]), (hr, hi), unroll=8)
        cr_scr[0:1, :] = hr
        ci_scr[0:1, :] = hi
        hTr_ref[pl.ds(b, 1), :] = hr
        hTi_ref[pl.ds(b, 1), :] = hi
    else:
        nseq = rt // seq_rows

        def seq(s, carry):
            b = n * nseq + s
            hr = h0r_ref[pl.ds(b, 1), :]
            hi = h0i_ref[pl.ds(b, 1), :]
            for t in range(seq_rows):
                hr, hi = step(s * seq_rows + t, hr, hi)
            hTr_ref[pl.ds(b, 1), :] = hr
            hTi_ref[pl.ds(b, 1), :] = hi
            return carry

        lax.fori_loop(0, nseq, seq, 0)

    y = _dot(hr_scr[...].astype(BF16), wcr_ref[...].astype(BF16)) + _dot(hi_scr[...].astype(BF16), wci_ref[...].astype(BF16))
    y_ref[...] = _gelu_tanh(y + d_ref[...] * u)


def _s5_scan(z, col0, wbr, wbi, wcr, wci, ab_re, ab_im, d_skip, h0_re, h0_im, *, n_seq, seq_rows):
    m = z.shape[0]
    rt = _pick(m, (256, 128, 64, 32, 16, 8))
    assert seq_rows % rt == 0 or rt % seq_rows == 0
    cb0 = col0 // LANES
    st = S5_SLAB_STATE
    kern = functools.partial(_s5_body, rt=rt, seq_rows=seq_rows)
    wspec_b = pl.BlockSpec((None, LANES, st), lambda s, n: (s, 0, 0))
    wspec_c = pl.BlockSpec((None, st, LANES), lambda s, n: (s, 0, 0))
    vspec = pl.BlockSpec((1, st), lambda s, n: (0, s))
    hspec = pl.BlockSpec((n_seq, st), lambda s, n: (0, s))
    return pl.pallas_call(
        kern,
        grid=(S5_SLABS, m // rt),
        in_specs=[pl.BlockSpec((rt, LANES), lambda s, n: (n, cb0 + s)), wspec_b, wspec_b, wspec_c, wspec_c,
                  vspec, vspec, pl.BlockSpec((1, LANES), lambda s, n: (0, s)), hspec, hspec],
        out_specs=[pl.BlockSpec((rt, LANES), lambda s, n: (n, s)), hspec, hspec],
        out_shape=[jax.ShapeDtypeStruct((m, MIX_B), F32),
                   jax.ShapeDtypeStruct((n_seq, S5_GROUPS * S5_STATE), F32),
                   jax.ShapeDtypeStruct((n_seq, S5_GROUPS * S5_STATE), F32)],
        scratch_shapes=[pltpu.VMEM((rt, st), F32), pltpu.VMEM((rt, st), F32),
                        pltpu.VMEM((SUBLANES, st), F32), pltpu.VMEM((SUBLANES, st), F32)],
        compiler_params=_cparams(("parallel", "arbitrary")),
        name="s5_scan",
    )(z, wbr, wbi, wcr, wci, ab_re.reshape(1, -1), ab_im.reshape(1, -1), d_skip.reshape(1, -1), h0_re, h0_im)


def _glu_body(a_ref, w_ref, b_ref, y_ref, o_ref):
    t = _dot(a_ref[...].astype(BF16), w_ref[...].astype(BF16)) + b_ref[...]
    o_ref[...] = (y_ref[...] * _sigmoid(t)).astype(o_ref.dtype)


def _glu(y, w, b):
    m, k = y.shape
    n = w.shape[1]
    tm = _pick(m, (1024, 512, 256, 128, 64, 32, 16, 8))
    tn = _pick(n, (512, 256, 128))
    return pl.pallas_call(
        _glu_body,
        grid=(m // tm, n // tn),
        in_specs=[pl.BlockSpec((tm, k), lambda i, j: (i, 0)), pl.BlockSpec((k, tn), lambda i, j: (0, j)),
                  pl.BlockSpec((1, tn), lambda i, j: (0, j)), pl.BlockSpec((tm, tn), lambda i, j: (i, j))],
        out_specs=pl.BlockSpec((tm, tn), lambda i, j: (i, j)),
        out_shape=jax.ShapeDtypeStruct((m, n), BF16),
        compiler_params=_cparams(("parallel", "arbitrary")),
        name="s5_glu",
    )(y, w, b.reshape(1, n), y)


def _mlstm_body(q_ref, k_ref, v_ref, og_ref, gt_ref, gb_ref, nw_ref, c0_ref, n0_ref, m0_ref,
                y_ref, cT_ref, nT_ref, mT_ref, c_scr, n_scr, m_scr, *, rt, chunk, valid, seq_rows):
    h = pl.program_id(0)
    n = pl.program_id(1)
    L = chunk
    nchunk = rt // L
    rl = _iota((L, L), 0)
    cl = _iota((L, L), 1)
    causal = rl >= cl
    tril = jnp.where(causal, 1.0, 0.0).astype(BF16)
    row = _iota((L, 1), 0)
    lane = _iota((L, LANES), 1)
    gbias = gb_ref[...]
    nw = nw_ref[...]
    scale = MLSTM_DQK ** -0.5

    def chunk_step(ci_, carry):
        r0 = pl.multiple_of(ci_ * L, L)
        grow = n * rt + ci_ * L
        is_start = (grow % seq_rows) == 0
        is_end = ((grow + L) % seq_rows) == 0
        b_loc = (ci_ * L) // seq_rows

        @pl.when(is_start)
        def _():
            c_scr[...] = c0_ref[b_loc, 0]
            n_scr[0:1, :] = n0_ref[b_loc, 0]
            m_scr[0:1, :] = jnp.broadcast_to(m0_ref[b_loc, 0], (1, LANES))

        q = q_ref[pl.ds(r0, L), :]
        k = k_ref[pl.ds(r0, L), :] * scale
        v = v_ref[pl.ds(r0, L), :]
        gt = gt_ref[pl.ds(r0, L), :] + gbias
        ig_raw = jnp.sum(jnp.where(lane == h, gt, 0.0), axis=-1, keepdims=True)
        fg_raw = jnp.sum(jnp.where(lane == MLSTM_HEADS + h, gt, 0.0), axis=-1, keepdims=True)
        ig = GATE_CAP * jnp.tanh(ig_raw / GATE_CAP)
        lf = -_softplus(-(GATE_CAP * jnp.tanh(fg_raw / GATE_CAP)))
        if valid < L:
            ok = row < valid
            ig = jnp.where(ok, ig, NEG_BIG)
            lf = jnp.where(ok, lf, 0.0)
        bcol = _dot_const(tril, jnp.broadcast_to(lf, (L, LANES)))[:, 0:1]
        pack = jnp.where(lane == 0, ig, jnp.where(lane == 1, bcol, 0.0))
        pack_t = pack.T
        ig_row = pack_t[0:1, :]
        b_row = pack_t[1:2, :]
        m_prev = m_scr[0:1, 0:1]
        log_d = jnp.where(causal, bcol - b_row + ig_row, NEG_BIG)
        log_p = bcol + m_prev
        m_tok = jnp.maximum(log_p, jnp.max(log_d, axis=-1, keepdims=True))
        d = jnp.exp(log_d - m_tok)
        p = jnp.exp(log_p - m_tok)
        qb = q.astype(BF16)
        kb = k.astype(BF16)
        c_old = c_scr[...]
        n_old = n_scr[0:1, :]
        s = _dot(qb, kb, NT) * d
        num = _dot(s.astype(BF16), v.astype(BF16)) + p * _dot(qb, c_old.astype(BF16), NT)
        den = jnp.sum(s, axis=-1, keepdims=True) + p * jnp.sum(q * n_old, axis=-1, keepdims=True)
        hh = num / jnp.maximum(jnp.abs(den), jnp.exp(-m_tok))
        m_new = m_tok[L - 1:L, :]
        b_last = bcol[L - 1:L, :]
        w_col = jnp.exp(b_last - bcol + ig - m_new)
        cs = jnp.exp(b_last + m_prev - m_new)
        c_scr[...] = cs * c_old + _dot((v * w_col).astype(BF16), kb, TN)
        n_scr[0:1, :] = cs * n_old + jnp.sum(w_col * k, axis=0, keepdims=True)
        m_scr[0:1, :] = jnp.broadcast_to(m_new, (1, LANES))

        hn = hh * lax.rsqrt(jnp.mean(hh * hh, axis=-1, keepdims=True) + RMS_EPS)
        y_ref[pl.ds(r0, L), :] = (hn * nw * _sigmoid(og_ref[pl.ds(r0, L), :])).astype(y_ref.dtype)

        @pl.when(is_end)
        def _():
            cT_ref[b_loc, 0] = c_scr[...]
            nT_ref[b_loc, 0] = n_scr[0:1, :]
            mT_ref[b_loc, 0] = m_scr[0:1, 0:1]

        return carry

    lax.fori_loop(0, nchunk, chunk_step, 0)


def _mlstm(z, c0, n0, m0, b_i, b_f, norm_w, *, n_seq, seq_rows, chunk, valid):
    m = z.shape[0]
    nh, dqk, dv = MLSTM_HEADS, MLSTM_DQK, MLSTM_DV
    if seq_rows >= 64:
        rt = _pick(m, (256, 128, 64))
        assert seq_rows % rt == 0
        nb = 1
        tiles_per_seq = seq_rows // rt
        sblk = lambda h, n: (n // tiles_per_seq, h, 0, 0)
    else:
        assert chunk == seq_rows
        rt = _pick(m, (8 * seq_rows, seq_rows))
        nb = rt // seq_rows
        sblk = lambda h, n: (n, h, 0, 0)
    gbias = jnp.pad(jnp.concatenate([b_i, b_f]).reshape(1, 2 * nh), ((0, 0), (0, LANES - 2 * nh)))
    n0r = n0.reshape(n_seq, nh, 1, dqk)
    m0r = m0.reshape(n_seq, nh, 1, 1)
    kern = functools.partial(_mlstm_body, rt=rt, chunk=chunk, valid=valid, seq_rows=seq_rows)
    y, c_t, n_t, m_t = pl.pallas_call(
        kern,
        grid=(nh, m // rt),
        in_specs=[
            pl.BlockSpec((rt, dqk), lambda h, n: (n, h)),
            pl.BlockSpec((rt, dqk), lambda h, n: (n, nh + h)),
            pl.BlockSpec((rt, dv), lambda h, n: (n, (2 * nh * dqk) // dv + h)),
            pl.BlockSpec((rt, dv), lambda h, n: (n, ZO_OG // dv + h)),
            pl.BlockSpec((rt, LANES), lambda h, n: (n, ZO_G // LANES)),
            pl.BlockSpec((1, LANES), lambda h, n: (0, 0)),
            pl.BlockSpec((1, dv), lambda h, n: (0, h)),
            pl.BlockSpec((nb, 1, dv, dqk), sblk),
            pl.BlockSpec((nb, 1, 1, dqk), sblk),
            pl.BlockSpec((nb, 1, 1, 1), sblk),
        ],
        out_specs=[pl.BlockSpec((rt, dv), lambda h, n: (n, h)),
                   pl.BlockSpec((nb, 1, dv, dqk), sblk),
                   pl.BlockSpec((nb, 1, 1, dqk), sblk),
                   pl.BlockSpec((nb, 1, 1, 1), sblk)],
        out_shape=[jax.ShapeDtypeStruct((m, nh * dv), BF16),
                   jax.ShapeDtypeStruct((n_seq, nh, dv, dqk), F32),
                   jax.ShapeDtypeStruct((n_seq, nh, 1, dqk), F32),
                   jax.ShapeDtypeStruct((n_seq, nh, 1, 1), F32)],
        scratch_shapes=[pltpu.VMEM((dv, dqk), F32), pltpu.VMEM((SUBLANES, dqk), F32), pltpu.VMEM((SUBLANES, LANES), F32)],
        compiler_params=_cparams(("parallel", "arbitrary")),
        name="mlstm",
    )(z, z, z, z, z, gbias, norm_w.reshape(1, nh * dv), c0, n0r, m0r)
    return y, c_t, n_t.reshape(n_seq, nh, dqk), m_t.reshape(n_seq, nh)


def _pad_cols(a, n):
    return jnp.pad(a, ((0, 0), (0, n - a.shape[1])))


def _rwkv_cols(a):
    c1 = 3 * MIX_A
    c2 = c1 + RWKV_DECAY_RANK
    c3 = c2 + RWKV_ICL_RANK
    return jnp.concatenate([a[:, :c1], _pad_cols(a[:, c1:c2], RANK_PAD), _pad_cols(a[:, c2:c3], RANK_PAD), a[:, c3:]], axis=1)


def _rwkv_cols_inv(a):
    return jnp.concatenate([a[:, :ZE_DW + RWKV_DECAY_RANK], a[:, ZE_DA:ZE_DA + RWKV_ICL_RANK], a[:, ZE_DG:ZE_U]], axis=1)


def _pad_seq(a, n_seq, t, tp):
    if tp == t:
        return a
    w = a.shape[1]
    return jnp.pad(a.reshape(n_seq, t, w), ((0, 0), (0, tp - t), (0, 0))).reshape(n_seq * tp, w)


def _unpad_seq(a, n_seq, t, tp):
    if tp == t:
        return a
    w = a.shape[1]
    return a.reshape(n_seq, tp, w)[:, :t].reshape(n_seq * t, w)


def _s5_block_weights(zr, zi, b_re, b_im, c_re, c_im):
    g, n, c = S5_GROUPS, S5_STATE, S5_GROUP
    sg = S5_SLAB_GROUPS
    bb_re = zr[..., None] * b_re - zi[..., None] * b_im
    bb_im = zr[..., None] * b_im + zi[..., None] * b_re
    eye = jnp.eye(sg, dtype=F32)

    def in_blocks(bb):
        t = bb.reshape(S5_SLABS, sg, n, c)
        return jnp.einsum('sgnc,gh->sgchn', t, eye).reshape(S5_SLABS, sg * c, sg * n)

    def out_blocks(cc):
        t = cc.reshape(S5_SLABS, sg, c, n)
        return jnp.einsum('sgcn,gh->sgnhc', t, eye).reshape(S5_SLABS, sg * n, sg * c)

    return in_blocks(bb_re), in_blocks(bb_im), out_blocks(c_re), out_blocks(-c_im)


def _trunk(x, n_seq, t, st, w):
    tp = t if t >= MLSTM_CHUNK else SUBLANES * ((t + SUBLANES - 1) // SUBLANES)
    chunk = MLSTM_CHUNK if t >= MLSTM_CHUNK else tp
    valid = chunk if tp == t else t
    assert t % chunk == 0 or tp == chunk

    xn = _rmsnorm(x, w['ln_mix_e'], BF16)
    z = _mm(xn, w['w_in_e'])
    zp = _pad_seq(z, n_seq, t, tp)
    ya, wkv_t = _rwkv(zp, _rwkv_cols(st['shift']), st['wkv'], w['mu'], w['w0'], w['w_up'], w['a0'], w['a_up'], w['g_up'],
                      w['k_k'], w['k_a'], w['r_k'], w['lnx_w'], w['lnx_b'],
                      n_seq=n_seq, seq_rows=tp, chunk=chunk, valid=valid)
    ya = _unpad_seq(ya, n_seq, t, tp)
    shift_t = _rwkv_cols_inv(z.reshape(n_seq, t, ZE_W)[:, -1, :ZE_U])
    yb, re_t, im_t = _s5_scan(z, ZE_U, w['s5_wbr'], w['s5_wbi'], w['s5_wcr'], w['s5_wci'], w['s5_ab_re'], w['s5_ab_im'],
                              w['s5_d'], st['s5_re'], st['s5_im'], n_seq=n_seq, seq_rows=t)
    yb = _glu(yb, w['s5_glu_w'], w['s5_glu_b'])
    x = _mm2_res(ya, yb, w['w_out_e'], x)
    hn = _rmsnorm(x, w['ln_ffn_e'], BF16)
    hid = _glu_up(hn, w['ffn_w1'], w['ffn_w3'])
    x = _mm_acc(hid, w['ffn_w2'], x)

    xn = _rmsnorm(x, w['ln_mix_o'], BF16)
    zo = _mm(xn, w['w_in_o'])
    zop = _pad_seq(zo, n_seq, t, tp)
    yc, c_t, n_t, m_t = _mlstm(zop, st['c'], st['n'], st['m'], w['b_i'], w['b_f'], w['norm_w'],
                               n_seq=n_seq, seq_rows=tp, chunk=chunk, valid=valid)
    yc = _unpad_seq(yc, n_seq, t, tp)
    x = _mm_res(yc, w['w_out_o'], x)
    hn, comb = _rmsnorm_router(x, w['ln_ffn_o'], w['router_w'], w['router_b'])
    hid = _moe_up(hn, comb, w['exp_w1'], w['exp_w3'])
    x = _mm_acc(hid, w['exp_w2'], x)
    y = _rmsnorm(x, w['final_norm'], F32)

    g, ns = S5_GROUPS, S5_STATE
    return (y, wkv_t[None], shift_t[None], re_t.reshape(1, n_seq, g, ns), im_t.reshape(1, n_seq, g, ns),
            c_t[None], n_t[None], m_t[None])


def kernel(x_prompt, x_sample, state_rwkv_wkv, state_rwkv_shift, state_s5_re, state_s5_im, state_mlstm_c, state_mlstm_n, state_mlstm_m, ln_mix_e, w_in_e, rwkv_mu, rwkv_w0, rwkv_w_up, rwkv_a0, rwkv_a_up, rwkv_g_up, rwkv_k_k, rwkv_k_a, rwkv_r_k, rwkv_lnx_w, rwkv_lnx_b, s5_a_re, s5_a_im, s5_log_step, s5_b_re, s5_b_im, s5_c_re, s5_c_im, s5_d, s5_glu_w, s5_glu_b, w_out_e, ln_ffn_e, ffn_w1, ffn_w3, ffn_w2, ln_mix_o, w_in_o, mlstm_b_i, mlstm_b_f, mlstm_norm_w, w_out_o, ln_ffn_o, router_w, router_b, exp_w1, exp_w3, exp_w2, final_norm):
    assert ln_mix_e.shape[0] == 1 and ln_mix_o.shape[0] == 1
    d = D_MODEL
    ab_re, ab_im, zr, zi = _s5_prep(s5_a_re[0], s5_a_im[0], s5_log_step[0])
    wbr, wbi, wcr, wci = _s5_block_weights(zr, zi, s5_b_re[0], s5_b_im[0], s5_c_re[0], s5_c_im[0])
    wo = w_in_o[0]
    nh = MLSTM_HEADS
    w = {
        'ln_mix_e': ln_mix_e[0],
        'w_in_e': jnp.concatenate([_rwkv_cols(w_in_e[0][:, :RWKV_IN]), w_in_e[0][:, RWKV_IN:]], axis=1),
        'mu': _rwkv_cols(rwkv_mu[0].reshape(1, RWKV_IN)),
        'w0': rwkv_w0[0], 'a0': rwkv_a0[0],
        'w_up': jnp.pad(rwkv_w_up[0], ((0, RANK_PAD - RWKV_DECAY_RANK), (0, 0))),
        'a_up': jnp.pad(rwkv_a_up[0], ((0, RANK_PAD - RWKV_ICL_RANK), (0, 0))),
        'g_up': rwkv_g_up[0], 'k_k': rwkv_k_k[0], 'k_a': rwkv_k_a[0], 'r_k': rwkv_r_k[0].reshape(MIX_A),
        'lnx_w': rwkv_lnx_w[0], 'lnx_b': rwkv_lnx_b[0],
        's5_wbr': wbr, 's5_wbi': wbi, 's5_wcr': wcr, 's5_wci': wci,
        's5_ab_re': ab_re, 's5_ab_im': ab_im, 's5_d': s5_d[0], 's5_glu_w': s5_glu_w[0], 's5_glu_b': s5_glu_b[0],
        'w_out_e': w_out_e[0], 'ln_ffn_e': ln_ffn_e[0],
        'ffn_w1': ffn_w1[0], 'ffn_w3': ffn_w3[0], 'ffn_w2': ffn_w2[0],
        'ln_mix_o': ln_mix_o[0],
        'w_in_o': jnp.concatenate([wo[:, :ZO_OG], wo[:, ZO_OG + 2 * nh:], _pad_cols(wo[:, ZO_OG:ZO_OG + 2 * nh], 2 * LANES)], axis=1),
        'b_i': mlstm_b_i[0], 'b_f': mlstm_b_f[0], 'norm_w': mlstm_norm_w[0],
        'w_out_o': w_out_o[0], 'ln_ffn_o': ln_ffn_o[0],
        'router_w': router_w[0], 'router_b': router_b[0],
        'exp_w1': exp_w1[0], 'exp_w3': exp_w3[0], 'exp_w2': exp_w2[0].reshape(N_EXPERTS * EXPERT_FF, d),
        'final_norm': final_norm,
    }
    bp, tp_, _ = x_prompt.shape
    bs, ts_, _ = x_sample.shape
    g, ns = S5_GROUPS, S5_STATE
    zero = lambda *s: jnp.zeros(s, F32)
    st_p = {'wkv': zero(bp, RWKV_HEADS, RWKV_HEAD, RWKV_HEAD), 'shift': zero(bp, RWKV_IN),
            's5_re': zero(bp, g * ns), 's5_im': zero(bp, g * ns),
            'c': zero(bp, nh, MLSTM_DV, MLSTM_DQK), 'n': zero(bp, nh, MLSTM_DQK), 'm': zero(bp, nh)}
    st_s = {'wkv': state_rwkv_wkv[0], 'shift': state_rwkv_shift[0],
            's5_re': state_s5_re[0].reshape(bs, g * ns), 's5_im': state_s5_im[0].reshape(bs, g * ns),
            'c': state_mlstm_c[0], 'n': state_mlstm_n[0], 'm': state_mlstm_m[0]}
    out_p = _trunk(x_prompt.reshape(bp * tp_, d), bp, tp_, st_p, w)
    out_s = _trunk(x_sample.reshape(bs * ts_, d), bs, ts_, st_s, w)
    y_p = out_p[0].reshape(bp, tp_, d)
    y_s = out_s[0].reshape(bs, ts_, d)
    return (y_p, y_s) + tuple(out_p[1:]) + tuple(out_s[1:])
```

```python
import functools
import math

import jax
import jax.numpy as jnp
from jax import lax
from jax.experimental import pallas as pl
from jax.experimental.pallas import tpu as pltpu

F32 = jnp.float32
BF16 = jnp.bfloat16

D_MODEL = 4096
MIX_A = D_MODEL // 2
MIX_B = D_MODEL - MIX_A
RWKV_HEAD = 64
RWKV_HEADS = MIX_A // RWKV_HEAD
RWKV_DECAY_RANK = 96
RWKV_ICL_RANK = 96
RWKV_GATE_RANK = 256
RWKV_IN = 3 * MIX_A + RWKV_DECAY_RANK + RWKV_ICL_RANK + RWKV_GATE_RANK
RWKV_LNX_EPS = 1e-5 * RWKV_HEAD
S5_GROUP = 16
S5_GROUPS = MIX_B // S5_GROUP
S5_STATE = 64
MLSTM_HEADS = 8
MLSTM_DQK = D_MODEL // 16
MLSTM_DV = D_MODEL // 8
MLSTM_CHUNK = 64
GATE_CAP = 15.0
FFN_DIM = 11008
N_EXPERTS = 8
EXPERT_FF = D_MODEL // 2
RMS_EPS = 1e-6

LANES = 128
SUBLANES = 8
VMEM_LIMIT = 56 * 1024 * 1024

RANK_PAD = LANES
ZE_DW = 3 * MIX_A
ZE_DA = ZE_DW + RANK_PAD
ZE_DG = ZE_DA + RANK_PAD
ZE_U = ZE_DG + RWKV_GATE_RANK
ZE_W = ZE_U + MIX_B
RW_CAT = 3 * LANES + 2 * RANK_PAD + RWKV_GATE_RANK
ZO_OG = 2 * MLSTM_HEADS * MLSTM_DQK + MLSTM_HEADS * MLSTM_DV
ZO_G = ZO_OG + MLSTM_HEADS * MLSTM_DV
ZO_W = ZO_G + 2 * LANES

S5_SLAB_GROUPS = 8
S5_SLABS = S5_GROUPS // S5_SLAB_GROUPS
S5_SLAB_STATE = S5_SLAB_GROUPS * S5_STATE

NN = (((1,), (0,)), ((), ()))
NT = (((1,), (1,)), ((), ()))
TN = (((0,), (0,)), ((), ()))
NEG_BIG = -1e30


def _cparams(sem):
    return pltpu.CompilerParams(dimension_semantics=sem, vmem_limit_bytes=VMEM_LIMIT)


def _pick(n, cands):
    for c in cands:
        if n % c == 0:
            return c
    raise ValueError(f"no tile for {n}")


def _dot(a, b, dims=NN):
    return lax.dot_general(a, b, dims, preferred_element_type=F32)


def _split2(a):
    hi = a.astype(BF16)
    lo = (a - hi.astype(F32)).astype(BF16)
    return hi, lo


def _dot_hp(a, b, dims=NN):
    ah, al = _split2(a)
    bh, bl = _split2(b)
    return _dot(ah, bh, dims) + (_dot(ah, bl, dims) + _dot(al, bh, dims))


def _dot_const(c_bf16, a):
    h0 = a.astype(BF16)
    r1 = a - h0.astype(F32)
    h1 = r1.astype(BF16)
    h2 = (r1 - h1.astype(F32)).astype(BF16)
    return _dot(c_bf16, h0) + (_dot(c_bf16, h1) + _dot(c_bf16, h2))


def _sigmoid(x):
    return 1.0 / (1.0 + jnp.exp(-x))


def _softplus(x):
    return jnp.maximum(x, 0.0) + jnp.log(1.0 + jnp.exp(-jnp.abs(x)))


def _iota(shape, dim):
    return lax.broadcasted_iota(jnp.int32, shape, dim)


def _rmsnorm_body(x_ref, g_ref, o_ref):
    x = x_ref[...]
    ms = jnp.mean(x * x, axis=-1, keepdims=True)
    o_ref[...] = (x * lax.rsqrt(ms + RMS_EPS) * g_ref[...]).astype(o_ref.dtype)


def _rmsnorm(x, g, out_dtype):
    m, d = x.shape
    tm = _pick(m, (256, 128, 64, 32, 16, 8))
    return pl.pallas_call(
        _rmsnorm_body,
        grid=(m // tm,),
        in_specs=[pl.BlockSpec((tm, d), lambda i: (i, 0)), pl.BlockSpec((1, d), lambda i: (0, 0))],
        out_specs=pl.BlockSpec((tm, d), lambda i: (i, 0)),
        out_shape=jax.ShapeDtypeStruct((m, d), out_dtype),
        compiler_params=_cparams(("parallel",)),
        name="rmsnorm",
    )(x, g.reshape(1, d))


def _router_body(x_ref, g_ref, rw_ref, rb_ref, hn_ref, comb_ref):
    x = x_ref[...]
    ms = jnp.mean(x * x, axis=-1, keepdims=True)
    h = x * lax.rsqrt(ms + RMS_EPS) * g_ref[...]
    hn_ref[...] = h.astype(hn_ref.dtype)
    logits = _dot_hp(h, rw_ref[...]) + rb_ref[...]
    lane = _iota(logits.shape, 1)
    logits = jnp.where(lane < N_EXPERTS, logits, NEG_BIG)
    m1 = jnp.max(logits, axis=-1, keepdims=True)
    i1 = jnp.min(jnp.where(logits == m1, lane, LANES), axis=-1, keepdims=True)
    rest = jnp.where(lane == i1, NEG_BIG, logits)
    m2 = jnp.max(rest, axis=-1, keepdims=True)
    i2 = jnp.min(jnp.where(rest == m2, lane, LANES), axis=-1, keepdims=True)
    e = jnp.exp(m2 - m1)
    g1 = 1.0 / (1.0 + e)
    g2 = e / (1.0 + e)
    comb_ref[...] = jnp.where(lane == i1, g1, 0.0) + jnp.where(lane == i2, g2, 0.0)


def _rmsnorm_router(x, g, router_w, router_b):
    m, d = x.shape
    tm = _pick(m, (256, 128, 64, 32, 16, 8))
    rw = jnp.pad(router_w, ((0, 0), (0, LANES - N_EXPERTS)))
    rb = jnp.pad(router_b.reshape(1, N_EXPERTS), ((0, 0), (0, LANES - N_EXPERTS)))
    return pl.pallas_call(
        _router_body,
        grid=(m // tm,),
        in_specs=[pl.BlockSpec((tm, d), lambda i: (i, 0)), pl.BlockSpec((1, d), lambda i: (0, 0)),
                  pl.BlockSpec((d, LANES), lambda i: (0, 0)), pl.BlockSpec((1, LANES), lambda i: (0, 0))],
        out_specs=[pl.BlockSpec((tm, d), lambda i: (i, 0)), pl.BlockSpec((tm, LANES), lambda i: (i, 0))],
        out_shape=[jax.ShapeDtypeStruct((m, d), BF16), jax.ShapeDtypeStruct((m, LANES), F32)],
        compiler_params=_cparams(("parallel",)),
        name="rmsnorm_router",
    )(x, g.reshape(1, d), rw, rb)


def _mm_body(a_ref, w_ref, o_ref):
    o_ref[...] = _dot(a_ref[...], w_ref[...].astype(BF16)).astype(o_ref.dtype)


def _mm(a, w, n_out=None):
    m, k = a.shape
    n = w.shape[1] if n_out is None else n_out
    tm = _pick(m, (1024, 512, 256, 128, 64, 32, 16, 8))
    tn = _pick(n, (512, 256, 128))
    return pl.pallas_call(
        _mm_body,
        grid=(m // tm, n // tn),
        in_specs=[pl.BlockSpec((tm, k), lambda i, j: (i, 0)), pl.BlockSpec((k, tn), lambda i, j: (0, j))],
        out_specs=pl.BlockSpec((tm, tn), lambda i, j: (i, j)),
        out_shape=jax.ShapeDtypeStruct((m, n), F32),
        compiler_params=_cparams(("parallel", "arbitrary")),
        name="proj_in",
    )(a, w)


def _mm2_res_body(a0_ref, a1_ref, w0_ref, w1_ref, r_ref, o_ref):
    acc = _dot(a0_ref[...], w0_ref[...].astype(BF16)) + _dot(a1_ref[...], w1_ref[...].astype(BF16))
    o_ref[...] = r_ref[...] + acc


def _mm2_res(a0, a1, w, res):
    m, k0 = a0.shape
    k1 = a1.shape[1]
    assert k0 == k1
    n = w.shape[1]
    tm = _pick(m, (1024, 512, 256, 128, 64, 32, 16, 8))
    tn = _pick(n, (512, 256, 128))
    return pl.pallas_call(
        _mm2_res_body,
        grid=(m // tm, n // tn),
        in_specs=[pl.BlockSpec((tm, k0), lambda i, j: (i, 0)), pl.BlockSpec((tm, k1), lambda i, j: (i, 0)),
                  pl.BlockSpec((k0, tn), lambda i, j: (0, j)), pl.BlockSpec((k1, tn), lambda i, j: (1, j)),
                  pl.BlockSpec((tm, tn), lambda i, j: (i, j))],
        out_specs=pl.BlockSpec((tm, tn), lambda i, j: (i, j)),
        out_shape=jax.ShapeDtypeStruct((m, n), F32),
        compiler_params=_cparams(("parallel", "arbitrary")),
        name="proj_out",
    )(a0, a1, w, w, res)


def _mm_res_body(a_ref, w_ref, r_ref, o_ref):
    o_ref[...] = r_ref[...] + _dot(a_ref[...], w_ref[...].astype(BF16))


def _mm_res(a, w, res):
    m, k = a.shape
    n = w.shape[1]
    tm = _pick(m, (1024, 512, 256, 128, 64, 32, 16, 8))
    tn = _pick(n, (512, 256, 128))
    return pl.pallas_call(
        _mm_res_body,
        grid=(m // tm, n // tn),
        in_specs=[pl.BlockSpec((tm, k), lambda i, j: (i, 0)), pl.BlockSpec((k, tn), lambda i, j: (0, j)),
                  pl.BlockSpec((tm, tn), lambda i, j: (i, j))],
        out_specs=pl.BlockSpec((tm, tn), lambda i, j: (i, j)),
        out_shape=jax.ShapeDtypeStruct((m, n), F32),
        compiler_params=_cparams(("parallel", "arbitrary")),
        name="proj_out1",
    )(a, w, res)


def _glu_up_body(a_ref, w1_ref, w3_ref, o_ref):
    a = a_ref[...]
    h1 = _dot(a, w1_ref[...].astype(BF16))
    h3 = _dot(a, w3_ref[...].astype(BF16))
    o_ref[...] = (h1 * _sigmoid(h1) * h3).astype(o_ref.dtype)


def _glu_up(a, w1, w3):
    m, k = a.shape
    n = w1.shape[1]
    tm = _pick(m, (1024, 512, 256, 128, 64, 32, 16, 8))
    tn = _pick(n, (256, 128))
    return pl.pallas_call(
        _glu_up_body,
        grid=(m // tm, n // tn),
        in_specs=[pl.BlockSpec((tm, k), lambda i, j: (i, 0)), pl.BlockSpec((k, tn), lambda i, j: (0, j)),
                  pl.BlockSpec((k, tn), lambda i, j: (0, j))],
        out_specs=pl.BlockSpec((tm, tn), lambda i, j: (i, j)),
        out_shape=jax.ShapeDtypeStruct((m, n), BF16),
        compiler_params=_cparams(("parallel", "arbitrary")),
        name="ffn_up",
    )(a, w1, w3)


def _moe_up_body(a_ref, comb_ref, w1_ref, w3_ref, o_ref):
    e = pl.program_id(1)
    a = a_ref[...]
    h1 = _dot(a, w1_ref[...].astype(BF16))
    h3 = _dot(a, w3_ref[...].astype(BF16))
    comb = comb_ref[...]
    gate = jnp.sum(jnp.where(_iota(comb.shape, 1) == e, comb, 0.0), axis=-1, keepdims=True)
    o_ref[...] = (h1 * _sigmoid(h1) * h3 * gate).astype(o_ref.dtype)


def _moe_up(a, comb, w1, w3):
    m, k = a.shape
    ne, _, f = w1.shape
    tm = _pick(m, (1024, 512, 256, 128, 64, 32, 16, 8))
    tn = _pick(f, (256, 128))
    nf = f // tn
    return pl.pallas_call(
        _moe_up_body,
        grid=(m // tm, ne, nf),
        in_specs=[pl.BlockSpec((tm, k), lambda i, e, j: (i, 0)), pl.BlockSpec((tm, LANES), lambda i, e, j: (i, 0)),
                  pl.BlockSpec((None, k, tn), lambda i, e, j: (e, 0, j)),
                  pl.BlockSpec((None, k, tn), lambda i, e, j: (e, 0, j))],
        out_specs=pl.BlockSpec((tm, tn), lambda i, e, j: (i, e * nf + j)),
        out_shape=jax.ShapeDtypeStruct((m, ne * f), BF16),
        compiler_params=_cparams(("parallel", "arbitrary", "arbitrary")),
        name="moe_up",
    )(a, comb, w1, w3)


def _mm_acc_body(a_ref, w_ref, r_ref, o_ref, *, k_total, tk):
    kk = pl.program_id(2)
    a = a_ref[...]
    w = w_ref[...]
    if k_total % tk != 0:
        valid = k_total - kk * tk
        a = jnp.where(_iota(a.shape, 1) < valid, a, jnp.zeros_like(a))
        w = jnp.where(_iota(w.shape, 0) < valid, w, 0.0)
    p = _dot(a, w.astype(BF16))

    @pl.when(kk == 0)
    def _():
        o_ref[...] = r_ref[...] + p

    @pl.when(kk > 0)
    def _():
        o_ref[...] += p


def _mm_acc(a, w, res):
    m, k = a.shape
    n = w.shape[1]
    tm = _pick(m, (2048, 1024, 512, 256, 128, 64, 32, 16, 8))
    tn = _pick(n, (1024, 512, 256, 128))
    tk = 512
    nk = pl.cdiv(k, tk)
    return pl.pallas_call(
        functools.partial(_mm_acc_body, k_total=k, tk=tk),
        grid=(m // tm, n // tn, nk),
        in_specs=[pl.BlockSpec((tm, tk), lambda i, j, kk: (i, kk)), pl.BlockSpec((tk, tn), lambda i, j, kk: (kk, j)),
                  pl.BlockSpec((tm, tn), lambda i, j, kk: (i, j))],
        out_specs=pl.BlockSpec((tm, tn), lambda i, j, kk: (i, j)),
        out_shape=jax.ShapeDtypeStruct((m, n), F32),
        compiler_params=_cparams(("parallel", "parallel", "arbitrary")),
        name="proj_down",
    )(a, w, res)


def _rwkv_body(zr_ref, zk_ref, zv_ref, zdw_ref, zda_ref, zdg_ref, mu_ref, sh_ref, s0_ref,
               w0_ref, wup_ref, a0_ref, aup_ref, gup_ref, kk_ref, ka_ref, rk_ref, lw_ref, lb_ref,
               y_ref, sT_ref, s_scr, carry_scr, *, bb, rt, chunk, valid, unroll):
    n = pl.program_id(2)
    n_last = pl.num_programs(2) - 1
    L = chunk
    L2 = 2 * L
    H = RWKV_HEAD
    nchunk = rt // L
    nsq = max(1, int(math.ceil(math.log2(L))))

    lane = _iota((1, LANES), 1)
    m0 = (lane < H).astype(F32)
    m1 = 1.0 - m0
    r2 = _iota((L2, L2), 0)
    c2 = _iota((L2, L2), 1)
    rh = jnp.where(r2 >= L, 1, 0)
    ch = jnp.where(c2 >= L, 1, 0)
    same = jnp.where(rh == ch, 1.0, 0.0)
    tdiff = (r2 - L * rh) - (c2 - L * ch)
    strict = same * jnp.where(tdiff > 0, 1.0, 0.0)
    incl = same * jnp.where(tdiff >= 0, 1.0, 0.0)
    rl = _iota((L, L), 0)
    cl = _iota((L, L), 1)
    tril = jnp.where(rl >= cl, 1.0, 0.0).astype(BF16)
    ri = _iota((LANES, LANES), 0)
    ci = _iota((LANES, LANES), 1)
    bd = jnp.where((ri >= H) == (ci >= H), 1.0, 0.0)
    bones = bd.astype(BF16)
    row = _iota((L, 1), 0)

    @pl.when(n == 0)
    def _():
        s_scr[...] = jnp.zeros_like(s_scr)
        carry_scr[...] = jnp.zeros_like(carry_scr)

    mu = mu_ref[...]
    w0 = w0_ref[...]
    a0 = a0_ref[...]
    kkw = kk_ref[...]
    kaw = ka_ref[...]
    rkw = rk_ref[...]
    lnw = lw_ref[...]
    lnb = lb_ref[...]

    def block_sum(x):
        return _dot_const_rhs(x, bones)

    def load(seq, r0, is_start):
        zc = jnp.concatenate([ref[seq, pl.ds(r0, L), :] for ref in (zr_ref, zk_ref, zv_ref, zdw_ref, zda_ref, zdg_ref)],
                             axis=1)
        first = jnp.where(is_start, sh_ref[pl.ds(seq, 1), :], carry_scr[seq, 0:1, :])
        s0 = s0_ref[seq, 0]
        s1 = s0_ref[seq, 1]
        zz = jnp.zeros((H, H), F32)
        s_init = jnp.concatenate([jnp.concatenate([s0, zz], axis=1), jnp.concatenate([zz, s1], axis=1)], axis=0)
        st = jnp.where(is_start, s_init, s_scr[seq])
        return zc, first, st

    def recur(seqs):
        pre = []
        for r, k, v, ka, kb, lw, cum, st in seqs:
            ce = jnp.exp(cum)
            cinv = jnp.exp(-cum)
            at = jnp.exp(cum - lw) * ka
            rt_ = ce * r
            bt = kb * cinv
            kt = k * cinv
            c_last = ce[L - 1:L, :]
            lhs = jnp.concatenate([at * m0, at * m1, rt_ * m0, rt_ * m1], axis=0)
            rhs = jnp.concatenate([bt * m0, bt * m1, kt * m0, kt * m1, st], axis=0)
            bk = jnp.concatenate([bt * c_last, kt * c_last], axis=0)
            pre.append((lhs, rhs, bk, c_last))
        gms = [_dot_x(p_[0], p_[1], NT) for p_ in pre]
        wvs = [_dot_x(jnp.concatenate([gm[0:L2, L2:2 * L2] * strict, gm[L2:, L2:2 * L2] * incl], axis=0),
                      jnp.concatenate([s[2], s[2]], axis=0)) for gm, s in zip(gms, seqs)]
        xs = [gm[0:L2, 2 * L2:] + wv[0:L2] for gm, wv in zip(gms, wvs)]
        ps = [gm[0:L2, 0:L2] * strict for gm in gms]
        for q in range(nsq):
            if q + 1 < nsq:
                xps = [_dot_x(p, jnp.concatenate([x, p], axis=1)) for x, p in zip(xs, ps)]
                xs = [x + xp[:, 0:LANES] for x, xp in zip(xs, xps)]
                ps = [xp[:, LANES:] for xp in xps]
            else:
                xs = [x + _dot_x(p, x) for x, p in zip(xs, ps)]
        us = [x[0:L, :] * m0 + x[L:L2, :] * m1 for x in xs]
        rus = [_dot_x(gm[L2:, 0:L2] * incl, jnp.concatenate([u, u], axis=0)) for gm, u in zip(gms, us)]
        sus = [_dot_x(jnp.concatenate([u, s[2]], axis=0), p_[2], TN) for u, s, p_ in zip(us, seqs, pre)]
        out = []
        for gm, wv, ru, su, s, p_ in zip(gms, wvs, rus, sus, seqs, pre):
            y2 = gm[L2:, 2 * L2:] + wv[L2:] + ru
            out.append((y2[0:L, :] * m0 + y2[L:L2, :] * m1, s[7] * p_[3] + bd * su))
        return out

    def compute_all(loaded):
        nu = len(loaded)
        stack = lambda xs: xs[0] if nu == 1 else jnp.concatenate(xs, axis=0)
        part = lambda x, u: x[u * L:(u + 1) * L]
        zc = stack([x[0] for x in loaded])
        zprev = stack([jnp.where(row == 0, x[1], pltpu.roll(x[0], 1, 0)) for x in loaded])
        zs = zc + mu * (zprev - zc)
        r = zs[:, 0:LANES]
        k = zs[:, LANES:2 * LANES]
        v = zs[:, 2 * LANES:3 * LANES]
        dw = zs[:, 3 * LANES:4 * LANES]
        da = zs[:, 4 * LANES:5 * LANES]
        dg = zs[:, 5 * LANES:]

        w_log = -_softplus(-(w0 + _dot_hp(jnp.tanh(dw), wup_ref[...]))) - 0.5
        lw = -jnp.exp(w_log)
        a = _sigmoid(a0 + _dot_hp(da, aup_ref[...]))
        g = _dot_hp(_sigmoid(dg), gup_ref[...])
        kk = k * kkw
        k = k * (1.0 + (a - 1.0) * kaw)
        rows = nu * L
        sums = block_sum(jnp.concatenate([kk * kk, r * k * rkw], axis=0))
        kk = kk / jnp.maximum(jnp.sqrt(sums[0:rows]), 1e-12)
        bonus = sums[rows:] * v
        ka = -kk
        kb = kk * a
        if valid < L:
            ok = stack([row < valid] * nu)
            lw = jnp.where(ok, lw, 0.0)
            r = jnp.where(ok, r, 0.0)
            k = jnp.where(ok, k, 0.0)
            v = jnp.where(ok, v, 0.0)
            ka = jnp.where(ok, ka, 0.0)
            kb = jnp.where(ok, kb, 0.0)

        lw_wide = lw if nu == 1 else jnp.concatenate([part(lw, u) for u in range(nu)], axis=1)
        cum_wide = _dot_const(tril, lw_wide)
        res = recur([(part(r, u), part(k, u), part(v, u), part(ka, u), part(kb, u), part(lw, u),
                      cum_wide[:, u * LANES:(u + 1) * LANES], loaded[u][2]) for u in range(nu)])
        sts = [x[1] for x in res]
        y = stack([x[0] for x in res])
        mean = block_sum(y) * (1.0 / H)
        yc = y - mean
        var = block_sum(yc * yc) * (1.0 / H)
        yn = yc * lax.rsqrt(var + RWKV_LNX_EPS) * lnw + lnb
        out = (yn + bonus) * g
        return [(part(out, u).astype(y_ref.dtype), sts[u], loaded[u][0][valid - 1:valid, :]) for u in range(nu)]

    def store(seq, r0, is_end, y, st, last_row):
        y_ref[seq, pl.ds(r0, L), :] = y
        s_scr[seq] = st
        carry_scr[seq, 0:1, :] = last_row

        @pl.when(is_end)
        def _():
            sT_ref[seq, 0] = st[0:H, 0:H]
            sT_ref[seq, 1] = st[H:2 * H, H:2 * H]

    def step(it, carry):
        sg = it // nchunk
        ci_ = it - sg * nchunk
        r0 = pl.multiple_of(ci_ * L, L)
        is_start = jnp.logical_and(n == 0, ci_ == 0)
        is_end = jnp.logical_and(n == n_last, ci_ == nchunk - 1)
        seqs = [sg * unroll + u for u in range(unroll)]
        loaded = [load(s, r0, is_start) for s in seqs]
        done = compute_all(loaded)
        for s, d in zip(seqs, done):
            store(s, r0, is_end, *d)
        return carry

    lax.fori_loop(0, (bb // unroll) * nchunk, step, 0)


def _dot_x(a, b, dims=NN):
    return _dot(a.astype(BF16), b.astype(BF16), dims)


def _dot_const_rhs(a, c_bf16):
    h0 = a.astype(BF16)
    r1 = a - h0.astype(F32)
    h1 = r1.astype(BF16)
    h2 = (r1 - h1.astype(F32)).astype(BF16)
    return _dot(h0, c_bf16) + (_dot(h1, c_bf16) + _dot(h2, c_bf16))


def _rwkv(z, shift_p, wkv0, mu_p, w0, wup_p, a0, aup_p, g_up, k_k, k_a, r_k, lnx_w, lnx_b,
          *, n_seq, seq_rows, chunk, valid):
    unroll = _pick(n_seq, (4, 2, 1))
    if seq_rows > chunk:
        rt = _pick(seq_rows, (256, 128, 64))
        bb = unroll
    else:
        rt = seq_rows
        bb = _pick(n_seq, (32, 16, 8, 4, 2, 1))
    assert rt % chunk == 0 and bb % unroll == 0 and (bb == n_seq or bb % SUBLANES == 0)
    sblk = lambda c, g, n: (g, c, 0, 0)
    npair = RWKV_HEADS // 2
    nrb = LANES

    def cat(a):
        rows = a.shape[0]
        rkv = a[:, :3 * MIX_A].reshape(rows, 3, npair, LANES).transpose(2, 0, 1, 3).reshape(npair, rows, 3 * LANES)
        tail = jnp.broadcast_to(a[None, :, 3 * MIX_A:], (npair, rows, ZE_U - 3 * MIX_A))
        return jnp.concatenate([rkv, tail], axis=2)

    mu_cat = cat(mu_p.reshape(1, ZE_U))
    sh_cat = cat(shift_p)
    row2 = lambda a: a.reshape(1, MIX_A)
    vec = pl.BlockSpec((1, LANES), lambda c, g, n: (0, c))
    kern = functools.partial(_rwkv_body, bb=bb, rt=rt, chunk=chunk, valid=valid, unroll=unroll)
    y, s_t = pl.pallas_call(
        kern,
        grid=(npair, n_seq // bb, seq_rows // rt),
        in_specs=[
            pl.BlockSpec((bb, rt, nrb), lambda c, g, n: (g, n, c)),
            pl.BlockSpec((bb, rt, nrb), lambda c, g, n: (g, n, npair + c)),
            pl.BlockSpec((bb, rt, nrb), lambda c, g, n: (g, n, 2 * npair + c)),
            pl.BlockSpec((bb, rt, RANK_PAD), lambda c, g, n: (g, n, ZE_DW // RANK_PAD)),
            pl.BlockSpec((bb, rt, RANK_PAD), lambda c, g, n: (g, n, ZE_DA // RANK_PAD)),
            pl.BlockSpec((bb, rt, RWKV_GATE_RANK), lambda c, g, n: (g, n, ZE_DG // RWKV_GATE_RANK)),
            pl.BlockSpec((None, 1, RW_CAT), lambda c, g, n: (c, 0, 0)),
            pl.BlockSpec((None, bb, RW_CAT), lambda c, g, n: (c, g, 0)),
            pl.BlockSpec((bb, 2, RWKV_HEAD, RWKV_HEAD), sblk),
            vec,
            pl.BlockSpec((RANK_PAD, LANES), lambda c, g, n: (0, c)),
            vec,
            pl.BlockSpec((RANK_PAD, LANES), lambda c, g, n: (0, c)),
            pl.BlockSpec((RWKV_GATE_RANK, LANES), lambda c, g, n: (0, c)),
            vec, vec, vec, vec, vec,
        ],
        out_specs=[pl.BlockSpec((bb, rt, LANES), lambda c, g, n: (g, n, c)),
                   pl.BlockSpec((bb, 2, RWKV_HEAD, RWKV_HEAD), sblk)],
        out_shape=[jax.ShapeDtypeStruct((n_seq, seq_rows, MIX_A), BF16),
                   jax.ShapeDtypeStruct((n_seq, RWKV_HEADS, RWKV_HEAD, RWKV_HEAD), F32)],
        scratch_shapes=[pltpu.VMEM((bb, LANES, LANES), F32), pltpu.VMEM((bb, SUBLANES, RW_CAT), F32)],
        compiler_params=_cparams(("parallel", "parallel", "arbitrary")),
        name="rwkv7",
    )(z, z, z, z, z, z, mu_cat, sh_cat, wkv0,
      row2(w0), wup_p, row2(a0), aup_p, g_up, row2(k_k), row2(k_a), row2(r_k), row2(lnx_w), row2(lnx_b))
    return y, s_t


def _s5_prep_body(are_ref, aim_ref, ls_ref, abre_ref, abim_ref, zr_ref, zi_ref):
    lam_re = are_ref[...]
    lam_im = aim_ref[...]
    dt = jnp.exp(ls_ref[...])
    mag = jnp.exp(lam_re * dt)
    ab_re = mag * jnp.cos(lam_im * dt)
    ab_im = mag * jnp.sin(lam_im * dt)
    inv = 1.0 / (lam_re * lam_re + lam_im * lam_im)
    abre_ref[...] = ab_re
    abim_ref[...] = ab_im
    zr_ref[...] = ((ab_re - 1.0) * lam_re + ab_im * lam_im) * inv
    zi_ref[...] = (ab_im * lam_re - (ab_re - 1.0) * lam_im) * inv


def _s5_prep(a_re, a_im, log_step):
    g, n = a_re.shape
    sd = jax.ShapeDtypeStruct((g, n), F32)
    return pl.pallas_call(_s5_prep_body, out_shape=[sd, sd, sd, sd], name="s5_discretise")(
        a_re, a_im, jnp.broadcast_to(log_step.reshape(g, 1), (g, n)))


def _gelu_tanh(x):
    return 0.5 * x * (1.0 + jnp.tanh(math.sqrt(2.0 / math.pi) * (x + 0.044715 * (x * x * x))))


def _s5_body(u_ref, wbr_ref, wbi_ref, wcr_ref, wci_ref, abr_ref, abi_ref, d_ref, h0r_ref, h0i_ref,
             y_ref, hTr_ref, hTi_ref, hr_scr, hi_scr, cr_scr, ci_scr, *, rt, seq_rows):
    n = pl.program_id(1)
    u = u_ref[...]
    ub = u.astype(BF16)
    hr_scr[...] = _dot(ub, wbr_ref[...].astype(BF16))
    hi_scr[...] = _dot(ub, wbi_ref[...].astype(BF16))
    ar = abr_ref[...]
    ai = abi_ref[...]

    @pl.when(n == 0)
    def _():
        cr_scr[...] = jnp.zeros_like(cr_scr)
        ci_scr[...] = jnp.zeros_like(ci_scr)

    def step(r, hr, hi):
        br = hr_scr[pl.ds(r, 1), :]
        bi = hi_scr[pl.ds(r, 1), :]
        nr = ar * hr - ai * hi + br
        ni = ar * hi + ai * hr + bi
        hr_scr[pl.ds(r, 1), :] = nr
        hi_scr[pl.ds(r, 1), :] = ni
        return nr, ni

    if seq_rows >= rt:
        b = (n * rt) // seq_rows
        is_start = ((n * rt) % seq_rows) == 0
        hr = jnp.where(is_start, h0r_ref[pl.ds(b, 1), :], cr_scr[0:1, :])
        hi = jnp.where(is_start, h0i_ref[pl.ds(b, 1), :], ci_scr[0:1, :])
        hr, hi = lax.fori_loop(0, rt, lambda r, c: step(r, c[0], c[1]), (hr, hi), unroll=8)
        cr_scr[0:1, :] = hr
        ci_scr[0:1, :] = hi
        hTr_ref[pl.ds(b, 1), :] = hr
        hTi_ref[pl.ds(b, 1), :] = hi
    else:
        nseq = rt // seq_rows

        def seq(s, carry):
            b = n * nseq + s
            hr = h0r_ref[pl.ds(b, 1), :]
            hi = h0i_ref[pl.ds(b, 1), :]
            for t in range(seq_rows):
                hr, hi = step(s * seq_rows + t, hr, hi)
            hTr_ref[pl.ds(b, 1), :] = hr
            hTi_ref[pl.ds(b, 1), :] = hi
            return carry

        lax.fori_loop(0, nseq, seq, 0)

    y = _dot(hr_scr[...].astype(BF16), wcr_ref[...].astype(BF16)) + _dot(hi_scr[...].astype(BF16), wci_ref[...].astype(BF16))
    y_ref[...] = _gelu_tanh(y + d_ref[...] * u)


def _s5_scan(z, col0, wbr, wbi, wcr, wci, ab_re, ab_im, d_skip, h0_re, h0_im, *, n_seq, seq_rows):
    m = z.shape[0]
    rt = _pick(m, (256, 128, 64, 32, 16, 8))
    assert seq_rows % rt == 0 or rt % seq_rows == 0
    cb0 = col0 // LANES
    st = S5_SLAB_STATE
    kern = functools.partial(_s5_body, rt=rt, seq_rows=seq_rows)
    wspec_b = pl.BlockSpec((None, LANES, st), lambda s, n: (s, 0, 0))
    wspec_c = pl.BlockSpec((None, st, LANES), lambda s, n: (s, 0, 0))
    vspec = pl.BlockSpec((1, st), lambda s, n: (0, s))
    hspec = pl.BlockSpec((n_seq, st), lambda s, n: (0, s))
    return pl.pallas_call(
        kern,
        grid=(S5_SLABS, m // rt),
        in_specs=[pl.BlockSpec((rt, LANES), lambda s, n: (n, cb0 + s)), wspec_b, wspec_b, wspec_c, wspec_c,
                  vspec, vspec, pl.BlockSpec((1, LANES), lambda s, n: (0, s)), hspec, hspec],
        out_specs=[pl.BlockSpec((rt, LANES), lambda s, n: (n, s)), hspec, hspec],
        out_shape=[jax.ShapeDtypeStruct((m, MIX_B), F32),
                   jax.ShapeDtypeStruct((n_seq, S5_GROUPS * S5_STATE), F32),
                   jax.ShapeDtypeStruct((n_seq, S5_GROUPS * S5_STATE), F32)],
        scratch_shapes=[pltpu.VMEM((rt, st), F32), pltpu.VMEM((rt, st), F32),
                        pltpu.VMEM((SUBLANES, st), F32), pltpu.VMEM((SUBLANES, st), F32)],
        compiler_params=_cparams(("parallel", "arbitrary")),
        name="s5_scan",
    )(z, wbr, wbi, wcr, wci, ab_re.reshape(1, -1), ab_im.reshape(1, -1), d_skip.reshape(1, -1), h0_re, h0_im)


def _glu_body(a_ref, w_ref, b_ref, y_ref, o_ref):
    t = _dot(a_ref[...].astype(BF16), w_ref[...].astype(BF16)) + b_ref[...]
    o_ref[...] = (y_ref[...] * _sigmoid(t)).astype(o_ref.dtype)


def _glu(y, w, b):
    m, k = y.shape
    n = w.shape[1]
    tm = _pick(m, (1024, 512, 256, 128, 64, 32, 16, 8))
    tn = _pick(n, (512, 256, 128))
    return pl.pallas_call(
        _glu_body,
        grid=(m // tm, n // tn),
        in_specs=[pl.BlockSpec((tm, k), lambda i, j: (i, 0)), pl.BlockSpec((k, tn), lambda i, j: (0, j)),
                  pl.BlockSpec((1, tn), lambda i, j: (0, j)), pl.BlockSpec((tm, tn), lambda i, j: (i, j))],
        out_specs=pl.BlockSpec((tm, tn), lambda i, j: (i, j)),
        out_shape=jax.ShapeDtypeStruct((m, n), BF16),
        compiler_params=_cparams(("parallel", "arbitrary")),
        name="s5_glu",
    )(y, w, b.reshape(1, n), y)


def _mlstm_body(q_ref, k_ref, v_ref, og_ref, gt_ref, gb_ref, nw_ref, c0_ref, n0_ref, m0_ref,
                y_ref, cT_ref, nT_ref, mT_ref, c_scr, n_scr, m_scr, *, bb, rt, chunk, valid, unroll):
    h = pl.program_id(0)
    n = pl.program_id(2)
    n_last = pl.num_programs(2) - 1
    L = chunk
    nchunk = rt // L
    rl = _iota((L, L), 0)
    cl = _iota((L, L), 1)
    causal = rl >= cl
    tril = jnp.where(causal, 1.0, 0.0).astype(BF16)
    row = _iota((L, 1), 0)
    lane = _iota((L, LANES), 1)
    gbias = gb_ref[...]
    nw = nw_ref[...]
    scale = MLSTM_DQK ** -0.5

    def load(seq, r0, is_start):
        c_old = jnp.where(is_start, c0_ref[seq, 0], c_scr[seq])
        n_old = jnp.where(is_start, n0_ref[seq, 0], n_scr[seq, 0:1, :])
        m_prev = jnp.where(is_start, m0_ref[seq, 0], m_scr[seq, 0:1, 0:1])
        rows = pl.ds(r0, L)
        return (q_ref[seq, rows, :], k_ref[seq, rows, :], v_ref[seq, rows, :], og_ref[seq, rows, :],
                gt_ref[seq, rows, :], c_old, n_old, m_prev)

    def compute_all(loaded):
        nu = len(loaded)
        igs, lfs = [], []
        for q, k, v, og, gt, c_old, n_old, m_prev in loaded:
            gt = gt + gbias
            ig_raw = jnp.sum(jnp.where(lane == h, gt, 0.0), axis=-1, keepdims=True)
            fg_raw = jnp.sum(jnp.where(lane == MLSTM_HEADS + h, gt, 0.0), axis=-1, keepdims=True)
            ig = GATE_CAP * jnp.tanh(ig_raw / GATE_CAP)
            lf = -_softplus(-(GATE_CAP * jnp.tanh(fg_raw / GATE_CAP)))
            if valid < L:
                ok = row < valid
                ig = jnp.where(ok, ig, NEG_BIG)
                lf = jnp.where(ok, lf, 0.0)
            igs.append(ig)
            lfs.append(lf)
        lf_mat = jnp.zeros((L, LANES), F32)
        for u in range(nu):
            lf_mat = jnp.where(lane == u, lfs[u], lf_mat)
        b_mat = _dot_const(tril, lf_mat)
        pack = b_mat
        for u in range(nu):
            pack = jnp.where(lane == nu + u, igs[u], pack)
        pack_t = pack.T
        gates, qbs, kbs, ks = [], [], [], []
        for u, (q, k, v, og, gt, c_old, n_old, m_prev) in enumerate(loaded):
            bcol = b_mat[:, u:u + 1]
            b_row = pack_t[u:u + 1, :]
            ig_row = pack_t[nu + u:nu + u + 1, :]
            log_d = jnp.where(causal, bcol - b_row + ig_row, NEG_BIG)
            log_p = bcol + m_prev
            m_tok = jnp.maximum(log_p, jnp.max(log_d, axis=-1, keepdims=True))
            gates.append((jnp.exp(log_d - m_tok), jnp.exp(log_p - m_tok), m_tok, bcol))
            k = k * scale
            ks.append(k)
            qbs.append(q.astype(BF16))
            kbs.append(k.astype(BF16))
        ss = [_dot(qb, kb, NT) * g_[0] for qb, kb, g_ in zip(qbs, kbs, gates)]
        svs = [_dot(s.astype(BF16), x[2].astype(BF16)) for s, x in zip(ss, loaded)]
        qcs = [_dot(qb, x[5].astype(BF16), NT) for qb, x in zip(qbs, loaded)]
        upd = []
        for u, (q, k, v, og, gt, c_old, n_old, m_prev) in enumerate(loaded):
            d, p, m_tok, bcol = gates[u]
            m_new = m_tok[L - 1:L, :]
            b_last = bcol[L - 1:L, :]
            w_col = jnp.exp(b_last - bcol + igs[u] - m_new)
            cs = jnp.exp(b_last + m_prev - m_new)
            upd.append((w_col, cs, m_new))
        vks = [_dot((x[2] * w_[0]).astype(BF16), kb, TN) for x, w_, kb in zip(loaded, upd, kbs)]
        out = []
        for u, (q, k, v, og, gt, c_old, n_old, m_prev) in enumerate(loaded):
            d, p, m_tok, bcol = gates[u]
            w_col, cs, m_new = upd[u]
            num = svs[u] + p * qcs[u]
            den = jnp.sum(ss[u], axis=-1, keepdims=True) + p * jnp.sum(q * n_old, axis=-1, keepdims=True)
            hh = num / jnp.maximum(jnp.abs(den), jnp.exp(-m_tok))
            c_new = cs * c_old + vks[u]
            n_new = cs * n_old + jnp.sum(w_col * ks[u], axis=0, keepdims=True)
            hn = hh * lax.rsqrt(jnp.mean(hh * hh, axis=-1, keepdims=True) + RMS_EPS)
            out.append(((hn * nw * _sigmoid(og)).astype(y_ref.dtype), c_new, n_new, m_new))
        return out

    def store(seq, r0, is_end, y, c_new, n_new, m_new):
        y_ref[seq, pl.ds(r0, L), :] = y
        c_scr[seq] = c_new
        n_scr[seq, 0:1, :] = n_new
        m_scr[seq, 0:1, :] = jnp.broadcast_to(m_new, (1, LANES))

        @pl.when(is_end)
        def _():
            cT_ref[seq, 0] = c_new
            nT_ref[seq, 0] = n_new
            mT_ref[seq, 0] = m_new

    def step(it, carry):
        sg = it // nchunk
        ci_ = it - sg * nchunk
        r0 = pl.multiple_of(ci_ * L, L)
        is_start = jnp.logical_and(n == 0, ci_ == 0)
        is_end = jnp.logical_and(n == n_last, ci_ == nchunk - 1)
        seqs = [sg * unroll + u for u in range(unroll)]
        loaded = [load(s, r0, is_start) for s in seqs]
        done = compute_all(loaded)
        for s, d in zip(seqs, done):
            store(s, r0, is_end, *d)
        return carry

    @pl.when(n == 0)
    def _():
        c_scr[...] = jnp.zeros_like(c_scr)
        n_scr[...] = jnp.zeros_like(n_scr)
        m_scr[...] = jnp.zeros_like(m_scr)

    lax.fori_loop(0, (bb // unroll) * nchunk, step, 0)


def _mlstm(z, c0, n0, m0, b_i, b_f, norm_w, *, n_seq, seq_rows, chunk, valid):
    nh, dqk, dv = MLSTM_HEADS, MLSTM_DQK, MLSTM_DV
    unroll = _pick(n_seq, (4, 2, 1))
    if seq_rows > chunk:
        rt = _pick(seq_rows, (256, 128, 64))
        bb = unroll
    else:
        rt = seq_rows
        bb = _pick(n_seq, (8, 4, 2, 1))
    assert rt % chunk == 0 and bb % unroll == 0
    sblk = lambda h, g, n: (g, h, 0, 0)
    gbias = jnp.pad(jnp.concatenate([b_i, b_f]).reshape(1, 2 * nh), ((0, 0), (0, LANES - 2 * nh)))
    n0r = n0.reshape(n_seq, nh, 1, dqk)
    m0r = m0.reshape(n_seq, nh, 1, 1)
    kern = functools.partial(_mlstm_body, bb=bb, rt=rt, chunk=chunk, valid=valid, unroll=unroll)
    y, c_t, n_t, m_t = pl.pallas_call(
        kern,
        grid=(nh, n_seq // bb, seq_rows // rt),
        in_specs=[
            pl.BlockSpec((bb, rt, dqk), lambda h, g, n: (g, n, h)),
            pl.BlockSpec((bb, rt, dqk), lambda h, g, n: (g, n, nh + h)),
            pl.BlockSpec((bb, rt, dv), lambda h, g, n: (g, n, (2 * nh * dqk) // dv + h)),
            pl.BlockSpec((bb, rt, dv), lambda h, g, n: (g, n, ZO_OG // dv + h)),
            pl.BlockSpec((bb, rt, LANES), lambda h, g, n: (g, n, ZO_G // LANES)),
            pl.BlockSpec((1, LANES), lambda h, g, n: (0, 0)),
            pl.BlockSpec((1, dv), lambda h, g, n: (0, h)),
            pl.BlockSpec((bb, 1, dv, dqk), sblk),
            pl.BlockSpec((bb, 1, 1, dqk), sblk),
            pl.BlockSpec((bb, 1, 1, 1), sblk),
        ],
        out_specs=[pl.BlockSpec((bb, rt, dv), lambda h, g, n: (g, n, h)),
                   pl.BlockSpec((bb, 1, dv, dqk), sblk),
                   pl.BlockSpec((bb, 1, 1, dqk), sblk),
                   pl.BlockSpec((bb, 1, 1, 1), sblk)],
        out_shape=[jax.ShapeDtypeStruct((n_seq, seq_rows, nh * dv), BF16),
                   jax.ShapeDtypeStruct((n_seq, nh, dv, dqk), F32),
                   jax.ShapeDtypeStruct((n_seq, nh, 1, dqk), F32),
                   jax.ShapeDtypeStruct((n_seq, nh, 1, 1), F32)],
        scratch_shapes=[pltpu.VMEM((bb, dv, dqk), F32), pltpu.VMEM((bb, SUBLANES, dqk), F32),
                        pltpu.VMEM((bb, SUBLANES, LANES), F32)],
        compiler_params=_cparams(("parallel", "parallel", "arbitrary")),
        name="mlstm",
    )(z, z, z, z, z, gbias, norm_w.reshape(1, nh * dv), c0, n0r, m0r)
    return y, c_t, n_t.reshape(n_seq, nh, dqk), m_t.reshape(n_seq, nh)


def _pad_cols(a, n):
    return jnp.pad(a, ((0, 0), (0, n - a.shape[1])))


def _rwkv_cols(a):
    c1 = 3 * MIX_A
    c2 = c1 + RWKV_DECAY_RANK
    c3 = c2 + RWKV_ICL_RANK
    return jnp.concatenate([a[:, :c1], _pad_cols(a[:, c1:c2], RANK_PAD), _pad_cols(a[:, c2:c3], RANK_PAD), a[:, c3:]], axis=1)


def _rwkv_cols_inv(a):
    return jnp.concatenate([a[:, :ZE_DW + RWKV_DECAY_RANK], a[:, ZE_DA:ZE_DA + RWKV_ICL_RANK], a[:, ZE_DG:ZE_U]], axis=1)


def _pad_seq(a, n_seq, t, tp):
    a = a.reshape(n_seq, t, a.shape[1])
    return a if tp == t else jnp.pad(a, ((0, 0), (0, tp - t), (0, 0)))


def _unpad_seq(a, n_seq, t, tp):
    a = a if tp == t else a[:, :t]
    return a.reshape(n_seq * t, a.shape[2])


def _s5_block_weights(zr, zi, b_re, b_im, c_re, c_im):
    g, n, c = S5_GROUPS, S5_STATE, S5_GROUP
    sg = S5_SLAB_GROUPS
    bb_re = zr[..., None] * b_re - zi[..., None] * b_im
    bb_im = zr[..., None] * b_im + zi[..., None] * b_re
    eye = jnp.eye(sg, dtype=F32)

    def in_blocks(bb):
        t = bb.reshape(S5_SLABS, sg, n, c)
        return jnp.einsum('sgnc,gh->sgchn', t, eye).reshape(S5_SLABS, sg * c, sg * n)

    def out_blocks(cc):
        t = cc.reshape(S5_SLABS, sg, c, n)
        return jnp.einsum('sgcn,gh->sgnhc', t, eye).reshape(S5_SLABS, sg * n, sg * c)

    return in_blocks(bb_re), in_blocks(bb_im), out_blocks(c_re), out_blocks(-c_im)


def _trunk(x, n_seq, t, st, w):
    tp = t if t >= MLSTM_CHUNK else SUBLANES * ((t + SUBLANES - 1) // SUBLANES)
    chunk = MLSTM_CHUNK if t >= MLSTM_CHUNK else tp
    valid = chunk if tp == t else t
    assert t % chunk == 0 or tp == chunk

    xn = _rmsnorm(x, w['ln_mix_e'], BF16)
    z = _mm(xn, w['w_in_e'])
    zp = _pad_seq(z, n_seq, t, tp)
    ya, wkv_t = _rwkv(zp, _rwkv_cols(st['shift']), st['wkv'], w['mu'], w['w0'], w['w_up'], w['a0'], w['a_up'], w['g_up'],
                      w['k_k'], w['k_a'], w['r_k'], w['lnx_w'], w['lnx_b'],
                      n_seq=n_seq, seq_rows=tp, chunk=chunk, valid=valid)
    ya = _unpad_seq(ya, n_seq, t, tp)
    shift_t = _rwkv_cols_inv(z.reshape(n_seq, t, ZE_W)[:, -1, :ZE_U])
    yb, re_t, im_t = _s5_scan(z, ZE_U, w['s5_wbr'], w['s5_wbi'], w['s5_wcr'], w['s5_wci'], w['s5_ab_re'], w['s5_ab_im'],
                              w['s5_d'], st['s5_re'], st['s5_im'], n_seq=n_seq, seq_rows=t)
    yb = _glu(yb, w['s5_glu_w'], w['s5_glu_b'])
    x = _mm2_res(ya, yb, w['w_out_e'], x)
    hn = _rmsnorm(x, w['ln_ffn_e'], BF16)
    hid = _glu_up(hn, w['ffn_w1'], w['ffn_w3'])
    x = _mm_acc(hid, w['ffn_w2'], x)

    xn = _rmsnorm(x, w['ln_mix_o'], BF16)
    zo = _mm(xn, w['w_in_o'])
    zop = _pad_seq(zo, n_seq, t, tp)
    yc, c_t, n_t, m_t = _mlstm(zop, st['c'], st['n'], st['m'], w['b_i'], w['b_f'], w['norm_w'],
                               n_seq=n_seq, seq_rows=tp, chunk=chunk, valid=valid)
    yc = _unpad_seq(yc, n_seq, t, tp)
    x = _mm_res(yc, w['w_out_o'], x)
    hn, comb = _rmsnorm_router(x, w['ln_ffn_o'], w['router_w'], w['router_b'])
    hid = _moe_up(hn, comb, w['exp_w1'], w['exp_w3'])
    x = _mm_acc(hid, w['exp_w2'], x)
    y = _rmsnorm(x, w['final_norm'], F32)

    g, ns = S5_GROUPS, S5_STATE
    return (y, wkv_t[None], shift_t[None], re_t.reshape(1, n_seq, g, ns), im_t.reshape(1, n_seq, g, ns),
            c_t[None], n_t[None], m_t[None])


def kernel(x_prompt, x_sample, state_rwkv_wkv, state_rwkv_shift, state_s5_re, state_s5_im, state_mlstm_c, state_mlstm_n, state_mlstm_m, ln_mix_e, w_in_e, rwkv_mu, rwkv_w0, rwkv_w_up, rwkv_a0, rwkv_a_up, rwkv_g_up, rwkv_k_k, rwkv_k_a, rwkv_r_k, rwkv_lnx_w, rwkv_lnx_b, s5_a_re, s5_a_im, s5_log_step, s5_b_re, s5_b_im, s5_c_re, s5_c_im, s5_d, s5_glu_w, s5_glu_b, w_out_e, ln_ffn_e, ffn_w1, ffn_w3, ffn_w2, ln_mix_o, w_in_o, mlstm_b_i, mlstm_b_f, mlstm_norm_w, w_out_o, ln_ffn_o, router_w, router_b, exp_w1, exp_w3, exp_w2, final_norm):
    assert ln_mix_e.shape[0] == 1 and ln_mix_o.shape[0] == 1
    d = D_MODEL
    ab_re, ab_im, zr, zi = _s5_prep(s5_a_re[0], s5_a_im[0], s5_log_step[0])
    wbr, wbi, wcr, wci = _s5_block_weights(zr, zi, s5_b_re[0], s5_b_im[0], s5_c_re[0], s5_c_im[0])
    wo = w_in_o[0]
    nh = MLSTM_HEADS
    w = {
        'ln_mix_e': ln_mix_e[0],
        'w_in_e': jnp.concatenate([_rwkv_cols(w_in_e[0][:, :RWKV_IN]), w_in_e[0][:, RWKV_IN:]], axis=1),
        'mu': _rwkv_cols(rwkv_mu[0].reshape(1, RWKV_IN)),
        'w0': rwkv_w0[0], 'a0': rwkv_a0[0],
        'w_up': jnp.pad(rwkv_w_up[0], ((0, RANK_PAD - RWKV_DECAY_RANK), (0, 0))),
        'a_up': jnp.pad(rwkv_a_up[0], ((0, RANK_PAD - RWKV_ICL_RANK), (0, 0))),
        'g_up': rwkv_g_up[0], 'k_k': rwkv_k_k[0], 'k_a': rwkv_k_a[0], 'r_k': rwkv_r_k[0].reshape(MIX_A),
        'lnx_w': rwkv_lnx_w[0], 'lnx_b': rwkv_lnx_b[0],
        's5_wbr': wbr, 's5_wbi': wbi, 's5_wcr': wcr, 's5_wci': wci,
        's5_ab_re': ab_re, 's5_ab_im': ab_im, 's5_d': s5_d[0], 's5_glu_w': s5_glu_w[0], 's5_glu_b': s5_glu_b[0],
        'w_out_e': w_out_e[0], 'ln_ffn_e': ln_ffn_e[0],
        'ffn_w1': ffn_w1[0], 'ffn_w3': ffn_w3[0], 'ffn_w2': ffn_w2[0],
        'ln_mix_o': ln_mix_o[0],
        'w_in_o': jnp.concatenate([wo[:, :ZO_OG], wo[:, ZO_OG + 2 * nh:], _pad_cols(wo[:, ZO_OG:ZO_OG + 2 * nh], 2 * LANES)], axis=1),
        'b_i': mlstm_b_i[0], 'b_f': mlstm_b_f[0], 'norm_w': mlstm_norm_w[0],
        'w_out_o': w_out_o[0], 'ln_ffn_o': ln_ffn_o[0],
        'router_w': router_w[0], 'router_b': router_b[0],
        'exp_w1': exp_w1[0], 'exp_w3': exp_w3[0], 'exp_w2': exp_w2[0].reshape(N_EXPERTS * EXPERT_FF, d),
        'final_norm': final_norm,
    }
    bp, tp_, _ = x_prompt.shape
    bs, ts_, _ = x_sample.shape
    g, ns = S5_GROUPS, S5_STATE
    zero = lambda *s: jnp.zeros(s, F32)
    st_p = {'wkv': zero(bp, RWKV_HEADS, RWKV_HEAD, RWKV_HEAD), 'shift': zero(bp, RWKV_IN),
            's5_re': zero(bp, g * ns), 's5_im': zero(bp, g * ns),
            'c': zero(bp, nh, MLSTM_DV, MLSTM_DQK), 'n': zero(bp, nh, MLSTM_DQK), 'm': zero(bp, nh)}
    st_s = {'wkv': state_rwkv_wkv[0], 'shift': state_rwkv_shift[0],
            's5_re': state_s5_re[0].reshape(bs, g * ns), 's5_im': state_s5_im[0].reshape(bs, g * ns),
            'c': state_mlstm_c[0], 'n': state_mlstm_n[0], 'm': state_mlstm_m[0]}
    out_p = _trunk(x_prompt.reshape(bp * tp_, d), bp, tp_, st_p, w)
    out_s = _trunk(x_sample.reshape(bs * ts_, d), bs, ts_, st_s, w)
    y_p = out_p[0].reshape(bp, tp_, d)
    y_s = out_s[0].reshape(bs, ts_, d)
    return (y_p, y_s) + tuple(out_p[1:]) + tuple(out_s[1:])
```

```python
import functools
import math

import jax
import jax.numpy as jnp
from jax import lax
from jax.experimental import pallas as pl
from jax.experimental.pallas import tpu as pltpu

F32 = jnp.float32
BF16 = jnp.bfloat16

D_MODEL = 4096
MIX_A = D_MODEL // 2
MIX_B = D_MODEL - MIX_A
RWKV_HEAD = 64
RWKV_HEADS = MIX_A // RWKV_HEAD
RWKV_DECAY_RANK = 96
RWKV_ICL_RANK = 96
RWKV_GATE_RANK = 256
RWKV_IN = 3 * MIX_A + RWKV_DECAY_RANK + RWKV_ICL_RANK + RWKV_GATE_RANK
RWKV_LNX_EPS = 1e-5 * RWKV_HEAD
S5_GROUP = 16
S5_GROUPS = MIX_B // S5_GROUP
S5_STATE = 64
MLSTM_HEADS = 8
MLSTM_DQK = D_MODEL // 16
MLSTM_DV = D_MODEL // 8
MLSTM_CHUNK = 64
GATE_CAP = 15.0
FFN_DIM = 11008
N_EXPERTS = 8
EXPERT_FF = D_MODEL // 2
RMS_EPS = 1e-6

LANES = 128
SUBLANES = 8
VMEM_LIMIT = 56 * 1024 * 1024

RANK_PAD = LANES
ZE_DW = 3 * MIX_A
ZE_DA = ZE_DW + RANK_PAD
ZE_DG = ZE_DA + RANK_PAD
ZE_U = ZE_DG + RWKV_GATE_RANK
ZE_W = ZE_U + MIX_B
RW_CAT = 3 * LANES + 2 * RANK_PAD + RWKV_GATE_RANK
ZO_OG = 2 * MLSTM_HEADS * MLSTM_DQK + MLSTM_HEADS * MLSTM_DV
ZO_G = ZO_OG + MLSTM_HEADS * MLSTM_DV
ZO_W = ZO_G + 2 * LANES

S5_SLAB_GROUPS = 16
S5_SLABS = S5_GROUPS // S5_SLAB_GROUPS
S5_SLAB_IN = S5_SLAB_GROUPS * S5_GROUP
S5_SLAB_STATE = S5_SLAB_GROUPS * S5_STATE

NN = (((1,), (0,)), ((), ()))
NT = (((1,), (1,)), ((), ()))
TN = (((0,), (0,)), ((), ()))
NEG_BIG = -1e30


def _cparams(sem):
    return pltpu.CompilerParams(dimension_semantics=sem, vmem_limit_bytes=VMEM_LIMIT)


def _pick(n, cands):
    for c in cands:
        if n % c == 0:
            return c
    raise ValueError(f"no tile for {n}")


def _dot(a, b, dims=NN):
    return lax.dot_general(a, b, dims, preferred_element_type=F32)


def _split2(a):
    hi = a.astype(BF16)
    lo = (a - hi.astype(F32)).astype(BF16)
    return hi, lo


def _dot_hp(a, b, dims=NN):
    ah, al = _split2(a)
    bh, bl = _split2(b)
    return _dot(ah, bh, dims) + (_dot(ah, bl, dims) + _dot(al, bh, dims))


def _dot_const(c_bf16, a):
    h0 = a.astype(BF16)
    r1 = a - h0.astype(F32)
    h1 = r1.astype(BF16)
    h2 = (r1 - h1.astype(F32)).astype(BF16)
    return _dot(c_bf16, h0) + (_dot(c_bf16, h1) + _dot(c_bf16, h2))


def _sigmoid(x):
    return 1.0 / (1.0 + jnp.exp(-x))


def _softplus(x):
    return jnp.maximum(x, 0.0) + jnp.log(1.0 + jnp.exp(-jnp.abs(x)))


def _iota(shape, dim):
    return lax.broadcasted_iota(jnp.int32, shape, dim)


def _rmsnorm_body(x_ref, g_ref, o_ref):
    x = x_ref[...]
    ms = jnp.mean(x * x, axis=-1, keepdims=True)
    o_ref[...] = (x * lax.rsqrt(ms + RMS_EPS) * g_ref[...]).astype(o_ref.dtype)


def _rmsnorm(x, g, out_dtype):
    m, d = x.shape
    tm = _pick(m, (256, 128, 64, 32, 16, 8))
    return pl.pallas_call(
        _rmsnorm_body,
        grid=(m // tm,),
        in_specs=[pl.BlockSpec((tm, d), lambda i: (i, 0)), pl.BlockSpec((1, d), lambda i: (0, 0))],
        out_specs=pl.BlockSpec((tm, d), lambda i: (i, 0)),
        out_shape=jax.ShapeDtypeStruct((m, d), out_dtype),
        compiler_params=_cparams(("parallel",)),
        name="rmsnorm",
    )(x, g.reshape(1, d))


def _router_body(x_ref, g_ref, rw_ref, rb_ref, hn_ref, comb_ref):
    x = x_ref[...]
    ms = jnp.mean(x * x, axis=-1, keepdims=True)
    h = x * lax.rsqrt(ms + RMS_EPS) * g_ref[...]
    hn_ref[...] = h.astype(hn_ref.dtype)
    logits = _dot_hp(h, rw_ref[...]) + rb_ref[...]
    lane = _iota(logits.shape, 1)
    logits = jnp.where(lane < N_EXPERTS, logits, NEG_BIG)
    m1 = jnp.max(logits, axis=-1, keepdims=True)
    i1 = jnp.min(jnp.where(logits == m1, lane, LANES), axis=-1, keepdims=True)
    rest = jnp.where(lane == i1, NEG_BIG, logits)
    m2 = jnp.max(rest, axis=-1, keepdims=True)
    i2 = jnp.min(jnp.where(rest == m2, lane, LANES), axis=-1, keepdims=True)
    e = jnp.exp(m2 - m1)
    g1 = 1.0 / (1.0 + e)
    g2 = e / (1.0 + e)
    comb_ref[...] = jnp.where(lane == i1, g1, 0.0) + jnp.where(lane == i2, g2, 0.0)


def _rmsnorm_router(x, g, router_w, router_b):
    m, d = x.shape
    tm = _pick(m, (256, 128, 64, 32, 16, 8))
    rw = jnp.pad(router_w, ((0, 0), (0, LANES - N_EXPERTS)))
    rb = jnp.pad(router_b.reshape(1, N_EXPERTS), ((0, 0), (0, LANES - N_EXPERTS)))
    return pl.pallas_call(
        _router_body,
        grid=(m // tm,),
        in_specs=[pl.BlockSpec((tm, d), lambda i: (i, 0)), pl.BlockSpec((1, d), lambda i: (0, 0)),
                  pl.BlockSpec((d, LANES), lambda i: (0, 0)), pl.BlockSpec((1, LANES), lambda i: (0, 0))],
        out_specs=[pl.BlockSpec((tm, d), lambda i: (i, 0)), pl.BlockSpec((tm, LANES), lambda i: (i, 0))],
        out_shape=[jax.ShapeDtypeStruct((m, d), BF16), jax.ShapeDtypeStruct((m, LANES), F32)],
        compiler_params=_cparams(("parallel",)),
        name="rmsnorm_router",
    )(x, g.reshape(1, d), rw, rb)


def _mm_body(a_ref, w_ref, o_ref):
    o_ref[...] = _dot(a_ref[...], w_ref[...].astype(BF16)).astype(o_ref.dtype)


def _mm(a, w, n_out=None):
    m, k = a.shape
    n = w.shape[1] if n_out is None else n_out
    tm = _pick(m, (1024, 512, 256, 128, 64, 32, 16, 8))
    tn = _pick(n, (512, 256, 128))
    return pl.pallas_call(
        _mm_body,
        grid=(m // tm, n // tn),
        in_specs=[pl.BlockSpec((tm, k), lambda i, j: (i, 0)), pl.BlockSpec((k, tn), lambda i, j: (0, j))],
        out_specs=pl.BlockSpec((tm, tn), lambda i, j: (i, j)),
        out_shape=jax.ShapeDtypeStruct((m, n), F32),
        compiler_params=_cparams(("parallel", "arbitrary")),
        name="proj_in",
    )(a, w)


def _mm_split_body(a_ref, wm_ref, wt_ref, o_ref, *, nm):
    j = pl.program_id(1)
    a = a_ref[...]

    @pl.when(j < nm)
    def _():
        o_ref[...] = _dot(a, wm_ref[...].astype(BF16))

    @pl.when(j >= nm)
    def _():
        o_ref[...] = _dot(a, wt_ref[...].astype(BF16))


def _mm_split(a, w_stack, n_main, w_tail):
    m, k = a.shape
    n_tail = w_tail.shape[1]
    tm = _pick(m, (1024, 512, 256, 128, 64, 32, 16, 8))
    tn = 2 * LANES
    assert n_main % tn == 0 and n_tail % tn == 0
    nm = n_main // tn
    return pl.pallas_call(
        functools.partial(_mm_split_body, nm=nm),
        grid=(m // tm, nm + n_tail // tn),
        in_specs=[pl.BlockSpec((tm, k), lambda i, j: (i, 0)),
                  pl.BlockSpec((None, k, tn), lambda i, j: (0, 0, jnp.minimum(j, nm - 1))),
                  pl.BlockSpec((k, tn), lambda i, j: (0, jnp.maximum(j - nm, 0)))],
        out_specs=pl.BlockSpec((tm, tn), lambda i, j: (i, j)),
        out_shape=jax.ShapeDtypeStruct((m, n_main + n_tail), F32),
        compiler_params=_cparams(("parallel", "arbitrary")),
        name="proj_in",
    )(a, w_stack, w_tail)


def _mm2_res_body(a0_ref, a1_ref, w0_ref, w1_ref, r_ref, o_ref):
    acc = _dot(a0_ref[...], w0_ref[...].astype(BF16)) + _dot(a1_ref[...], w1_ref[...].astype(BF16))
    o_ref[...] = r_ref[...] + acc


def _mm2_res(a0, a1, w, res):
    m, k0 = a0.shape
    k1 = a1.shape[1]
    assert k0 == k1
    n = w.shape[1]
    tm = _pick(m, (1024, 512, 256, 128, 64, 32, 16, 8))
    tn = _pick(n, (512, 256, 128))
    return pl.pallas_call(
        _mm2_res_body,
        grid=(m // tm, n // tn),
        in_specs=[pl.BlockSpec((tm, k0), lambda i, j: (i, 0)), pl.BlockSpec((tm, k1), lambda i, j: (i, 0)),
                  pl.BlockSpec((k0, tn), lambda i, j: (0, j)), pl.BlockSpec((k1, tn), lambda i, j: (1, j)),
                  pl.BlockSpec((tm, tn), lambda i, j: (i, j))],
        out_specs=pl.BlockSpec((tm, tn), lambda i, j: (i, j)),
        out_shape=jax.ShapeDtypeStruct((m, n), F32),
        compiler_params=_cparams(("parallel", "arbitrary")),
        name="proj_out",
    )(a0, a1, w, w, res)


def _mm_res_body(a_ref, w_ref, r_ref, o_ref):
    o_ref[...] = r_ref[...] + _dot(a_ref[...], w_ref[...].astype(BF16))


def _mm_res(a, w, res):
    m, k = a.shape
    n = w.shape[1]
    tm = _pick(m, (1024, 512, 256, 128, 64, 32, 16, 8))
    tn = _pick(n, (512, 256, 128))
    return pl.pallas_call(
        _mm_res_body,
        grid=(m // tm, n // tn),
        in_specs=[pl.BlockSpec((tm, k), lambda i, j: (i, 0)), pl.BlockSpec((k, tn), lambda i, j: (0, j)),
                  pl.BlockSpec((tm, tn), lambda i, j: (i, j))],
        out_specs=pl.BlockSpec((tm, tn), lambda i, j: (i, j)),
        out_shape=jax.ShapeDtypeStruct((m, n), F32),
        compiler_params=_cparams(("parallel", "arbitrary")),
        name="proj_out1",
    )(a, w, res)


def _glu_up_body(a_ref, w1_ref, w3_ref, o_ref):
    a = a_ref[...]
    h1 = _dot(a, w1_ref[...].astype(BF16))
    h3 = _dot(a, w3_ref[...].astype(BF16))
    o_ref[...] = (h1 * _sigmoid(h1) * h3).astype(o_ref.dtype)


def _glu_up(a, w1, w3):
    m, k = a.shape
    n = w1.shape[1]
    tm = _pick(m, (1024, 512, 256, 128, 64, 32, 16, 8))
    tn = _pick(n, (256, 128))
    return pl.pallas_call(
        _glu_up_body,
        grid=(m // tm, n // tn),
        in_specs=[pl.BlockSpec((tm, k), lambda i, j: (i, 0)), pl.BlockSpec((k, tn), lambda i, j: (0, j)),
                  pl.BlockSpec((k, tn), lambda i, j: (0, j))],
        out_specs=pl.BlockSpec((tm, tn), lambda i, j: (i, j)),
        out_shape=jax.ShapeDtypeStruct((m, n), BF16),
        compiler_params=_cparams(("parallel", "arbitrary")),
        name="ffn_up",
    )(a, w1, w3)


def _moe_up_body(a_ref, comb_ref, w1_ref, w3_ref, o_ref):
    e = pl.program_id(1)
    a = a_ref[...]
    h1 = _dot(a, w1_ref[...].astype(BF16))
    h3 = _dot(a, w3_ref[...].astype(BF16))
    comb = comb_ref[...]
    gate = jnp.sum(jnp.where(_iota(comb.shape, 1) == e, comb, 0.0), axis=-1, keepdims=True)
    o_ref[...] = (h1 * _sigmoid(h1) * h3 * gate).astype(o_ref.dtype)


def _moe_up(a, comb, w1, w3):
    m, k = a.shape
    ne, _, f = w1.shape
    tm = _pick(m, (1024, 512, 256, 128, 64, 32, 16, 8))
    tn = _pick(f, (256, 128))
    nf = f // tn
    return pl.pallas_call(
        _moe_up_body,
        grid=(m // tm, ne, nf),
        in_specs=[pl.BlockSpec((tm, k), lambda i, e, j: (i, 0)), pl.BlockSpec((tm, LANES), lambda i, e, j: (i, 0)),
                  pl.BlockSpec((None, k, tn), lambda i, e, j: (e, 0, j)),
                  pl.BlockSpec((None, k, tn), lambda i, e, j: (e, 0, j))],
        out_specs=pl.BlockSpec((tm, tn), lambda i, e, j: (i, e * nf + j)),
        out_shape=jax.ShapeDtypeStruct((m, ne * f), BF16),
        compiler_params=_cparams(("parallel", "arbitrary", "arbitrary")),
        name="moe_up",
    )(a, comb, w1, w3)


def _mm_acc_body(a_ref, w_ref, r_ref, o_ref, *, k_total, tk, rc):
    kk = pl.program_id(2)
    ragged = k_total % tk != 0
    valid = k_total - kk * tk
    w = w_ref[...]
    if ragged:
        w = jnp.where(_iota(w.shape, 0) < valid, w, 0.0)
    w = w.astype(BF16)

    @pl.when(kk == 0)
    def _():
        o_ref[...] = r_ref[...]

    for r0 in range(0, a_ref.shape[0], rc):
        a = a_ref[r0:r0 + rc, :]
        if ragged:
            a = jnp.where(_iota(a.shape, 1) < valid, a, jnp.zeros_like(a))
        o_ref[r0:r0 + rc, :] += _dot(a, w)


def _mm_acc(a, w, res):
    m, k = a.shape
    n = w.shape[1]
    tm = _pick(m, (2048, 1024, 512, 256, 128, 64, 32, 16, 8))
    tn = _pick(n, (1024, 512, 256, 128))
    tk = 512
    nk = pl.cdiv(k, tk)
    return pl.pallas_call(
        functools.partial(_mm_acc_body, k_total=k, tk=tk, rc=min(tm, 256)),
        grid=(m // tm, n // tn, nk),
        in_specs=[pl.BlockSpec((tm, tk), lambda i, j, kk: (i, kk)), pl.BlockSpec((tk, tn), lambda i, j, kk: (kk, j)),
                  pl.BlockSpec((tm, tn), lambda i, j, kk: (i, j))],
        out_specs=pl.BlockSpec((tm, tn), lambda i, j, kk: (i, j)),
        out_shape=jax.ShapeDtypeStruct((m, n), F32),
        compiler_params=_cparams(("parallel", "parallel", "arbitrary")),
        name="proj_down",
    )(a, w, res)


def _rwkv_body(zr_ref, zk_ref, zv_ref, zdw_ref, zda_ref, zdg_ref, mu_ref, sh_ref, s0_ref,
               w0_ref, wup_ref, a0_ref, aup_ref, gup_ref, kk_ref, ka_ref, rk_ref, lw_ref, lb_ref,
               y_ref, sT_ref, s_scr, carry_scr, *, bb, rt, chunk, valid, unroll):
    n = pl.program_id(2)
    n_last = pl.num_programs(2) - 1
    L = chunk
    L2 = 2 * L
    H = RWKV_HEAD
    nchunk = rt // L
    nsq = max(1, int(math.ceil(math.log2(L))))

    lane = _iota((1, LANES), 1)
    m0 = (lane < H).astype(F32)
    m1 = 1.0 - m0
    r2 = _iota((L2, L2), 0)
    c2 = _iota((L2, L2), 1)
    rh = jnp.where(r2 >= L, 1, 0)
    ch = jnp.where(c2 >= L, 1, 0)
    same = jnp.where(rh == ch, 1.0, 0.0)
    tdiff = (r2 - L * rh) - (c2 - L * ch)
    strict = same * jnp.where(tdiff > 0, 1.0, 0.0)
    incl = same * jnp.where(tdiff >= 0, 1.0, 0.0)
    rl = _iota((L, L), 0)
    cl = _iota((L, L), 1)
    tril = jnp.where(rl >= cl, 1.0, 0.0).astype(BF16)
    ri = _iota((LANES, LANES), 0)
    ci = _iota((LANES, LANES), 1)
    bd = jnp.where((ri >= H) == (ci >= H), 1.0, 0.0)
    bones = bd.astype(BF16)
    row = _iota((L, 1), 0)

    @pl.when(n == 0)
    def _():
        s_scr[...] = jnp.zeros_like(s_scr)
        carry_scr[...] = jnp.zeros_like(carry_scr)

    mu = mu_ref[...]
    w0 = w0_ref[...]
    a0 = a0_ref[...]
    kkw = kk_ref[...]
    kaw = ka_ref[...]
    rkw = rk_ref[...]
    lnw = lw_ref[...]
    lnb = lb_ref[...]

    def block_sum(x):
        return _dot_const_rhs(x, bones)

    def load(seq, r0, is_start):
        zc = jnp.concatenate([ref[seq, pl.ds(r0, L), :] for ref in (zr_ref, zk_ref, zv_ref, zdw_ref, zda_ref, zdg_ref)],
                             axis=1)
        first = jnp.where(is_start, sh_ref[pl.ds(seq, 1), :], carry_scr[seq, 0:1, :])
        s0 = s0_ref[seq, 0]
        s1 = s0_ref[seq, 1]
        zz = jnp.zeros((H, H), F32)
        s_init = jnp.concatenate([jnp.concatenate([s0, zz], axis=1), jnp.concatenate([zz, s1], axis=1)], axis=0)
        st = jnp.where(is_start, s_init, s_scr[seq])
        return zc, first, st

    def recur(seqs):
        pre = []
        for r, k, v, ka, kb, lw, cum, st in seqs:
            ce = jnp.exp(cum)
            cinv = jnp.exp(-cum)
            at = jnp.exp(cum - lw) * ka
            rt_ = ce * r
            bt = kb * cinv
            kt = k * cinv
            c_last = ce[L - 1:L, :]
            lhs = jnp.concatenate([at * m0, at * m1, rt_ * m0, rt_ * m1], axis=0)
            rhs = jnp.concatenate([bt * m0, bt * m1, kt * m0, kt * m1, st], axis=0)
            bk = jnp.concatenate([bt * c_last, kt * c_last], axis=0)
            pre.append((lhs, rhs, bk, c_last))
        gms = [_dot_x(p_[0], p_[1], NT) for p_ in pre]
        wvs = [_dot_x(jnp.concatenate([gm[0:L2, L2:2 * L2] * strict, gm[L2:, L2:2 * L2] * incl], axis=0),
                      jnp.concatenate([s[2], s[2]], axis=0)) for gm, s in zip(gms, seqs)]
        xs = [gm[0:L2, 2 * L2:] + wv[0:L2] for gm, wv in zip(gms, wvs)]
        ps = [gm[0:L2, 0:L2] * strict for gm in gms]
        for q in range(nsq):
            if q + 1 < nsq:
                xps = [_dot_x(p, jnp.concatenate([x, p], axis=1)) for x, p in zip(xs, ps)]
                xs = [x + xp[:, 0:LANES] for x, xp in zip(xs, xps)]
                ps = [xp[:, LANES:] for xp in xps]
            else:
                xs = [x + _dot_x(p, x) for x, p in zip(xs, ps)]
        us = [x[0:L, :] * m0 + x[L:L2, :] * m1 for x in xs]
        rus = [_dot_x(gm[L2:, 0:L2] * incl, jnp.concatenate([u, u], axis=0)) for gm, u in zip(gms, us)]
        sus = [_dot_x(jnp.concatenate([u, s[2]], axis=0), p_[2], TN) for u, s, p_ in zip(us, seqs, pre)]
        out = []
        for gm, wv, ru, su, s, p_ in zip(gms, wvs, rus, sus, seqs, pre):
            y2 = gm[L2:, 2 * L2:] + wv[L2:] + ru
            out.append((y2[0:L, :] * m0 + y2[L:L2, :] * m1, s[7] * p_[3] + bd * su))
        return out

    def compute_all(loaded):
        nu = len(loaded)
        stack = lambda xs: xs[0] if nu == 1 else jnp.concatenate(xs, axis=0)
        part = lambda x, u: x[u * L:(u + 1) * L]
        zc = stack([x[0] for x in loaded])
        zprev = stack([jnp.where(row == 0, x[1], pltpu.roll(x[0], 1, 0)) for x in loaded])
        zs = zc + mu * (zprev - zc)
        r = zs[:, 0:LANES]
        k = zs[:, LANES:2 * LANES]
        v = zs[:, 2 * LANES:3 * LANES]
        dw = zs[:, 3 * LANES:4 * LANES]
        da = zs[:, 4 * LANES:5 * LANES]
        dg = zs[:, 5 * LANES:]

        w_log = -_softplus(-(w0 + _dot_hp(jnp.tanh(dw), wup_ref[...]))) - 0.5
        lw = -jnp.exp(w_log)
        a = _sigmoid(a0 + _dot_hp(da, aup_ref[...]))
        g = _dot_hp(_sigmoid(dg), gup_ref[...])
        kk = k * kkw
        k = k * (1.0 + (a - 1.0) * kaw)
        rows = nu * L
        sums = block_sum(jnp.concatenate([kk * kk, r * k * rkw], axis=0))
        kk = kk / jnp.maximum(jnp.sqrt(sums[0:rows]), 1e-12)
        bonus = sums[rows:] * v
        ka = -kk
        kb = kk * a
        if valid < L:
            ok = stack([row < valid] * nu)
            lw = jnp.where(ok, lw, 0.0)
            r = jnp.where(ok, r, 0.0)
            k = jnp.where(ok, k, 0.0)
            v = jnp.where(ok, v, 0.0)
            ka = jnp.where(ok, ka, 0.0)
            kb = jnp.where(ok, kb, 0.0)

        lw_wide = lw if nu == 1 else jnp.concatenate([part(lw, u) for u in range(nu)], axis=1)
        cum_wide = _dot_const(tril, lw_wide)
        res = recur([(part(r, u), part(k, u), part(v, u), part(ka, u), part(kb, u), part(lw, u),
                      cum_wide[:, u * LANES:(u + 1) * LANES], loaded[u][2]) for u in range(nu)])
        sts = [x[1] for x in res]
        y = stack([x[0] for x in res])
        mean = block_sum(y) * (1.0 / H)
        yc = y - mean
        var = block_sum(yc * yc) * (1.0 / H)
        yn = yc * lax.rsqrt(var + RWKV_LNX_EPS) * lnw + lnb
        out = (yn + bonus) * g
        return [(part(out, u).astype(y_ref.dtype), sts[u], loaded[u][0][valid - 1:valid, :]) for u in range(nu)]

    def store(seq, r0, is_end, y, st, last_row):
        y_ref[seq, pl.ds(r0, L), :] = y
        s_scr[seq] = st
        carry_scr[seq, 0:1, :] = last_row

        @pl.when(is_end)
        def _():
            sT_ref[seq, 0] = st[0:H, 0:H]
            sT_ref[seq, 1] = st[H:2 * H, H:2 * H]

    def step(it, carry):
        sg = it // nchunk
        ci_ = it - sg * nchunk
        r0 = pl.multiple_of(ci_ * L, L)
        is_start = jnp.logical_and(n == 0, ci_ == 0)
        is_end = jnp.logical_and(n == n_last, ci_ == nchunk - 1)
        seqs = [sg * unroll + u for u in range(unroll)]
        loaded = [load(s, r0, is_start) for s in seqs]
        done = compute_all(loaded)
        for s, d in zip(seqs, done):
            store(s, r0, is_end, *d)
        return carry

    lax.fori_loop(0, (bb // unroll) * nchunk, step, 0)


def _dot_x(a, b, dims=NN):
    return _dot(a.astype(BF16), b.astype(BF16), dims)


def _dot_const_rhs(a, c_bf16):
    h0 = a.astype(BF16)
    r1 = a - h0.astype(F32)
    h1 = r1.astype(BF16)
    h2 = (r1 - h1.astype(F32)).astype(BF16)
    return _dot(h0, c_bf16) + (_dot(h1, c_bf16) + _dot(h2, c_bf16))


def _rwkv(z, shift_p, wkv0, mu_p, w0, wup_p, a0, aup_p, g_up, k_k, k_a, r_k, lnx_w, lnx_b,
          *, n_seq, seq_rows, chunk, valid):
    unroll = _pick(n_seq, (4, 2, 1))
    if seq_rows > chunk:
        rt = _pick(seq_rows, (256, 128, 64))
        bb = unroll
    else:
        rt = seq_rows
        bb = _pick(n_seq, (32, 16, 8, 4, 2, 1))
    assert rt % chunk == 0 and bb % unroll == 0 and (bb == n_seq or bb % SUBLANES == 0)
    sblk = lambda c, g, n: (g, c, 0, 0)
    npair = RWKV_HEADS // 2
    nrb = LANES

    def cat(a):
        rows = a.shape[0]
        rkv = a[:, :3 * MIX_A].reshape(rows, 3, npair, LANES).transpose(2, 0, 1, 3).reshape(npair, rows, 3 * LANES)
        tail = jnp.broadcast_to(a[None, :, 3 * MIX_A:], (npair, rows, ZE_U - 3 * MIX_A))
        return jnp.concatenate([rkv, tail], axis=2)

    mu_cat = cat(mu_p.reshape(1, ZE_U))
    sh_cat = cat(shift_p)
    row2 = lambda a: a.reshape(1, MIX_A)
    vec = pl.BlockSpec((1, LANES), lambda c, g, n: (0, c))
    kern = functools.partial(_rwkv_body, bb=bb, rt=rt, chunk=chunk, valid=valid, unroll=unroll)
    y, s_t = pl.pallas_call(
        kern,
        grid=(npair, n_seq // bb, seq_rows // rt),
        in_specs=[
            pl.BlockSpec((bb, rt, nrb), lambda c, g, n: (g, n, c)),
            pl.BlockSpec((bb, rt, nrb), lambda c, g, n: (g, n, npair + c)),
            pl.BlockSpec((bb, rt, nrb), lambda c, g, n: (g, n, 2 * npair + c)),
            pl.BlockSpec((bb, rt, RANK_PAD), lambda c, g, n: (g, n, ZE_DW // RANK_PAD)),
            pl.BlockSpec((bb, rt, RANK_PAD), lambda c, g, n: (g, n, ZE_DA // RANK_PAD)),
            pl.BlockSpec((bb, rt, RWKV_GATE_RANK), lambda c, g, n: (g, n, ZE_DG // RWKV_GATE_RANK)),
            pl.BlockSpec((None, 1, RW_CAT), lambda c, g, n: (c, 0, 0)),
            pl.BlockSpec((None, bb, RW_CAT), lambda c, g, n: (c, g, 0)),
            pl.BlockSpec((bb, 2, RWKV_HEAD, RWKV_HEAD), sblk),
            vec,
            pl.BlockSpec((RANK_PAD, LANES), lambda c, g, n: (0, c)),
            vec,
            pl.BlockSpec((RANK_PAD, LANES), lambda c, g, n: (0, c)),
            pl.BlockSpec((RWKV_GATE_RANK, LANES), lambda c, g, n: (0, c)),
            vec, vec, vec, vec, vec,
        ],
        out_specs=[pl.BlockSpec((bb, rt, LANES), lambda c, g, n: (g, n, c)),
                   pl.BlockSpec((bb, 2, RWKV_HEAD, RWKV_HEAD), sblk)],
        out_shape=[jax.ShapeDtypeStruct((n_seq, seq_rows, MIX_A), BF16),
                   jax.ShapeDtypeStruct((n_seq, RWKV_HEADS, RWKV_HEAD, RWKV_HEAD), F32)],
        scratch_shapes=[pltpu.VMEM((bb, LANES, LANES), F32), pltpu.VMEM((bb, SUBLANES, RW_CAT), F32)],
        compiler_params=_cparams(("parallel", "parallel", "arbitrary")),
        name="rwkv7",
    )(z, z, z, z, z, z, mu_cat, sh_cat, wkv0,
      row2(w0), wup_p, row2(a0), aup_p, g_up, row2(k_k), row2(k_a), row2(r_k), row2(lnx_w), row2(lnx_b))
    return y, s_t


def _rwkv_cat(a):
    npair = RWKV_HEADS // 2
    rows = a.shape[0]
    rkv = a[:, :3 * MIX_A].reshape(rows, 3, npair, LANES).transpose(2, 0, 1, 3).reshape(npair, rows, 3 * LANES)
    tail = jnp.broadcast_to(a[None, :, 3 * MIX_A:], (npair, rows, ZE_U - 3 * MIX_A))
    return jnp.concatenate([rkv, tail], axis=2)


def _rwkv_step_body(zr_ref, zk_ref, zv_ref, zdw_ref, zda_ref, zdg_ref, mu_ref, sh_ref, s0_ref,
                    w0_ref, wup_ref, a0_ref, aup_ref, gup_ref, kk_ref, ka_ref, rk_ref, lw_ref, lb_ref,
                    y_ref, sT_ref, op_scr, yt_scr, *, steps, nb):
    H = RWKV_HEAD
    rows = steps * nb
    ri = _iota((LANES, LANES), 0)
    ci = _iota((LANES, LANES), 1)
    bones = jnp.where((ri >= H) == (ci >= H), 1.0, 0.0).astype(BF16)
    block_sum = lambda x: _dot_const_rhs(x, bones)

    zc = jnp.concatenate([ref[...].reshape(rows, ref.shape[2]) for ref in (zr_ref, zk_ref, zv_ref, zdw_ref, zda_ref, zdg_ref)],
                         axis=1)
    zprev = jnp.concatenate([sh_ref[...], zc[0:rows - nb]], axis=0) if steps > 1 else sh_ref[...]
    zs = zc + mu_ref[...] * (zprev - zc)
    r = zs[:, 0:LANES]
    k = zs[:, LANES:2 * LANES]
    v = zs[:, 2 * LANES:3 * LANES]
    dw = zs[:, 3 * LANES:4 * LANES]
    da = zs[:, 4 * LANES:5 * LANES]
    dg = zs[:, 5 * LANES:]
    w_log = -_softplus(-(w0_ref[...] + _dot_hp(jnp.tanh(dw), wup_ref[...]))) - 0.5
    decay = jnp.exp(-jnp.exp(w_log))
    a = _sigmoid(a0_ref[...] + _dot_hp(da, aup_ref[...]))
    g = _dot_hp(_sigmoid(dg), gup_ref[...])
    kk = k * kk_ref[...]
    k = k * (1.0 + (a - 1.0) * ka_ref[...])
    sums = block_sum(jnp.concatenate([kk * kk, r * k * rk_ref[...]], axis=0))
    kk = kk / jnp.maximum(jnp.sqrt(sums[0:rows]), 1e-12)
    bonus = sums[rows:] * v

    for qi, x in enumerate((decay, -kk, kk * a, k, r, v)):
        for t in range(steps):
            op_scr[qi, t] = x[t * nb:(t + 1) * nb, :].T

    for t in range(steps):
        src = s0_ref if t == 0 else sT_ref
        for hh in range(2):
            lo = hh * H
            w_t = op_scr[0, t, lo:lo + H, :]
            ka_t = op_scr[1, t, lo:lo + H, :]
            kb_t = op_scr[2, t, lo:lo + H, :]
            k_t = op_scr[3, t, lo:lo + H, :]
            r_t = op_scr[4, t, lo:lo + H, :]

            def value_row(i, carry, src=src, hh=hh, lo=lo, t=t, w_t=w_t, ka_t=ka_t, kb_t=kb_t, k_t=k_t, r_t=r_t):
                s_i = src[hh, i]
                sa = jnp.sum(s_i * ka_t, axis=0, keepdims=True)
                v_i = op_scr[5, t, pl.ds(lo + i, 1), :]
                s_n = s_i * w_t + sa * kb_t + v_i * k_t
                sT_ref[hh, i] = s_n
                yt_scr[t, pl.ds(lo + i, 1), :] = jnp.sum(s_n * r_t, axis=0, keepdims=True)
                return carry

            lax.fori_loop(0, H, value_row, 0, unroll=4)

    y = jnp.concatenate([yt_scr[t].T for t in range(steps)], axis=0)
    mean = block_sum(y) * (1.0 / H)
    yc = y - mean
    var = block_sum(yc * yc) * (1.0 / H)
    yn = yc * lax.rsqrt(var + RWKV_LNX_EPS) * lw_ref[...] + lb_ref[...]
    y_ref[...] = ((yn + bonus) * g).reshape(steps, nb, LANES).astype(y_ref.dtype)


def _rwkv_step(z_tb, shift_p, wkv0_t, mu_p, w0, wup_p, a0, aup_p, g_up, k_k, k_a, r_k, lnx_w, lnx_b):
    steps, nb, _ = z_tb.shape
    assert nb == LANES
    npair = RWKV_HEADS // 2
    row2 = lambda a: a.reshape(1, MIX_A)
    vec = pl.BlockSpec((1, LANES), lambda c: (0, c))
    zspec = lambda width, idx: pl.BlockSpec((steps, nb, width), lambda c: (0, 0, idx(c)))
    sspec = pl.BlockSpec((2, RWKV_HEAD, RWKV_HEAD, nb), lambda c: (c, 0, 0, 0))
    kern = functools.partial(_rwkv_step_body, steps=steps, nb=nb)
    return pl.pallas_call(
        kern,
        grid=(npair,),
        in_specs=[
            zspec(LANES, lambda c: c), zspec(LANES, lambda c: npair + c), zspec(LANES, lambda c: 2 * npair + c),
            zspec(RANK_PAD, lambda c: ZE_DW // RANK_PAD), zspec(RANK_PAD, lambda c: ZE_DA // RANK_PAD),
            zspec(RWKV_GATE_RANK, lambda c: ZE_DG // RWKV_GATE_RANK),
            pl.BlockSpec((None, 1, RW_CAT), lambda c: (c, 0, 0)),
            pl.BlockSpec((None, nb, RW_CAT), lambda c: (c, 0, 0)),
            sspec,
            vec,
            pl.BlockSpec((RANK_PAD, LANES), lambda c: (0, c)),
            vec,
            pl.BlockSpec((RANK_PAD, LANES), lambda c: (0, c)),
            pl.BlockSpec((RWKV_GATE_RANK, LANES), lambda c: (0, c)),
            vec, vec, vec, vec, vec,
        ],
        out_specs=[pl.BlockSpec((steps, nb, LANES), lambda c: (0, 0, c)), sspec],
        out_shape=[jax.ShapeDtypeStruct((steps, nb, MIX_A), BF16),
                   jax.ShapeDtypeStruct((RWKV_HEADS, RWKV_HEAD, RWKV_HEAD, nb), F32)],
        scratch_shapes=[pltpu.VMEM((6, steps, LANES, nb), F32), pltpu.VMEM((steps, LANES, nb), F32)],
        compiler_params=_cparams(("parallel",)),
        name="rwkv7_step",
    )(z_tb, z_tb, z_tb, z_tb, z_tb, z_tb, _rwkv_cat(mu_p.reshape(1, ZE_U)), _rwkv_cat(shift_p), wkv0_t,
      row2(w0), wup_p, row2(a0), aup_p, g_up, row2(k_k), row2(k_a), row2(r_k), row2(lnx_w), row2(lnx_b))


def _s5_prep_body(are_ref, aim_ref, ls_ref, abre_ref, abim_ref, zr_ref, zi_ref):
    lam_re = are_ref[...]
    lam_im = aim_ref[...]
    dt = jnp.exp(ls_ref[...])
    mag = jnp.exp(lam_re * dt)
    ab_re = mag * jnp.cos(lam_im * dt)
    ab_im = mag * jnp.sin(lam_im * dt)
    inv = 1.0 / (lam_re * lam_re + lam_im * lam_im)
    abre_ref[...] = ab_re
    abim_ref[...] = ab_im
    zr_ref[...] = ((ab_re - 1.0) * lam_re + ab_im * lam_im) * inv
    zi_ref[...] = (ab_im * lam_re - (ab_re - 1.0) * lam_im) * inv


def _s5_prep(a_re, a_im, log_step):
    g, n = a_re.shape
    sd = jax.ShapeDtypeStruct((g, n), F32)
    return pl.pallas_call(_s5_prep_body, out_shape=[sd, sd, sd, sd], name="s5_discretise")(
        a_re, a_im, jnp.broadcast_to(log_step.reshape(g, 1), (g, n)))


def _gelu_tanh(x):
    return 0.5 * x * (1.0 + jnp.tanh(math.sqrt(2.0 / math.pi) * (x + 0.044715 * (x * x * x))))


def _s5_body(u_ref, wbr_ref, wbi_ref, wcr_ref, wci_ref, abr_ref, abi_ref, d_ref, h0r_ref, h0i_ref,
             y_ref, hTr_ref, hTi_ref, hr_scr, hi_scr, pr_scr, pi_scr, cr_scr, ci_scr, u_st, y_st, *, rt, seq_rows):
    n = pl.program_id(1)
    long_seq = seq_rows >= rt
    grp = SUBLANES if long_seq else rt // seq_rows
    steps = rt // grp
    ar1 = abr_ref[...]
    ai1 = abi_ref[...]
    ar = jnp.broadcast_to(ar1, (grp, ar1.shape[1]))
    ai = jnp.broadcast_to(ai1, (grp, ai1.shape[1]))
    nhalf = u_ref.shape[1] // LANES
    for hf in range(nhalf):
        u_st[hf] = u_ref[:, hf * LANES:(hf + 1) * LANES]
    u = jnp.concatenate(
        [jnp.concatenate([u_st[hf, pl.ds(j, grp, stride=steps), :] for hf in range(nhalf)], axis=1) for j in range(steps)],
        axis=0)
    ub = u.astype(BF16)
    hr_scr[...] = _dot(ub, wbr_ref[...].astype(BF16))
    hi_scr[...] = _dot(ub, wbi_ref[...].astype(BF16))

    def rows(j):
        return pl.ds(j * grp, grp) if isinstance(j, int) else pl.ds(pl.multiple_of(j * grp, grp), grp)

    def scan_step(j, c):
        hr, hi = c
        nr = ar * hr - ai * hi + hr_scr[rows(j), :]
        ni = ar * hi + ai * hr + hi_scr[rows(j), :]
        hr_scr[rows(j), :] = nr
        hi_scr[rows(j), :] = ni
        return nr, ni

    if long_seq:
        @pl.when(n == 0)
        def _():
            def pw_step(j, c):
                qr, qi = c
                pr_scr[pl.ds(j, 1), :] = qr
                pi_scr[pl.ds(j, 1), :] = qi
                return qr * ar1 - qi * ai1, qr * ai1 + qi * ar1
            lax.fori_loop(0, steps, pw_step, (ar1, ai1))
            cr_scr[...] = jnp.zeros_like(cr_scr)
            ci_scr[...] = jnp.zeros_like(ci_scr)

        b = (n * rt) // seq_rows
        is_start = ((n * rt) % seq_rows) == 0
        h_in_r = jnp.where(is_start, h0r_ref[pl.ds(b, 1), :], cr_scr[0:1, :])
        h_in_i = jnp.where(is_start, h0i_ref[pl.ds(b, 1), :], ci_scr[0:1, :])
        zero = jnp.zeros((grp, hr_scr.shape[1]), F32)
        er, ei = lax.fori_loop(0, steps, scan_step, (zero, zero), unroll=4)
        pS_r = pr_scr[steps - 1:steps, :]
        pS_i = pi_scr[steps - 1:steps, :]
        cr_rows, ci_rows = [h_in_r], [h_in_i]
        for s in range(grp):
            pr_, pi_ = cr_rows[-1], ci_rows[-1]
            cr_rows.append(er[s:s + 1, :] + pS_r * pr_ - pS_i * pi_)
            ci_rows.append(ei[s:s + 1, :] + pS_r * pi_ + pS_i * pr_)
        c_r = jnp.concatenate(cr_rows[:grp], axis=0)
        c_i = jnp.concatenate(ci_rows[:grp], axis=0)
        cr_scr[0:1, :] = cr_rows[grp]
        ci_scr[0:1, :] = ci_rows[grp]
        hTr_ref[pl.ds(b, 1), :] = cr_rows[grp]
        hTi_ref[pl.ds(b, 1), :] = ci_rows[grp]

        def fix_step(j, carry):
            qr = pr_scr[pl.ds(j, 1), :]
            qi = pi_scr[pl.ds(j, 1), :]
            hr_scr[rows(j), :] = hr_scr[rows(j), :] + (qr * c_r - qi * c_i)
            hi_scr[rows(j), :] = hi_scr[rows(j), :] + (qr * c_i + qi * c_r)
            return carry

        lax.fori_loop(0, steps, fix_step, 0, unroll=4)
    else:
        hr, hi = h0r_ref[...], h0i_ref[...]
        for t in range(steps):
            hr, hi = scan_step(t, (hr, hi))
        hTr_ref[...] = hr
        hTi_ref[...] = hi

    y = _dot(hr_scr[...].astype(BF16), wcr_ref[...].astype(BF16)) + _dot(hi_scr[...].astype(BF16), wci_ref[...].astype(BF16))
    y = _gelu_tanh(y + d_ref[...] * u)
    for hf in range(nhalf):
        y_st[hf] = y[:, hf * LANES:(hf + 1) * LANES]
    for hf in range(nhalf):
        for s in range(grp):
            y_ref[s * steps:(s + 1) * steps, hf * LANES:(hf + 1) * LANES] = y_st[hf, pl.ds(s, steps, stride=grp), :]


def _s5_scan(z, col0, wbr, wbi, wcr, wci, ab_re, ab_im, d_skip, h0_re, h0_im, *, n_seq, seq_rows):
    m = z.shape[0]
    st = S5_SLAB_STATE
    if seq_rows >= 8 * SUBLANES:
        rt = _pick(seq_rows, (512, 256, 128, 64))
        hspec = pl.BlockSpec((n_seq, st), lambda s, n: (0, s))
        pw_rows = rt // SUBLANES
    else:
        rt = seq_rows * _pick(n_seq, (128, 64, 32, 16, 8))
        hspec = pl.BlockSpec((rt // seq_rows, st), lambda s, n: (n, s))
        pw_rows = SUBLANES
    win = S5_SLAB_IN
    cb0 = col0 // win
    kern = functools.partial(_s5_body, rt=rt, seq_rows=seq_rows)
    wspec_b = pl.BlockSpec((None, win, st), lambda s, n: (s, 0, 0))
    wspec_c = pl.BlockSpec((None, st, win), lambda s, n: (s, 0, 0))
    vspec = pl.BlockSpec((1, st), lambda s, n: (0, s))
    return pl.pallas_call(
        kern,
        grid=(S5_SLABS, m // rt),
        in_specs=[pl.BlockSpec((rt, win), lambda s, n: (n, cb0 + s)), wspec_b, wspec_b, wspec_c, wspec_c,
                  vspec, vspec, pl.BlockSpec((1, win), lambda s, n: (0, s)), hspec, hspec],
        out_specs=[pl.BlockSpec((rt, win), lambda s, n: (n, s)), hspec, hspec],
        out_shape=[jax.ShapeDtypeStruct((m, MIX_B), F32),
                   jax.ShapeDtypeStruct((n_seq, S5_GROUPS * S5_STATE), F32),
                   jax.ShapeDtypeStruct((n_seq, S5_GROUPS * S5_STATE), F32)],
        scratch_shapes=[pltpu.VMEM((rt, st), F32), pltpu.VMEM((rt, st), F32),
                        pltpu.VMEM((pw_rows, st), F32), pltpu.VMEM((pw_rows, st), F32),
                        pltpu.VMEM((SUBLANES, st), F32), pltpu.VMEM((SUBLANES, st), F32),
                        pltpu.VMEM((win // LANES, rt, LANES), F32), pltpu.VMEM((win // LANES, rt, LANES), F32)],
        compiler_params=_cparams(("parallel", "arbitrary")),
        name="s5_scan",
    )(z, wbr, wbi, wcr, wci, ab_re.reshape(1, -1), ab_im.reshape(1, -1), d_skip.reshape(1, -1), h0_re, h0_im)


def _glu_body(a_ref, w_ref, b_ref, y_ref, o_ref):
    t = _dot(a_ref[...].astype(BF16), w_ref[...].astype(BF16)) + b_ref[...]
    o_ref[...] = (y_ref[...] * _sigmoid(t)).astype(o_ref.dtype)


def _glu(y, w, b):
    m, k = y.shape
    n = w.shape[1]
    tm = _pick(m, (1024, 512, 256, 128, 64, 32, 16, 8))
    tn = _pick(n, (512, 256, 128))
    return pl.pallas_call(
        _glu_body,
        grid=(m // tm, n // tn),
        in_specs=[pl.BlockSpec((tm, k), lambda i, j: (i, 0)), pl.BlockSpec((k, tn), lambda i, j: (0, j)),
                  pl.BlockSpec((1, tn), lambda i, j: (0, j)), pl.BlockSpec((tm, tn), lambda i, j: (i, j))],
        out_specs=pl.BlockSpec((tm, tn), lambda i, j: (i, j)),
        out_shape=jax.ShapeDtypeStruct((m, n), BF16),
        compiler_params=_cparams(("parallel", "arbitrary")),
        name="s5_glu",
    )(y, w, b.reshape(1, n), y)


def _mlstm_body(q_ref, k_ref, v_ref, og_ref, gt_ref, gb_ref, nw_ref, c0_ref, n0_ref, m0_ref,
                y_ref, cT_ref, nT_ref, mT_ref, c_scr, n_scr, m_scr, *, bb, rt, chunk, valid, unroll):
    h = pl.program_id(0)
    n = pl.program_id(2)
    n_last = pl.num_programs(2) - 1
    L = chunk
    nchunk = rt // L
    rl = _iota((L, L), 0)
    cl = _iota((L, L), 1)
    causal = rl >= cl
    tril = jnp.where(causal, 1.0, 0.0).astype(BF16)
    row = _iota((L, 1), 0)
    lane = _iota((L, LANES), 1)
    gbias = gb_ref[...]
    nw = nw_ref[...]
    scale = MLSTM_DQK ** -0.5

    def load(seq, r0, is_start):
        c_old = jnp.where(is_start, c0_ref[seq, 0], c_scr[seq])
        n_old = jnp.where(is_start, n0_ref[seq, 0], n_scr[seq, 0:1, :])
        m_prev = jnp.where(is_start, m0_ref[seq, 0], m_scr[seq, 0:1, 0:1])
        rows = pl.ds(r0, L)
        return (q_ref[seq, rows, :], k_ref[seq, rows, :], v_ref[seq, rows, :], og_ref[seq, rows, :],
                gt_ref[seq, rows, :], c_old, n_old, m_prev)

    def compute_all(loaded):
        nu = len(loaded)
        igs, lfs = [], []
        for q, k, v, og, gt, c_old, n_old, m_prev in loaded:
            gt = gt + gbias
            ig_raw = jnp.sum(jnp.where(lane == h, gt, 0.0), axis=-1, keepdims=True)
            fg_raw = jnp.sum(jnp.where(lane == MLSTM_HEADS + h, gt, 0.0), axis=-1, keepdims=True)
            ig = GATE_CAP * jnp.tanh(ig_raw / GATE_CAP)
            lf = -_softplus(-(GATE_CAP * jnp.tanh(fg_raw / GATE_CAP)))
            if valid < L:
                ok = row < valid
                ig = jnp.where(ok, ig, NEG_BIG)
                lf = jnp.where(ok, lf, 0.0)
            igs.append(ig)
            lfs.append(lf)
        lf_mat = jnp.zeros((L, LANES), F32)
        for u in range(nu):
            lf_mat = jnp.where(lane == u, lfs[u], lf_mat)
        b_mat = _dot_const(tril, lf_mat)
        pack = b_mat
        for u in range(nu):
            pack = jnp.where(lane == nu + u, igs[u], pack)
        pack_t = pack.T
        gates, qbs, kbs, ks = [], [], [], []
        for u, (q, k, v, og, gt, c_old, n_old, m_prev) in enumerate(loaded):
            bcol = b_mat[:, u:u + 1]
            b_row = pack_t[u:u + 1, :]
            ig_row = pack_t[nu + u:nu + u + 1, :]
            log_d = jnp.where(causal, bcol - b_row + ig_row, NEG_BIG)
            log_p = bcol + m_prev
            m_tok = jnp.maximum(log_p, jnp.max(log_d, axis=-1, keepdims=True))
            gates.append((jnp.exp(log_d - m_tok), jnp.exp(log_p - m_tok), m_tok, bcol))
            k = k * scale
            ks.append(k)
            qbs.append(q.astype(BF16))
            kbs.append(k.astype(BF16))
        ss = [_dot(qb, kb, NT) * g_[0] for qb, kb, g_ in zip(qbs, kbs, gates)]
        svs = [_dot(s.astype(BF16), x[2].astype(BF16)) for s, x in zip(ss, loaded)]
        qcs = [_dot(qb, x[5].astype(BF16), NT) for qb, x in zip(qbs, loaded)]
        upd = []
        for u, (q, k, v, og, gt, c_old, n_old, m_prev) in enumerate(loaded):
            d, p, m_tok, bcol = gates[u]
            m_new = m_tok[L - 1:L, :]
            b_last = bcol[L - 1:L, :]
            w_col = jnp.exp(b_last - bcol + igs[u] - m_new)
            cs = jnp.exp(b_last + m_prev - m_new)
            upd.append((w_col, cs, m_new))
        vks = [_dot((x[2] * w_[0]).astype(BF16), kb, TN) for x, w_, kb in zip(loaded, upd, kbs)]
        out = []
        for u, (q, k, v, og, gt, c_old, n_old, m_prev) in enumerate(loaded):
            d, p, m_tok, bcol = gates[u]
            w_col, cs, m_new = upd[u]
            num = svs[u] + p * qcs[u]
            den = jnp.sum(ss[u], axis=-1, keepdims=True) + p * jnp.sum(q * n_old, axis=-1, keepdims=True)
            hh = num / jnp.maximum(jnp.abs(den), jnp.exp(-m_tok))
            c_new = cs * c_old + vks[u]
            n_new = cs * n_old + jnp.sum(w_col * ks[u], axis=0, keepdims=True)
            hn = hh * lax.rsqrt(jnp.mean(hh * hh, axis=-1, keepdims=True) + RMS_EPS)
            out.append(((hn * nw * _sigmoid(og)).astype(y_ref.dtype), c_new, n_new, m_new))
        return out

    def store(seq, r0, is_end, y, c_new, n_new, m_new):
        y_ref[seq, pl.ds(r0, L), :] = y
        c_scr[seq] = c_new
        n_scr[seq, 0:1, :] = n_new
        m_scr[seq, 0:1, :] = jnp.broadcast_to(m_new, (1, LANES))

        @pl.when(is_end)
        def _():
            cT_ref[seq, 0] = c_new
            nT_ref[seq, 0] = n_new
            mT_ref[seq, 0] = m_new

    def step(it, carry):
        sg = it // nchunk
        ci_ = it - sg * nchunk
        r0 = pl.multiple_of(ci_ * L, L)
        is_start = jnp.logical_and(n == 0, ci_ == 0)
        is_end = jnp.logical_and(n == n_last, ci_ == nchunk - 1)
        seqs = [sg * unroll + u for u in range(unroll)]
        loaded = [load(s, r0, is_start) for s in seqs]
        done = compute_all(loaded)
        for s, d in zip(seqs, done):
            store(s, r0, is_end, *d)
        return carry

    @pl.when(n == 0)
    def _():
        c_scr[...] = jnp.zeros_like(c_scr)
        n_scr[...] = jnp.zeros_like(n_scr)
        m_scr[...] = jnp.zeros_like(m_scr)

    lax.fori_loop(0, (bb // unroll) * nchunk, step, 0)


def _mlstm(z, c0, n0, m0, b_i, b_f, norm_w, *, n_seq, seq_rows, chunk, valid):
    nh, dqk, dv = MLSTM_HEADS, MLSTM_DQK, MLSTM_DV
    unroll = _pick(n_seq, (4, 2, 1))
    if seq_rows > chunk:
        rt = _pick(seq_rows, (256, 128, 64))
        bb = unroll
    else:
        rt = seq_rows
        bb = _pick(n_seq, (8, 4, 2, 1))
    assert rt % chunk == 0 and bb % unroll == 0
    sblk = lambda h, g, n: (g, h, 0, 0)
    gbias = jnp.pad(jnp.concatenate([b_i, b_f]).reshape(1, 2 * nh), ((0, 0), (0, LANES - 2 * nh)))
    n0r = n0.reshape(n_seq, nh, 1, dqk)
    m0r = m0.reshape(n_seq, nh, 1, 1)
    kern = functools.partial(_mlstm_body, bb=bb, rt=rt, chunk=chunk, valid=valid, unroll=unroll)
    y, c_t, n_t, m_t = pl.pallas_call(
        kern,
        grid=(nh, n_seq // bb, seq_rows // rt),
        in_specs=[
            pl.BlockSpec((bb, rt, dqk), lambda h, g, n: (g, n, h)),
            pl.BlockSpec((bb, rt, dqk), lambda h, g, n: (g, n, nh + h)),
            pl.BlockSpec((bb, rt, dv), lambda h, g, n: (g, n, (2 * nh * dqk) // dv + h)),
            pl.BlockSpec((bb, rt, dv), lambda h, g, n: (g, n, ZO_OG // dv + h)),
            pl.BlockSpec((bb, rt, LANES), lambda h, g, n: (g, n, ZO_G // LANES)),
            pl.BlockSpec((1, LANES), lambda h, g, n: (0, 0)),
            pl.BlockSpec((1, dv), lambda h, g, n: (0, h)),
            pl.BlockSpec((bb, 1, dv, dqk), sblk),
            pl.BlockSpec((bb, 1, 1, dqk), sblk),
            pl.BlockSpec((bb, 1, 1, 1), sblk),
        ],
        out_specs=[pl.BlockSpec((bb, rt, dv), lambda h, g, n: (g, n, h)),
                   pl.BlockSpec((bb, 1, dv, dqk), sblk),
                   pl.BlockSpec((bb, 1, 1, dqk), sblk),
                   pl.BlockSpec((bb, 1, 1, 1), sblk)],
        out_shape=[jax.ShapeDtypeStruct((n_seq, seq_rows, nh * dv), BF16),
                   jax.ShapeDtypeStruct((n_seq, nh, dv, dqk), F32),
                   jax.ShapeDtypeStruct((n_seq, nh, 1, dqk), F32),
                   jax.ShapeDtypeStruct((n_seq, nh, 1, 1), F32)],
        scratch_shapes=[pltpu.VMEM((bb, dv, dqk), F32), pltpu.VMEM((bb, SUBLANES, dqk), F32),
                        pltpu.VMEM((bb, SUBLANES, LANES), F32)],
        compiler_params=_cparams(("parallel", "parallel", "arbitrary")),
        name="mlstm",
    )(z, z, z, z, z, gbias, norm_w.reshape(1, nh * dv), c0, n0r, m0r)
    return y, c_t, n_t.reshape(n_seq, nh, dqk), m_t.reshape(n_seq, nh)


def _pad_cols(a, n):
    return jnp.pad(a, ((0, 0), (0, n - a.shape[1])))


def _rwkv_cols(a):
    c1 = 3 * MIX_A
    c2 = c1 + RWKV_DECAY_RANK
    c3 = c2 + RWKV_ICL_RANK
    return jnp.concatenate([a[:, :c1], _pad_cols(a[:, c1:c2], RANK_PAD), _pad_cols(a[:, c2:c3], RANK_PAD), a[:, c3:]], axis=1)


def _rwkv_cols_inv(a):
    return jnp.concatenate([a[:, :ZE_DW + RWKV_DECAY_RANK], a[:, ZE_DA:ZE_DA + RWKV_ICL_RANK], a[:, ZE_DG:ZE_U]], axis=1)


def _pad_seq(a, n_seq, t, tp):
    a = a.reshape(n_seq, t, a.shape[1])
    return a if tp == t else jnp.pad(a, ((0, 0), (0, tp - t), (0, 0)))


def _unpad_seq(a, n_seq, t, tp):
    a = a if tp == t else a[:, :t]
    return a.reshape(n_seq * t, a.shape[2])


def _s5_block_weights(zr, zi, b_re, b_im, c_re, c_im):
    g, n, c = S5_GROUPS, S5_STATE, S5_GROUP
    sg = S5_SLAB_GROUPS
    bb_re = zr[..., None] * b_re - zi[..., None] * b_im
    bb_im = zr[..., None] * b_im + zi[..., None] * b_re
    eye = jnp.eye(sg, dtype=F32)

    def in_blocks(bb):
        t = bb.reshape(S5_SLABS, sg, n, c)
        return jnp.einsum('sgnc,gh->sgchn', t, eye).reshape(S5_SLABS, sg * c, sg * n)

    def out_blocks(cc):
        t = cc.reshape(S5_SLABS, sg, c, n)
        return jnp.einsum('sgcn,gh->sgnhc', t, eye).reshape(S5_SLABS, sg * n, sg * c)

    return in_blocks(bb_re), in_blocks(bb_im), out_blocks(c_re), out_blocks(-c_im)


def _trunk(x, n_seq, t, st, w):
    tp = t if t >= MLSTM_CHUNK else SUBLANES * ((t + SUBLANES - 1) // SUBLANES)
    chunk = MLSTM_CHUNK if t >= MLSTM_CHUNK else tp
    valid = chunk if tp == t else t
    assert t % chunk == 0 or tp == chunk

    xn = _rmsnorm(x, w['ln_mix_e'], BF16)
    z = _mm_split(xn, w['w_in_e'], ZE_DW, w['w_in_e_tail'])
    rw = (w['mu'], w['w0'], w['w_up'], w['a0'], w['a_up'], w['g_up'], w['k_k'], w['k_a'], w['r_k'], w['lnx_w'], w['lnx_b'])
    if t < chunk and n_seq == LANES:
        z_tb = z.reshape(n_seq, t, ZE_W).transpose(1, 0, 2)
        ya, wkv_t = _rwkv_step(z_tb, _rwkv_cols(st['shift']), st['wkv'].transpose(1, 2, 3, 0), *rw)
        ya = ya.transpose(1, 0, 2).reshape(n_seq * t, MIX_A)
        wkv_t = wkv_t.transpose(3, 0, 1, 2)
    else:
        ya, wkv_t = _rwkv(_pad_seq(z, n_seq, t, tp), _rwkv_cols(st['shift']), st['wkv'], *rw,
                          n_seq=n_seq, seq_rows=tp, chunk=chunk, valid=valid)
        ya = _unpad_seq(ya, n_seq, t, tp)
    shift_t = _rwkv_cols_inv(z.reshape(n_seq, t, ZE_W)[:, -1, :ZE_U])
    yb, re_t, im_t = _s5_scan(z, ZE_U, w['s5_wbr'], w['s5_wbi'], w['s5_wcr'], w['s5_wci'], w['s5_ab_re'], w['s5_ab_im'],
                              w['s5_d'], st['s5_re'], st['s5_im'], n_seq=n_seq, seq_rows=t)
    yb = _glu(yb, w['s5_glu_w'], w['s5_glu_b'])
    x = _mm2_res(ya, yb, w['w_out_e'], x)
    hn = _rmsnorm(x, w['ln_ffn_e'], BF16)
    hid = _glu_up(hn, w['ffn_w1'], w['ffn_w3'])
    x = _mm_acc(hid, w['ffn_w2'], x)

    xn = _rmsnorm(x, w['ln_mix_o'], BF16)
    zo = _mm_split(xn, w['w_in_o'], ZO_OG, w['w_in_o_tail'])
    zop = _pad_seq(zo, n_seq, t, tp)
    yc, c_t, n_t, m_t = _mlstm(zop, st['c'], st['n'], st['m'], w['b_i'], w['b_f'], w['norm_w'],
                               n_seq=n_seq, seq_rows=tp, chunk=chunk, valid=valid)
    yc = _unpad_seq(yc, n_seq, t, tp)
    x = _mm_res(yc, w['w_out_o'], x)
    hn, comb = _rmsnorm_router(x, w['ln_ffn_o'], w['router_w'], w['router_b'])
    hid = _moe_up(hn, comb, w['exp_w1'], w['exp_w3'])
    x = _mm_acc(hid, w['exp_w2'], x)
    y = _rmsnorm(x, w['final_norm'], F32)

    g, ns = S5_GROUPS, S5_STATE
    return (y, wkv_t[None], shift_t[None], re_t.reshape(1, n_seq, g, ns), im_t.reshape(1, n_seq, g, ns),
            c_t[None], n_t[None], m_t[None])


def kernel(x_prompt, x_sample, state_rwkv_wkv, state_rwkv_shift, state_s5_re, state_s5_im, state_mlstm_c, state_mlstm_n, state_mlstm_m, ln_mix_e, w_in_e, rwkv_mu, rwkv_w0, rwkv_w_up, rwkv_a0, rwkv_a_up, rwkv_g_up, rwkv_k_k, rwkv_k_a, rwkv_r_k, rwkv_lnx_w, rwkv_lnx_b, s5_a_re, s5_a_im, s5_log_step, s5_b_re, s5_b_im, s5_c_re, s5_c_im, s5_d, s5_glu_w, s5_glu_b, w_out_e, ln_ffn_e, ffn_w1, ffn_w3, ffn_w2, ln_mix_o, w_in_o, mlstm_b_i, mlstm_b_f, mlstm_norm_w, w_out_o, ln_ffn_o, router_w, router_b, exp_w1, exp_w3, exp_w2, final_norm):
    assert ln_mix_e.shape[0] == 1 and ln_mix_o.shape[0] == 1
    d = D_MODEL
    ab_re, ab_im, zr, zi = _s5_prep(s5_a_re[0], s5_a_im[0], s5_log_step[0])
    wbr, wbi, wcr, wci = _s5_block_weights(zr, zi, s5_b_re[0], s5_b_im[0], s5_c_re[0], s5_c_im[0])
    wo = w_in_o[0]
    we = w_in_e[0]
    c_dw = 3 * MIX_A
    c_da = c_dw + RWKV_DECAY_RANK
    c_dg = c_da + RWKV_ICL_RANK
    nh = MLSTM_HEADS
    w = {
        'ln_mix_e': ln_mix_e[0],
        'w_in_e': w_in_e,
        'w_in_e_tail': jnp.concatenate([_pad_cols(we[:, c_dw:c_da], RANK_PAD), _pad_cols(we[:, c_da:c_dg], RANK_PAD),
                                        we[:, c_dg:]], axis=1),
        'mu': _rwkv_cols(rwkv_mu[0].reshape(1, RWKV_IN)),
        'w0': rwkv_w0[0], 'a0': rwkv_a0[0],
        'w_up': jnp.pad(rwkv_w_up[0], ((0, RANK_PAD - RWKV_DECAY_RANK), (0, 0))),
        'a_up': jnp.pad(rwkv_a_up[0], ((0, RANK_PAD - RWKV_ICL_RANK), (0, 0))),
        'g_up': rwkv_g_up[0], 'k_k': rwkv_k_k[0], 'k_a': rwkv_k_a[0], 'r_k': rwkv_r_k[0].reshape(MIX_A),
        'lnx_w': rwkv_lnx_w[0], 'lnx_b': rwkv_lnx_b[0],
        's5_wbr': wbr, 's5_wbi': wbi, 's5_wcr': wcr, 's5_wci': wci,
        's5_ab_re': ab_re, 's5_ab_im': ab_im, 's5_d': s5_d[0], 's5_glu_w': s5_glu_w[0], 's5_glu_b': s5_glu_b[0],
        'w_out_e': w_out_e[0], 'ln_ffn_e': ln_ffn_e[0],
        'ffn_w1': ffn_w1[0], 'ffn_w3': ffn_w3[0], 'ffn_w2': ffn_w2[0],
        'ln_mix_o': ln_mix_o[0],
        'w_in_o': w_in_o,
        'w_in_o_tail': jnp.concatenate([wo[:, ZO_OG + 2 * nh:], _pad_cols(wo[:, ZO_OG:ZO_OG + 2 * nh], 2 * LANES)], axis=1),
        'b_i': mlstm_b_i[0], 'b_f': mlstm_b_f[0], 'norm_w': mlstm_norm_w[0],
        'w_out_o': w_out_o[0], 'ln_ffn_o': ln_ffn_o[0],
        'router_w': router_w[0], 'router_b': router_b[0],
        'exp_w1': exp_w1[0], 'exp_w3': exp_w3[0], 'exp_w2': exp_w2[0].reshape(N_EXPERTS * EXPERT_FF, d),
        'final_norm': final_norm,
    }
    bp, tp_, _ = x_prompt.shape
    bs, ts_, _ = x_sample.shape
    g, ns = S5_GROUPS, S5_STATE
    zero = lambda *s: jnp.zeros(s, F32)
    st_p = {'wkv': zero(bp, RWKV_HEADS, RWKV_HEAD, RWKV_HEAD), 'shift': zero(bp, RWKV_IN),
            's5_re': zero(bp, g * ns), 's5_im': zero(bp, g * ns),
            'c': zero(bp, nh, MLSTM_DV, MLSTM_DQK), 'n': zero(bp, nh, MLSTM_DQK), 'm': zero(bp, nh)}
    st_s = {'wkv': state_rwkv_wkv[0], 'shift': state_rwkv_shift[0],
            's5_re': state_s5_re[0].reshape(bs, g * ns), 's5_im': state_s5_im[0].reshape(bs, g * ns),
            'c': state_mlstm_c[0], 'n': state_mlstm_n[0], 'm': state_mlstm_m[0]}
    out_p = _trunk(x_prompt.reshape(bp * tp_, d), bp, tp_, st_p, w)
    out_s = _trunk(x_sample.reshape(bs * ts_, d), bs, ts_, st_s, w)
    y_p = out_p[0].reshape(bp, tp_, d)
    y_s = out_s[0].reshape(bs, ts_, d)
    return (y_p, y_s) + tuple(out_p[1:]) + tuple(out_s[1:])
```

```python
import functools
import math

import jax
import jax.numpy as jnp
from jax import lax
from jax.experimental import pallas as pl
from jax.experimental.pallas import tpu as pltpu

F32 = jnp.float32
BF16 = jnp.bfloat16

D_MODEL = 4096
MIX_A = D_MODEL // 2
MIX_B = D_MODEL - MIX_A
RWKV_HEAD = 64
RWKV_HEADS = MIX_A // RWKV_HEAD
RWKV_DECAY_RANK = 96
RWKV_ICL_RANK = 96
RWKV_GATE_RANK = 256
RWKV_IN = 3 * MIX_A + RWKV_DECAY_RANK + RWKV_ICL_RANK + RWKV_GATE_RANK
RWKV_LNX_EPS = 1e-5 * RWKV_HEAD
S5_GROUP = 16
S5_GROUPS = MIX_B // S5_GROUP
S5_STATE = 64
MLSTM_HEADS = 8
MLSTM_DQK = D_MODEL // 16
MLSTM_DV = D_MODEL // 8
MLSTM_CHUNK = 64
GATE_CAP = 15.0
FFN_DIM = 11008
N_EXPERTS = 8
EXPERT_FF = D_MODEL // 2
RMS_EPS = 1e-6

LANES = 128
SUBLANES = 8
VMEM_LIMIT = 56 * 1024 * 1024

RANK_PAD = LANES
ZE_DW = 3 * MIX_A
ZE_DA = ZE_DW + RANK_PAD
ZE_DG = ZE_DA + RANK_PAD
ZE_U = ZE_DG + RWKV_GATE_RANK
ZE_W = ZE_U + MIX_B
RW_CAT = 3 * LANES + 2 * RANK_PAD + RWKV_GATE_RANK
ZO_OG = 2 * MLSTM_HEADS * MLSTM_DQK + MLSTM_HEADS * MLSTM_DV
ZO_G = ZO_OG + MLSTM_HEADS * MLSTM_DV
ZO_W = ZO_G + 4 * LANES

S5_SLAB_GROUPS = 16
S5_SLABS = S5_GROUPS // S5_SLAB_GROUPS
S5_SLAB_IN = S5_SLAB_GROUPS * S5_GROUP
S5_SLAB_STATE = S5_SLAB_GROUPS * S5_STATE

NN = (((1,), (0,)), ((), ()))
NT = (((1,), (1,)), ((), ()))
TN = (((0,), (0,)), ((), ()))
NEG_BIG = -1e30


def _cparams(sem):
    return pltpu.CompilerParams(dimension_semantics=sem, vmem_limit_bytes=VMEM_LIMIT)


def _pick(n, cands):
    for c in cands:
        if n % c == 0:
            return c
    raise ValueError(f"no tile for {n}")


def _dot(a, b, dims=NN):
    return lax.dot_general(a, b, dims, preferred_element_type=F32)


def _split2(a):
    hi = a.astype(BF16)
    lo = (a - hi.astype(F32)).astype(BF16)
    return hi, lo


def _dot_hp(a, b, dims=NN):
    ah, al = _split2(a)
    bh, bl = _split2(b)
    return _dot(ah, bh, dims) + (_dot(ah, bl, dims) + _dot(al, bh, dims))


def _dot_const(c_bf16, a):
    h0 = a.astype(BF16)
    r1 = a - h0.astype(F32)
    h1 = r1.astype(BF16)
    h2 = (r1 - h1.astype(F32)).astype(BF16)
    return _dot(c_bf16, h0) + (_dot(c_bf16, h1) + _dot(c_bf16, h2))


def _sigmoid(x):
    return 1.0 / (1.0 + jnp.exp(-x))


def _softplus(x):
    return jnp.maximum(x, 0.0) + jnp.log(1.0 + jnp.exp(-jnp.abs(x)))


def _iota(shape, dim):
    return lax.broadcasted_iota(jnp.int32, shape, dim)


def _rmsnorm_body(x_ref, g_ref, o_ref):
    x = x_ref[...]
    ms = jnp.mean(x * x, axis=-1, keepdims=True)
    o_ref[...] = (x * lax.rsqrt(ms + RMS_EPS) * g_ref[...]).astype(o_ref.dtype)


def _rmsnorm(x, g, out_dtype):
    m, d = x.shape
    tm = _pick(m, (256, 128, 64, 32, 16, 8))
    return pl.pallas_call(
        _rmsnorm_body,
        grid=(m // tm,),
        in_specs=[pl.BlockSpec((tm, d), lambda i: (i, 0)), pl.BlockSpec((1, d), lambda i: (0, 0))],
        out_specs=pl.BlockSpec((tm, d), lambda i: (i, 0)),
        out_shape=jax.ShapeDtypeStruct((m, d), out_dtype),
        compiler_params=_cparams(("parallel",)),
        name="rmsnorm",
    )(x, g.reshape(1, d))


def _router_body(x_ref, g_ref, rw_ref, rb_ref, hn_ref, comb_ref):
    x = x_ref[...]
    ms = jnp.mean(x * x, axis=-1, keepdims=True)
    h = x * lax.rsqrt(ms + RMS_EPS) * g_ref[...]
    hn_ref[...] = h.astype(hn_ref.dtype)
    logits = _dot_hp(h, rw_ref[...]) + rb_ref[...]
    lane = _iota(logits.shape, 1)
    logits = jnp.where(lane < N_EXPERTS, logits, NEG_BIG)
    m1 = jnp.max(logits, axis=-1, keepdims=True)
    i1 = jnp.min(jnp.where(logits == m1, lane, LANES), axis=-1, keepdims=True)
    rest = jnp.where(lane == i1, NEG_BIG, logits)
    m2 = jnp.max(rest, axis=-1, keepdims=True)
    i2 = jnp.min(jnp.where(rest == m2, lane, LANES), axis=-1, keepdims=True)
    e = jnp.exp(m2 - m1)
    g1 = 1.0 / (1.0 + e)
    g2 = e / (1.0 + e)
    comb_ref[...] = jnp.where(lane == i1, g1, 0.0) + jnp.where(lane == i2, g2, 0.0)


def _rmsnorm_router(x, g, router_w, router_b):
    m, d = x.shape
    tm = _pick(m, (256, 128, 64, 32, 16, 8))
    rw = jnp.pad(router_w, ((0, 0), (0, LANES - N_EXPERTS)))
    rb = jnp.pad(router_b.reshape(1, N_EXPERTS), ((0, 0), (0, LANES - N_EXPERTS)))
    return pl.pallas_call(
        _router_body,
        grid=(m // tm,),
        in_specs=[pl.BlockSpec((tm, d), lambda i: (i, 0)), pl.BlockSpec((1, d), lambda i: (0, 0)),
                  pl.BlockSpec((d, LANES), lambda i: (0, 0)), pl.BlockSpec((1, LANES), lambda i: (0, 0))],
        out_specs=[pl.BlockSpec((tm, d), lambda i: (i, 0)), pl.BlockSpec((tm, LANES), lambda i: (i, 0))],
        out_shape=[jax.ShapeDtypeStruct((m, d), BF16), jax.ShapeDtypeStruct((m, LANES), F32)],
        compiler_params=_cparams(("parallel",)),
        name="rmsnorm_router",
    )(x, g.reshape(1, d), rw, rb)


def _mm_body(a_ref, w_ref, o_ref):
    o_ref[...] = _dot(a_ref[...], w_ref[...].astype(BF16)).astype(o_ref.dtype)


def _mm(a, w, n_out=None):
    m, k = a.shape
    n = w.shape[1] if n_out is None else n_out
    tm = _pick(m, (1024, 512, 256, 128, 64, 32, 16, 8))
    tn = _pick(n, (512, 256, 128))
    return pl.pallas_call(
        _mm_body,
        grid=(m // tm, n // tn),
        in_specs=[pl.BlockSpec((tm, k), lambda i, j: (i, 0)), pl.BlockSpec((k, tn), lambda i, j: (0, j))],
        out_specs=pl.BlockSpec((tm, tn), lambda i, j: (i, j)),
        out_shape=jax.ShapeDtypeStruct((m, n), F32),
        compiler_params=_cparams(("parallel", "arbitrary")),
        name="proj_in",
    )(a, w)


def _mm_nt_body(a_ref, wt_ref, *rest):
    o_ref = rest[-1]
    o_ref[...] = _dot(a_ref[...], wt_ref[...].astype(BF16), NT)


def _mm_nt_into(a, wt, n_rows, col0, n_total, prev=None):
    m, k = a.shape
    tm = _pick(m, (1024, 512, 256, 128, 64, 32, 16, 8))
    tn = 4 * LANES
    assert n_rows % tn == 0 and col0 % tn == 0
    in_specs = [pl.BlockSpec((tm, k), lambda i, j: (i, 0)), pl.BlockSpec((tn, k), lambda i, j: (j, 0))]
    args = [a, wt]
    aliases = {}
    if prev is not None:
        in_specs.append(pl.BlockSpec(memory_space=pl.ANY))
        args.append(prev)
        aliases = {2: 0}
    return pl.pallas_call(
        _mm_nt_body,
        grid=(m // tm, n_rows // tn),
        in_specs=in_specs,
        out_specs=pl.BlockSpec((tm, tn), lambda i, j: (i, col0 // tn + j)),
        out_shape=jax.ShapeDtypeStruct((m, n_total), F32),
        input_output_aliases=aliases,
        compiler_params=_cparams(("parallel", "arbitrary")),
        name="proj_in",
    )(*args)


def _mm_split(a, w_t, n_main, w_tail_t):
    n_total = n_main + w_tail_t.shape[0]
    z = _mm_nt_into(a, w_t, n_main, 0, n_total)
    return _mm_nt_into(a, w_tail_t, w_tail_t.shape[0], n_main, n_total, prev=z)


def _mm2_res_body(a0_ref, a1_ref, w0_ref, w1_ref, r_ref, o_ref):
    acc = _dot(a0_ref[...], w0_ref[...].astype(BF16)) + _dot(a1_ref[...], w1_ref[...].astype(BF16))
    o_ref[...] = r_ref[...] + acc


def _mm2_res(a0, a1, w, res):
    m, k0 = a0.shape
    k1 = a1.shape[1]
    assert k0 == k1
    n = w.shape[1]
    tm = _pick(m, (1024, 512, 256, 128, 64, 32, 16, 8))
    tn = _pick(n, (512, 256, 128))
    return pl.pallas_call(
        _mm2_res_body,
        grid=(m // tm, n // tn),
        in_specs=[pl.BlockSpec((tm, k0), lambda i, j: (i, 0)), pl.BlockSpec((tm, k1), lambda i, j: (i, 0)),
                  pl.BlockSpec((k0, tn), lambda i, j: (0, j)), pl.BlockSpec((k1, tn), lambda i, j: (1, j)),
                  pl.BlockSpec((tm, tn), lambda i, j: (i, j))],
        out_specs=pl.BlockSpec((tm, tn), lambda i, j: (i, j)),
        out_shape=jax.ShapeDtypeStruct((m, n), F32),
        compiler_params=_cparams(("parallel", "arbitrary")),
        name="proj_out",
    )(a0, a1, w, w, res)


def _mm_res_body(a_ref, w_ref, r_ref, o_ref):
    o_ref[...] = r_ref[...] + _dot(a_ref[...], w_ref[...].astype(BF16))


def _mm_res(a, w, res):
    m, k = a.shape
    n = w.shape[1]
    tm = _pick(m, (1024, 512, 256, 128, 64, 32, 16, 8))
    tn = _pick(n, (512, 256, 128))
    return pl.pallas_call(
        _mm_res_body,
        grid=(m // tm, n // tn),
        in_specs=[pl.BlockSpec((tm, k), lambda i, j: (i, 0)), pl.BlockSpec((k, tn), lambda i, j: (0, j)),
                  pl.BlockSpec((tm, tn), lambda i, j: (i, j))],
        out_specs=pl.BlockSpec((tm, tn), lambda i, j: (i, j)),
        out_shape=jax.ShapeDtypeStruct((m, n), F32),
        compiler_params=_cparams(("parallel", "arbitrary")),
        name="proj_out1",
    )(a, w, res)


def _glu_up_body(a_ref, w1_ref, w3_ref, o_ref):
    a = a_ref[...]
    h1 = _dot(a, w1_ref[...].astype(BF16))
    h3 = _dot(a, w3_ref[...].astype(BF16))
    o_ref[...] = (h1 * _sigmoid(h1) * h3).astype(o_ref.dtype)


def _glu_up(a, w1, w3):
    m, k = a.shape
    n = w1.shape[1]
    tm = _pick(m, (1024, 512, 256, 128, 64, 32, 16, 8))
    tn = _pick(n, (256, 128))
    return pl.pallas_call(
        _glu_up_body,
        grid=(m // tm, n // tn),
        in_specs=[pl.BlockSpec((tm, k), lambda i, j: (i, 0)), pl.BlockSpec((k, tn), lambda i, j: (0, j)),
                  pl.BlockSpec((k, tn), lambda i, j: (0, j))],
        out_specs=pl.BlockSpec((tm, tn), lambda i, j: (i, j)),
        out_shape=jax.ShapeDtypeStruct((m, n), BF16),
        compiler_params=_cparams(("parallel", "arbitrary")),
        name="ffn_up",
    )(a, w1, w3)


def _moe_up_body(a_ref, comb_ref, w1_ref, w3_ref, o_ref):
    e = pl.program_id(1)
    a = a_ref[...]
    h1 = _dot(a, w1_ref[...].astype(BF16))
    h3 = _dot(a, w3_ref[...].astype(BF16))
    comb = comb_ref[...]
    gate = jnp.sum(jnp.where(_iota(comb.shape, 1) == e, comb, 0.0), axis=-1, keepdims=True)
    o_ref[...] = (h1 * _sigmoid(h1) * h3 * gate).astype(o_ref.dtype)


def _moe_up(a, comb, w1, w3):
    m, k = a.shape
    ne, _, f = w1.shape
    tm = _pick(m, (1024, 512, 256, 128, 64, 32, 16, 8))
    tn = _pick(f, (256, 128))
    nf = f // tn
    return pl.pallas_call(
        _moe_up_body,
        grid=(m // tm, ne, nf),
        in_specs=[pl.BlockSpec((tm, k), lambda i, e, j: (i, 0)), pl.BlockSpec((tm, LANES), lambda i, e, j: (i, 0)),
                  pl.BlockSpec((None, k, tn), lambda i, e, j: (e, 0, j)),
                  pl.BlockSpec((None, k, tn), lambda i, e, j: (e, 0, j))],
        out_specs=pl.BlockSpec((tm, tn), lambda i, e, j: (i, e * nf + j)),
        out_shape=jax.ShapeDtypeStruct((m, ne * f), BF16),
        compiler_params=_cparams(("parallel", "arbitrary", "arbitrary")),
        name="moe_up",
    )(a, comb, w1, w3)


def _mm_acc_body(a_ref, w_ref, r_ref, o_ref, *, k_total, tk, rc):
    kk = pl.program_id(2)
    ragged = k_total % tk != 0
    valid = k_total - kk * tk
    w = w_ref[...]
    if ragged:
        w = jnp.where(_iota(w.shape, 0) < valid, w, 0.0)
    w = w.astype(BF16)

    @pl.when(kk == 0)
    def _():
        o_ref[...] = r_ref[...]

    a = a_ref[...]
    if ragged:
        a = jnp.where(_iota(a.shape, 1) < valid, a, jnp.zeros_like(a))
    for c0 in range(0, w.shape[1], rc):
        o_ref[:, c0:c0 + rc] += _dot(a, w[:, c0:c0 + rc])


def _mm_acc(a, w, res):
    m, k = a.shape
    n = w.shape[1]
    tm = _pick(m, (2048, 1024, 512, 256, 128, 64, 32, 16, 8))
    tn = _pick(n, (1024, 512, 256, 128))
    tk = 512
    nk = pl.cdiv(k, tk)
    return pl.pallas_call(
        functools.partial(_mm_acc_body, k_total=k, tk=tk, rc=min(tn, 2 * LANES)),
        grid=(m // tm, n // tn, nk),
        in_specs=[pl.BlockSpec((tm, tk), lambda i, j, kk: (i, kk)), pl.BlockSpec((tk, tn), lambda i, j, kk: (kk, j)),
                  pl.BlockSpec((tm, tn), lambda i, j, kk: (i, j))],
        out_specs=pl.BlockSpec((tm, tn), lambda i, j, kk: (i, j)),
        out_shape=jax.ShapeDtypeStruct((m, n), F32),
        compiler_params=_cparams(("parallel", "parallel", "arbitrary")),
        name="proj_down",
    )(a, w, res)


def _rwkv_body(zr_ref, zk_ref, zv_ref, zdw_ref, zda_ref, zdg_ref, mu_ref, sh_ref, s0_ref,
               w0_ref, wup_ref, a0_ref, aup_ref, gup_ref, kk_ref, ka_ref, rk_ref, lw_ref, lb_ref,
               y_ref, sT_ref, s_scr, carry_scr, *, bb, rt, chunk, valid, unroll):
    n = pl.program_id(2)
    n_last = pl.num_programs(2) - 1
    L = chunk
    L2 = 2 * L
    H = RWKV_HEAD
    nchunk = rt // L
    nsq = max(1, int(math.ceil(math.log2(L))))

    lane = _iota((1, LANES), 1)
    m0 = (lane < H).astype(F32)
    m1 = 1.0 - m0
    r2 = _iota((L2, L2), 0)
    c2 = _iota((L2, L2), 1)
    rh = jnp.where(r2 >= L, 1, 0)
    ch = jnp.where(c2 >= L, 1, 0)
    same = jnp.where(rh == ch, 1.0, 0.0)
    tdiff = (r2 - L * rh) - (c2 - L * ch)
    strict = same * jnp.where(tdiff > 0, 1.0, 0.0)
    incl = same * jnp.where(tdiff >= 0, 1.0, 0.0)
    rl = _iota((L, L), 0)
    cl = _iota((L, L), 1)
    tril = jnp.where(rl >= cl, 1.0, 0.0).astype(BF16)
    ri = _iota((LANES, LANES), 0)
    ci = _iota((LANES, LANES), 1)
    bd = jnp.where((ri >= H) == (ci >= H), 1.0, 0.0)
    bones = bd.astype(BF16)
    row = _iota((L, 1), 0)

    @pl.when(n == 0)
    def _():
        s_scr[...] = jnp.zeros_like(s_scr)
        carry_scr[...] = jnp.zeros_like(carry_scr)

    mu = mu_ref[...]
    w0 = w0_ref[...]
    a0 = a0_ref[...]
    kkw = kk_ref[...]
    kaw = ka_ref[...]
    rkw = rk_ref[...]
    lnw = lw_ref[...]
    lnb = lb_ref[...]

    def block_sum(x):
        return _dot_const_rhs(x, bones)

    def load(seq, r0, is_start):
        zc = jnp.concatenate([ref[seq, pl.ds(r0, L), :] for ref in (zr_ref, zk_ref, zv_ref, zdw_ref, zda_ref, zdg_ref)],
                             axis=1)
        first = jnp.where(is_start, sh_ref[pl.ds(seq, 1), :], carry_scr[seq, 0:1, :])
        s0 = s0_ref[seq, 0]
        s1 = s0_ref[seq, 1]
        zz = jnp.zeros((H, H), F32)
        s_init = jnp.concatenate([jnp.concatenate([s0, zz], axis=1), jnp.concatenate([zz, s1], axis=1)], axis=0)
        st = jnp.where(is_start, s_init, s_scr[seq])
        return zc, first, st

    def recur(seqs):
        pre = []
        for r, k, v, ka, kb, lw, cum, st in seqs:
            ce = jnp.exp(cum)
            cinv = jnp.exp(-cum)
            at = jnp.exp(cum - lw) * ka
            rt_ = ce * r
            bt = kb * cinv
            kt = k * cinv
            c_last = ce[L - 1:L, :]
            lhs = jnp.concatenate([at * m0, at * m1, rt_ * m0, rt_ * m1], axis=0)
            rhs = jnp.concatenate([bt * m0, bt * m1, kt * m0, kt * m1, st], axis=0)
            bk = jnp.concatenate([bt * c_last, kt * c_last], axis=0)
            pre.append((lhs, rhs, bk, c_last))
        gms = [_dot_x(p_[0], p_[1], NT) for p_ in pre]
        wvs = [_dot_x(jnp.concatenate([gm[0:L2, L2:2 * L2] * strict, gm[L2:, L2:2 * L2] * incl], axis=0),
                      jnp.concatenate([s[2], s[2]], axis=0)) for gm, s in zip(gms, seqs)]
        xs = [gm[0:L2, 2 * L2:] + wv[0:L2] for gm, wv in zip(gms, wvs)]
        ps = [gm[0:L2, 0:L2] * strict for gm in gms]
        for q in range(nsq):
            if q + 1 < nsq:
                xps = [_dot_x(p, jnp.concatenate([x, p], axis=1)) for x, p in zip(xs, ps)]
                xs = [x + xp[:, 0:LANES] for x, xp in zip(xs, xps)]
                ps = [xp[:, LANES:] for xp in xps]
            else:
                xs = [x + _dot_x(p, x) for x, p in zip(xs, ps)]
        us = [x[0:L, :] * m0 + x[L:L2, :] * m1 for x in xs]
        rus = [_dot_x(gm[L2:, 0:L2] * incl, jnp.concatenate([u, u], axis=0)) for gm, u in zip(gms, us)]
        sus = [_dot_x(jnp.concatenate([u, s[2]], axis=0), p_[2], TN) for u, s, p_ in zip(us, seqs, pre)]
        out = []
        for gm, wv, ru, su, s, p_ in zip(gms, wvs, rus, sus, seqs, pre):
            y2 = gm[L2:, 2 * L2:] + wv[L2:] + ru
            out.append((y2[0:L, :] * m0 + y2[L:L2, :] * m1, s[7] * p_[3] + bd * su))
        return out

    def compute_all(loaded):
        nu = len(loaded)
        stack = lambda xs: xs[0] if nu == 1 else jnp.concatenate(xs, axis=0)
        part = lambda x, u: x[u * L:(u + 1) * L]
        zc = stack([x[0] for x in loaded])
        zprev = stack([jnp.where(row == 0, x[1], pltpu.roll(x[0], 1, 0)) for x in loaded])
        zs = zc + mu * (zprev - zc)
        r = zs[:, 0:LANES]
        k = zs[:, LANES:2 * LANES]
        v = zs[:, 2 * LANES:3 * LANES]
        dw = zs[:, 3 * LANES:4 * LANES]
        da = zs[:, 4 * LANES:5 * LANES]
        dg = zs[:, 5 * LANES:]

        w_log = -_softplus(-(w0 + _dot_hp(jnp.tanh(dw), wup_ref[...]))) - 0.5
        lw = -jnp.exp(w_log)
        a = _sigmoid(a0 + _dot_hp(da, aup_ref[...]))
        g = _dot_hp(_sigmoid(dg), gup_ref[...])
        kk = k * kkw
        k = k * (1.0 + (a - 1.0) * kaw)
        rows = nu * L
        sums = block_sum(jnp.concatenate([kk * kk, r * k * rkw], axis=0))
        kk = kk / jnp.maximum(jnp.sqrt(sums[0:rows]), 1e-12)
        bonus = sums[rows:] * v
        ka = -kk
        kb = kk * a
        if valid < L:
            ok = stack([row < valid] * nu)
            lw = jnp.where(ok, lw, 0.0)
            r = jnp.where(ok, r, 0.0)
            k = jnp.where(ok, k, 0.0)
            v = jnp.where(ok, v, 0.0)
            ka = jnp.where(ok, ka, 0.0)
            kb = jnp.where(ok, kb, 0.0)

        lw_wide = lw if nu == 1 else jnp.concatenate([part(lw, u) for u in range(nu)], axis=1)
        cum_wide = _dot_const(tril, lw_wide)
        res = recur([(part(r, u), part(k, u), part(v, u), part(ka, u), part(kb, u), part(lw, u),
                      cum_wide[:, u * LANES:(u + 1) * LANES], loaded[u][2]) for u in range(nu)])
        sts = [x[1] for x in res]
        y = stack([x[0] for x in res])
        mean = block_sum(y) * (1.0 / H)
        yc = y - mean
        var = block_sum(yc * yc) * (1.0 / H)
        yn = yc * lax.rsqrt(var + RWKV_LNX_EPS) * lnw + lnb
        out = (yn + bonus) * g
        return [(part(out, u).astype(y_ref.dtype), sts[u], loaded[u][0][valid - 1:valid, :]) for u in range(nu)]

    def store(seq, r0, is_end, y, st, last_row):
        y_ref[seq, pl.ds(r0, L), :] = y
        s_scr[seq] = st
        carry_scr[seq, 0:1, :] = last_row

        @pl.when(is_end)
        def _():
            sT_ref[seq, 0] = st[0:H, 0:H]
            sT_ref[seq, 1] = st[H:2 * H, H:2 * H]

    def step(it, carry):
        sg = it // nchunk
        ci_ = it - sg * nchunk
        r0 = pl.multiple_of(ci_ * L, L)
        is_start = jnp.logical_and(n == 0, ci_ == 0)
        is_end = jnp.logical_and(n == n_last, ci_ == nchunk - 1)
        seqs = [sg * unroll + u for u in range(unroll)]
        loaded = [load(s, r0, is_start) for s in seqs]
        done = compute_all(loaded)
        for s, d in zip(seqs, done):
            store(s, r0, is_end, *d)
        return carry

    lax.fori_loop(0, (bb // unroll) * nchunk, step, 0)


def _dot_x(a, b, dims=NN):
    return _dot(a.astype(BF16), b.astype(BF16), dims)


def _dot_const_rhs(a, c_bf16):
    h0 = a.astype(BF16)
    r1 = a - h0.astype(F32)
    h1 = r1.astype(BF16)
    h2 = (r1 - h1.astype(F32)).astype(BF16)
    return _dot(h0, c_bf16) + (_dot(h1, c_bf16) + _dot(h2, c_bf16))


def _rwkv(z, shift_p, wkv0, mu_p, w0, wup_p, a0, aup_p, g_up, k_k, k_a, r_k, lnx_w, lnx_b,
          *, n_seq, seq_rows, chunk, valid):
    unroll = _pick(n_seq, (4, 2, 1))
    if seq_rows > chunk:
        rt = _pick(seq_rows, (256, 128, 64))
        bb = unroll
    else:
        rt = seq_rows
        bb = _pick(n_seq, (32, 16, 8, 4, 2, 1))
    assert rt % chunk == 0 and bb % unroll == 0 and (bb == n_seq or bb % SUBLANES == 0)
    sblk = lambda c, g, n: (g, c, 0, 0)
    npair = RWKV_HEADS // 2
    nrb = LANES

    def cat(a):
        rows = a.shape[0]
        rkv = a[:, :3 * MIX_A].reshape(rows, 3, npair, LANES).transpose(2, 0, 1, 3).reshape(npair, rows, 3 * LANES)
        tail = jnp.broadcast_to(a[None, :, 3 * MIX_A:], (npair, rows, ZE_U - 3 * MIX_A))
        return jnp.concatenate([rkv, tail], axis=2)

    mu_cat = cat(mu_p.reshape(1, ZE_U))
    sh_cat = cat(shift_p)
    row2 = lambda a: a.reshape(1, MIX_A)
    vec = pl.BlockSpec((1, LANES), lambda c, g, n: (0, c))
    kern = functools.partial(_rwkv_body, bb=bb, rt=rt, chunk=chunk, valid=valid, unroll=unroll)
    y, s_t = pl.pallas_call(
        kern,
        grid=(npair, n_seq // bb, seq_rows // rt),
        in_specs=[
            pl.BlockSpec((bb, rt, nrb), lambda c, g, n: (g, n, c)),
            pl.BlockSpec((bb, rt, nrb), lambda c, g, n: (g, n, npair + c)),
            pl.BlockSpec((bb, rt, nrb), lambda c, g, n: (g, n, 2 * npair + c)),
            pl.BlockSpec((bb, rt, RANK_PAD), lambda c, g, n: (g, n, ZE_DW // RANK_PAD)),
            pl.BlockSpec((bb, rt, RANK_PAD), lambda c, g, n: (g, n, ZE_DA // RANK_PAD)),
            pl.BlockSpec((bb, rt, RWKV_GATE_RANK), lambda c, g, n: (g, n, ZE_DG // RWKV_GATE_RANK)),
            pl.BlockSpec((None, 1, RW_CAT), lambda c, g, n: (c, 0, 0)),
            pl.BlockSpec((None, bb, RW_CAT), lambda c, g, n: (c, g, 0)),
            pl.BlockSpec((bb, 2, RWKV_HEAD, RWKV_HEAD), sblk),
            vec,
            pl.BlockSpec((RANK_PAD, LANES), lambda c, g, n: (0, c)),
            vec,
            pl.BlockSpec((RANK_PAD, LANES), lambda c, g, n: (0, c)),
            pl.BlockSpec((RWKV_GATE_RANK, LANES), lambda c, g, n: (0, c)),
            vec, vec, vec, vec, vec,
        ],
        out_specs=[pl.BlockSpec((bb, rt, LANES), lambda c, g, n: (g, n, c)),
                   pl.BlockSpec((bb, 2, RWKV_HEAD, RWKV_HEAD), sblk)],
        out_shape=[jax.ShapeDtypeStruct((n_seq, seq_rows, MIX_A), BF16),
                   jax.ShapeDtypeStruct((n_seq, RWKV_HEADS, RWKV_HEAD, RWKV_HEAD), F32)],
        scratch_shapes=[pltpu.VMEM((bb, LANES, LANES), F32), pltpu.VMEM((bb, SUBLANES, RW_CAT), F32)],
        compiler_params=_cparams(("parallel", "parallel", "arbitrary")),
        name="rwkv7",
    )(z, z, z, z, z, z, mu_cat, sh_cat, wkv0,
      row2(w0), wup_p, row2(a0), aup_p, g_up, row2(k_k), row2(k_a), row2(r_k), row2(lnx_w), row2(lnx_b))
    return y, s_t


def _rwkv_cat(a):
    npair = RWKV_HEADS // 2
    rows = a.shape[0]
    rkv = a[:, :3 * MIX_A].reshape(rows, 3, npair, LANES).transpose(2, 0, 1, 3).reshape(npair, rows, 3 * LANES)
    tail = jnp.broadcast_to(a[None, :, 3 * MIX_A:], (npair, rows, ZE_U - 3 * MIX_A))
    return jnp.concatenate([rkv, tail], axis=2)


def _rwkv_step_body(zr_ref, zk_ref, zv_ref, zdw_ref, zda_ref, zdg_ref, mu_ref, sh_ref, s0_ref,
                    w0_ref, wup_ref, a0_ref, aup_ref, gup_ref, kk_ref, ka_ref, rk_ref, lw_ref, lb_ref,
                    y_ref, sT_ref, op_scr, yt_scr, *, steps, nb):
    H = RWKV_HEAD
    rows = steps * nb
    ri = _iota((LANES, LANES), 0)
    ci = _iota((LANES, LANES), 1)
    bones = jnp.where((ri >= H) == (ci >= H), 1.0, 0.0).astype(BF16)
    block_sum = lambda x: _dot_const_rhs(x, bones)

    zc = jnp.concatenate([ref[...].reshape(rows, ref.shape[2]) for ref in (zr_ref, zk_ref, zv_ref, zdw_ref, zda_ref, zdg_ref)],
                         axis=1)
    zprev = jnp.concatenate([sh_ref[...], zc[0:rows - nb]], axis=0) if steps > 1 else sh_ref[...]
    zs = zc + mu_ref[...] * (zprev - zc)
    r = zs[:, 0:LANES]
    k = zs[:, LANES:2 * LANES]
    v = zs[:, 2 * LANES:3 * LANES]
    dw = zs[:, 3 * LANES:4 * LANES]
    da = zs[:, 4 * LANES:5 * LANES]
    dg = zs[:, 5 * LANES:]
    w_log = -_softplus(-(w0_ref[...] + _dot_hp(jnp.tanh(dw), wup_ref[...]))) - 0.5
    decay = jnp.exp(-jnp.exp(w_log))
    a = _sigmoid(a0_ref[...] + _dot_hp(da, aup_ref[...]))
    g = _dot_hp(_sigmoid(dg), gup_ref[...])
    kk = k * kk_ref[...]
    k = k * (1.0 + (a - 1.0) * ka_ref[...])
    sums = block_sum(jnp.concatenate([kk * kk, r * k * rk_ref[...]], axis=0))
    kk = kk / jnp.maximum(jnp.sqrt(sums[0:rows]), 1e-12)
    bonus = sums[rows:] * v

    for qi, x in enumerate((decay, -kk, kk * a, k, r, v)):
        for t in range(steps):
            op_scr[qi, t] = x[t * nb:(t + 1) * nb, :].T

    for t in range(steps):
        src = s0_ref if t == 0 else sT_ref
        for hh in range(2):
            lo = hh * H
            w_t = op_scr[0, t, lo:lo + H, :]
            ka_t = op_scr[1, t, lo:lo + H, :]
            kb_t = op_scr[2, t, lo:lo + H, :]
            k_t = op_scr[3, t, lo:lo + H, :]
            r_t = op_scr[4, t, lo:lo + H, :]

            def value_row(i, carry, src=src, hh=hh, lo=lo, t=t, w_t=w_t, ka_t=ka_t, kb_t=kb_t, k_t=k_t, r_t=r_t):
                s_i = src[hh, i]
                sa = jnp.sum(s_i * ka_t, axis=0, keepdims=True)
                v_i = op_scr[5, t, pl.ds(lo + i, 1), :]
                s_n = s_i * w_t + sa * kb_t + v_i * k_t
                sT_ref[hh, i] = s_n
                yt_scr[t, pl.ds(lo + i, 1), :] = jnp.sum(s_n * r_t, axis=0, keepdims=True)
                return carry

            lax.fori_loop(0, H, value_row, 0, unroll=4)

    y = jnp.concatenate([yt_scr[t].T for t in range(steps)], axis=0)
    mean = block_sum(y) * (1.0 / H)
    yc = y - mean
    var = block_sum(yc * yc) * (1.0 / H)
    yn = yc * lax.rsqrt(var + RWKV_LNX_EPS) * lw_ref[...] + lb_ref[...]
    y_ref[...] = ((yn + bonus) * g).reshape(steps, nb, LANES).astype(y_ref.dtype)


def _rwkv_step(z_tb, shift_p, wkv0_t, mu_p, w0, wup_p, a0, aup_p, g_up, k_k, k_a, r_k, lnx_w, lnx_b):
    steps, nb, _ = z_tb.shape
    assert nb == LANES
    npair = RWKV_HEADS // 2
    row2 = lambda a: a.reshape(1, MIX_A)
    vec = pl.BlockSpec((1, LANES), lambda c: (0, c))
    zspec = lambda width, idx: pl.BlockSpec((steps, nb, width), lambda c: (0, 0, idx(c)))
    sspec = pl.BlockSpec((2, RWKV_HEAD, RWKV_HEAD, nb), lambda c: (c, 0, 0, 0))
    kern = functools.partial(_rwkv_step_body, steps=steps, nb=nb)
    return pl.pallas_call(
        kern,
        grid=(npair,),
        in_specs=[
            zspec(LANES, lambda c: c), zspec(LANES, lambda c: npair + c), zspec(LANES, lambda c: 2 * npair + c),
            zspec(RANK_PAD, lambda c: ZE_DW // RANK_PAD), zspec(RANK_PAD, lambda c: ZE_DA // RANK_PAD),
            zspec(RWKV_GATE_RANK, lambda c: ZE_DG // RWKV_GATE_RANK),
            pl.BlockSpec((None, 1, RW_CAT), lambda c: (c, 0, 0)),
            pl.BlockSpec((None, nb, RW_CAT), lambda c: (c, 0, 0)),
            sspec,
            vec,
            pl.BlockSpec((RANK_PAD, LANES), lambda c: (0, c)),
            vec,
            pl.BlockSpec((RANK_PAD, LANES), lambda c: (0, c)),
            pl.BlockSpec((RWKV_GATE_RANK, LANES), lambda c: (0, c)),
            vec, vec, vec, vec, vec,
        ],
        out_specs=[pl.BlockSpec((steps, nb, LANES), lambda c: (0, 0, c)), sspec],
        out_shape=[jax.ShapeDtypeStruct((steps, nb, MIX_A), BF16),
                   jax.ShapeDtypeStruct((RWKV_HEADS, RWKV_HEAD, RWKV_HEAD, nb), F32)],
        scratch_shapes=[pltpu.VMEM((6, steps, LANES, nb), F32), pltpu.VMEM((steps, LANES, nb), F32)],
        compiler_params=_cparams(("parallel",)),
        name="rwkv7_step",
    )(z_tb, z_tb, z_tb, z_tb, z_tb, z_tb, _rwkv_cat(mu_p.reshape(1, ZE_U)), _rwkv_cat(shift_p), wkv0_t,
      row2(w0), wup_p, row2(a0), aup_p, g_up, row2(k_k), row2(k_a), row2(r_k), row2(lnx_w), row2(lnx_b))


def _s5_prep_body(are_ref, aim_ref, ls_ref, abre_ref, abim_ref, zr_ref, zi_ref):
    lam_re = are_ref[...]
    lam_im = aim_ref[...]
    dt = jnp.exp(ls_ref[...])
    mag = jnp.exp(lam_re * dt)
    ab_re = mag * jnp.cos(lam_im * dt)
    ab_im = mag * jnp.sin(lam_im * dt)
    inv = 1.0 / (lam_re * lam_re + lam_im * lam_im)
    abre_ref[...] = ab_re
    abim_ref[...] = ab_im
    zr_ref[...] = ((ab_re - 1.0) * lam_re + ab_im * lam_im) * inv
    zi_ref[...] = (ab_im * lam_re - (ab_re - 1.0) * lam_im) * inv


def _s5_prep(a_re, a_im, log_step):
    g, n = a_re.shape
    sd = jax.ShapeDtypeStruct((g, n), F32)
    return pl.pallas_call(_s5_prep_body, out_shape=[sd, sd, sd, sd], name="s5_discretise")(
        a_re, a_im, jnp.broadcast_to(log_step.reshape(g, 1), (g, n)))


def _gelu_tanh(x):
    return 0.5 * x * (1.0 + jnp.tanh(math.sqrt(2.0 / math.pi) * (x + 0.044715 * (x * x * x))))


def _s5_body(u_ref, wbr_ref, wbi_ref, wcr_ref, wci_ref, abr_ref, abi_ref, d_ref, h0r_ref, h0i_ref,
             y_ref, hTr_ref, hTi_ref, hr_scr, hi_scr, pr_scr, pi_scr, cr_scr, ci_scr, u_st, y_st, *, rt, seq_rows):
    n = pl.program_id(1)
    long_seq = seq_rows >= rt
    grp = SUBLANES if long_seq else rt // seq_rows
    steps = rt // grp
    ar1 = abr_ref[...]
    ai1 = abi_ref[...]
    ar = jnp.broadcast_to(ar1, (grp, ar1.shape[1]))
    ai = jnp.broadcast_to(ai1, (grp, ai1.shape[1]))
    nhalf = u_ref.shape[1] // LANES
    for hf in range(nhalf):
        u_st[hf] = u_ref[:, hf * LANES:(hf + 1) * LANES]
    u = jnp.concatenate(
        [jnp.concatenate([u_st[hf, pl.ds(j, grp, stride=steps), :] for hf in range(nhalf)], axis=1) for j in range(steps)],
        axis=0)
    ub = u.astype(BF16)
    hr_scr[...] = _dot(ub, wbr_ref[...].astype(BF16))
    hi_scr[...] = _dot(ub, wbi_ref[...].astype(BF16))

    def rows(j):
        return pl.ds(j * grp, grp) if isinstance(j, int) else pl.ds(pl.multiple_of(j * grp, grp), grp)

    def scan_step(j, c):
        hr, hi = c
        nr = ar * hr - ai * hi + hr_scr[rows(j), :]
        ni = ar * hi + ai * hr + hi_scr[rows(j), :]
        hr_scr[rows(j), :] = nr
        hi_scr[rows(j), :] = ni
        return nr, ni

    if long_seq:
        @pl.when(n == 0)
        def _():
            def pw_step(j, c):
                qr, qi = c
                pr_scr[pl.ds(j, 1), :] = qr
                pi_scr[pl.ds(j, 1), :] = qi
                return qr * ar1 - qi * ai1, qr * ai1 + qi * ar1
            lax.fori_loop(0, steps, pw_step, (ar1, ai1))
            cr_scr[...] = jnp.zeros_like(cr_scr)
            ci_scr[...] = jnp.zeros_like(ci_scr)

        b = (n * rt) // seq_rows
        is_start = ((n * rt) % seq_rows) == 0
        h_in_r = jnp.where(is_start, h0r_ref[pl.ds(b, 1), :], cr_scr[0:1, :])
        h_in_i = jnp.where(is_start, h0i_ref[pl.ds(b, 1), :], ci_scr[0:1, :])
        zero = jnp.zeros((grp, hr_scr.shape[1]), F32)
        er, ei = lax.fori_loop(0, steps, scan_step, (zero, zero), unroll=4)
        pS_r = pr_scr[steps - 1:steps, :]
        pS_i = pi_scr[steps - 1:steps, :]
        cr_rows, ci_rows = [h_in_r], [h_in_i]
        for s in range(grp):
            pr_, pi_ = cr_rows[-1], ci_rows[-1]
            cr_rows.append(er[s:s + 1, :] + pS_r * pr_ - pS_i * pi_)
            ci_rows.append(ei[s:s + 1, :] + pS_r * pi_ + pS_i * pr_)
        c_r = jnp.concatenate(cr_rows[:grp], axis=0)
        c_i = jnp.concatenate(ci_rows[:grp], axis=0)
        cr_scr[0:1, :] = cr_rows[grp]
        ci_scr[0:1, :] = ci_rows[grp]
        hTr_ref[pl.ds(b, 1), :] = cr_rows[grp]
        hTi_ref[pl.ds(b, 1), :] = ci_rows[grp]

        def fix_step(j, carry):
            qr = pr_scr[pl.ds(j, 1), :]
            qi = pi_scr[pl.ds(j, 1), :]
            hr_scr[rows(j), :] = hr_scr[rows(j), :] + (qr * c_r - qi * c_i)
            hi_scr[rows(j), :] = hi_scr[rows(j), :] + (qr * c_i + qi * c_r)
            return carry

        lax.fori_loop(0, steps, fix_step, 0, unroll=4)
    else:
        hr, hi = h0r_ref[...], h0i_ref[...]
        for t in range(steps):
            hr, hi = scan_step(t, (hr, hi))
        hTr_ref[...] = hr
        hTi_ref[...] = hi

    y = _dot(hr_scr[...].astype(BF16), wcr_ref[...].astype(BF16)) + _dot(hi_scr[...].astype(BF16), wci_ref[...].astype(BF16))
    y = _gelu_tanh(y + d_ref[...] * u)
    for hf in range(nhalf):
        y_st[hf] = y[:, hf * LANES:(hf + 1) * LANES]
    for hf in range(nhalf):
        for s in range(grp):
            y_ref[s * steps:(s + 1) * steps, hf * LANES:(hf + 1) * LANES] = y_st[hf, pl.ds(s, steps, stride=grp), :]


def _s5_scan(z, col0, wbr, wbi, wcr, wci, ab_re, ab_im, d_skip, h0_re, h0_im, *, n_seq, seq_rows):
    m = z.shape[0]
    st = S5_SLAB_STATE
    if seq_rows >= 8 * SUBLANES:
        rt = _pick(seq_rows, (512, 256, 128, 64))
        hspec = pl.BlockSpec((n_seq, st), lambda s, n: (0, s))
        pw_rows = rt // SUBLANES
    else:
        rt = seq_rows * _pick(n_seq, (128, 64, 32, 16, 8))
        hspec = pl.BlockSpec((rt // seq_rows, st), lambda s, n: (n, s))
        pw_rows = SUBLANES
    win = S5_SLAB_IN
    cb0 = col0 // win
    kern = functools.partial(_s5_body, rt=rt, seq_rows=seq_rows)
    wspec_b = pl.BlockSpec((None, win, st), lambda s, n: (s, 0, 0))
    wspec_c = pl.BlockSpec((None, st, win), lambda s, n: (s, 0, 0))
    vspec = pl.BlockSpec((1, st), lambda s, n: (0, s))
    return pl.pallas_call(
        kern,
        grid=(S5_SLABS, m // rt),
        in_specs=[pl.BlockSpec((rt, win), lambda s, n: (n, cb0 + s)), wspec_b, wspec_b, wspec_c, wspec_c,
                  vspec, vspec, pl.BlockSpec((1, win), lambda s, n: (0, s)), hspec, hspec],
        out_specs=[pl.BlockSpec((rt, win), lambda s, n: (n, s)), hspec, hspec],
        out_shape=[jax.ShapeDtypeStruct((m, MIX_B), F32),
                   jax.ShapeDtypeStruct((n_seq, S5_GROUPS * S5_STATE), F32),
                   jax.ShapeDtypeStruct((n_seq, S5_GROUPS * S5_STATE), F32)],
        scratch_shapes=[pltpu.VMEM((rt, st), F32), pltpu.VMEM((rt, st), F32),
                        pltpu.VMEM((pw_rows, st), F32), pltpu.VMEM((pw_rows, st), F32),
                        pltpu.VMEM((SUBLANES, st), F32), pltpu.VMEM((SUBLANES, st), F32),
                        pltpu.VMEM((win // LANES, rt, LANES), F32), pltpu.VMEM((win // LANES, rt, LANES), F32)],
        compiler_params=_cparams(("parallel", "arbitrary")),
        name="s5_scan",
    )(z, wbr, wbi, wcr, wci, ab_re.reshape(1, -1), ab_im.reshape(1, -1), d_skip.reshape(1, -1), h0_re, h0_im)


def _glu_body(a_ref, w_ref, b_ref, y_ref, o_ref):
    t = _dot(a_ref[...].astype(BF16), w_ref[...].astype(BF16)) + b_ref[...]
    o_ref[...] = (y_ref[...] * _sigmoid(t)).astype(o_ref.dtype)


def _glu(y, w, b):
    m, k = y.shape
    n = w.shape[1]
    tm = _pick(m, (1024, 512, 256, 128, 64, 32, 16, 8))
    tn = _pick(n, (512, 256, 128))
    return pl.pallas_call(
        _glu_body,
        grid=(m // tm, n // tn),
        in_specs=[pl.BlockSpec((tm, k), lambda i, j: (i, 0)), pl.BlockSpec((k, tn), lambda i, j: (0, j)),
                  pl.BlockSpec((1, tn), lambda i, j: (0, j)), pl.BlockSpec((tm, tn), lambda i, j: (i, j))],
        out_specs=pl.BlockSpec((tm, tn), lambda i, j: (i, j)),
        out_shape=jax.ShapeDtypeStruct((m, n), BF16),
        compiler_params=_cparams(("parallel", "arbitrary")),
        name="s5_glu",
    )(y, w, b.reshape(1, n), y)


def _mlstm_body(q_ref, k_ref, v_ref, og_ref, gt_ref, gb_ref, nw_ref, c0_ref, n0_ref, m0_ref,
                y_ref, cT_ref, nT_ref, mT_ref, c_scr, n_scr, m_scr, *, bb, rt, chunk, valid, unroll):
    h = pl.program_id(0)
    n = pl.program_id(2)
    n_last = pl.num_programs(2) - 1
    L = chunk
    nchunk = rt // L
    rl = _iota((L, L), 0)
    cl = _iota((L, L), 1)
    causal = rl >= cl
    tril = jnp.where(causal, 1.0, 0.0).astype(BF16)
    row = _iota((L, 1), 0)
    lane = _iota((L, LANES), 1)
    gbias = gb_ref[...]
    nw = nw_ref[...]
    scale = MLSTM_DQK ** -0.5

    def load(seq, r0, is_start):
        c_old = jnp.where(is_start, c0_ref[seq, 0], c_scr[seq])
        n_old = jnp.where(is_start, n0_ref[seq, 0], n_scr[seq, 0:1, :])
        m_prev = jnp.where(is_start, m0_ref[seq, 0], m_scr[seq, 0:1, 0:1])
        rows = pl.ds(r0, L)
        return (q_ref[seq, rows, :], k_ref[seq, rows, :], v_ref[seq, rows, :], og_ref[seq, rows, :],
                gt_ref[seq, rows, :], c_old, n_old, m_prev)

    def compute_all(loaded):
        nu = len(loaded)
        igs, lfs = [], []
        for q, k, v, og, gt, c_old, n_old, m_prev in loaded:
            gt = gt + gbias
            ig_raw = jnp.sum(jnp.where(lane == h, gt, 0.0), axis=-1, keepdims=True)
            fg_raw = jnp.sum(jnp.where(lane == MLSTM_HEADS + h, gt, 0.0), axis=-1, keepdims=True)
            ig = GATE_CAP * jnp.tanh(ig_raw / GATE_CAP)
            lf = -_softplus(-(GATE_CAP * jnp.tanh(fg_raw / GATE_CAP)))
            if valid < L:
                ok = row < valid
                ig = jnp.where(ok, ig, NEG_BIG)
                lf = jnp.where(ok, lf, 0.0)
            igs.append(ig)
            lfs.append(lf)
        lf_mat = jnp.zeros((L, LANES), F32)
        for u in range(nu):
            lf_mat = jnp.where(lane == u, lfs[u], lf_mat)
        b_mat = _dot_const(tril, lf_mat)
        pack = b_mat
        for u in range(nu):
            pack = jnp.where(lane == nu + u, igs[u], pack)
        pack_t = pack.T
        gates, qbs, kbs, ks = [], [], [], []
        for u, (q, k, v, og, gt, c_old, n_old, m_prev) in enumerate(loaded):
            bcol = b_mat[:, u:u + 1]
            b_row = pack_t[u:u + 1, :]
            ig_row = pack_t[nu + u:nu + u + 1, :]
            log_d = jnp.where(causal, bcol - b_row + ig_row, NEG_BIG)
            log_p = bcol + m_prev
            m_tok = jnp.maximum(log_p, jnp.max(log_d, axis=-1, keepdims=True))
            gates.append((jnp.exp(log_d - m_tok), jnp.exp(log_p - m_tok), m_tok, bcol))
            k = k * scale
            ks.append(k)
            qbs.append(q.astype(BF16))
            kbs.append(k.astype(BF16))
        ss = [_dot(qb, kb, NT) * g_[0] for qb, kb, g_ in zip(qbs, kbs, gates)]
        svs = [_dot(s.astype(BF16), x[2].astype(BF16)) for s, x in zip(ss, loaded)]
        qcs = [_dot(qb, x[5].astype(BF16), NT) for qb, x in zip(qbs, loaded)]
        upd = []
        for u, (q, k, v, og, gt, c_old, n_old, m_prev) in enumerate(loaded):
            d, p, m_tok, bcol = gates[u]
            m_new = m_tok[L - 1:L, :]
            b_last = bcol[L - 1:L, :]
            w_col = jnp.exp(b_last - bcol + igs[u] - m_new)
            cs = jnp.exp(b_last + m_prev - m_new)
            upd.append((w_col, cs, m_new))
        vks = [_dot((x[2] * w_[0]).astype(BF16), kb, TN) for x, w_, kb in zip(loaded, upd, kbs)]
        out = []
        for u, (q, k, v, og, gt, c_old, n_old, m_prev) in enumerate(loaded):
            d, p, m_tok, bcol = gates[u]
            w_col, cs, m_new = upd[u]
            num = svs[u] + p * qcs[u]
            den = jnp.sum(ss[u], axis=-1, keepdims=True) + p * jnp.sum(q * n_old, axis=-1, keepdims=True)
            hh = num / jnp.maximum(jnp.abs(den), jnp.exp(-m_tok))
            c_new = cs * c_old + vks[u]
            n_new = cs * n_old + jnp.sum(w_col * ks[u], axis=0, keepdims=True)
            hn = hh * lax.rsqrt(jnp.mean(hh * hh, axis=-1, keepdims=True) + RMS_EPS)
            out.append(((hn * nw * _sigmoid(og)).astype(y_ref.dtype), c_new, n_new, m_new))
        return out

    def store(seq, r0, is_end, y, c_new, n_new, m_new):
        y_ref[seq, pl.ds(r0, L), :] = y
        c_scr[seq] = c_new
        n_scr[seq, 0:1, :] = n_new
        m_scr[seq, 0:1, :] = jnp.broadcast_to(m_new, (1, LANES))

        @pl.when(is_end)
        def _():
            cT_ref[seq, 0] = c_new
            nT_ref[seq, 0] = n_new
            mT_ref[seq, 0] = m_new

    def step(it, carry):
        sg = it // nchunk
        ci_ = it - sg * nchunk
        r0 = pl.multiple_of(ci_ * L, L)
        is_start = jnp.logical_and(n == 0, ci_ == 0)
        is_end = jnp.logical_and(n == n_last, ci_ == nchunk - 1)
        seqs = [sg * unroll + u for u in range(unroll)]
        loaded = [load(s, r0, is_start) for s in seqs]
        done = compute_all(loaded)
        for s, d in zip(seqs, done):
            store(s, r0, is_end, *d)
        return carry

    @pl.when(n == 0)
    def _():
        c_scr[...] = jnp.zeros_like(c_scr)
        n_scr[...] = jnp.zeros_like(n_scr)
        m_scr[...] = jnp.zeros_like(m_scr)

    lax.fori_loop(0, (bb // unroll) * nchunk, step, 0)


def _mlstm(z, c0, n0, m0, b_i, b_f, norm_w, *, n_seq, seq_rows, chunk, valid):
    nh, dqk, dv = MLSTM_HEADS, MLSTM_DQK, MLSTM_DV
    unroll = _pick(n_seq, (4, 2, 1))
    if seq_rows > chunk:
        rt = _pick(seq_rows, (256, 128, 64))
        bb = unroll
    else:
        rt = seq_rows
        bb = _pick(n_seq, (8, 4, 2, 1))
    assert rt % chunk == 0 and bb % unroll == 0
    sblk = lambda h, g, n: (g, h, 0, 0)
    gbias = jnp.pad(jnp.concatenate([b_i, b_f]).reshape(1, 2 * nh), ((0, 0), (0, LANES - 2 * nh)))
    n0r = n0.reshape(n_seq, nh, 1, dqk)
    m0r = m0.reshape(n_seq, nh, 1, 1)
    kern = functools.partial(_mlstm_body, bb=bb, rt=rt, chunk=chunk, valid=valid, unroll=unroll)
    y, c_t, n_t, m_t = pl.pallas_call(
        kern,
        grid=(nh, n_seq // bb, seq_rows // rt),
        in_specs=[
            pl.BlockSpec((bb, rt, dqk), lambda h, g, n: (g, n, h)),
            pl.BlockSpec((bb, rt, dqk), lambda h, g, n: (g, n, nh + h)),
            pl.BlockSpec((bb, rt, dv), lambda h, g, n: (g, n, (2 * nh * dqk) // dv + h)),
            pl.BlockSpec((bb, rt, dv), lambda h, g, n: (g, n, ZO_OG // dv + h)),
            pl.BlockSpec((bb, rt, LANES), lambda h, g, n: (g, n, ZO_G // LANES)),
            pl.BlockSpec((1, LANES), lambda h, g, n: (0, 0)),
            pl.BlockSpec((1, dv), lambda h, g, n: (0, h)),
            pl.BlockSpec((bb, 1, dv, dqk), sblk),
            pl.BlockSpec((bb, 1, 1, dqk), sblk),
            pl.BlockSpec((bb, 1, 1, 1), sblk),
        ],
        out_specs=[pl.BlockSpec((bb, rt, dv), lambda h, g, n: (g, n, h)),
                   pl.BlockSpec((bb, 1, dv, dqk), sblk),
                   pl.BlockSpec((bb, 1, 1, dqk), sblk),
                   pl.BlockSpec((bb, 1, 1, 1), sblk)],
        out_shape=[jax.ShapeDtypeStruct((n_seq, seq_rows, nh * dv), BF16),
                   jax.ShapeDtypeStruct((n_seq, nh, dv, dqk), F32),
                   jax.ShapeDtypeStruct((n_seq, nh, 1, dqk), F32),
                   jax.ShapeDtypeStruct((n_seq, nh, 1, 1), F32)],
        scratch_shapes=[pltpu.VMEM((bb, dv, dqk), F32), pltpu.VMEM((bb, SUBLANES, dqk), F32),
                        pltpu.VMEM((bb, SUBLANES, LANES), F32)],
        compiler_params=_cparams(("parallel", "parallel", "arbitrary")),
        name="mlstm",
    )(z, z, z, z, z, gbias, norm_w.reshape(1, nh * dv), c0, n0r, m0r)
    return y, c_t, n_t.reshape(n_seq, nh, dqk), m_t.reshape(n_seq, nh)


def _pad_cols(a, n):
    return jnp.pad(a, ((0, 0), (0, n - a.shape[1])))


def _pad_rows(a, n):
    return jnp.pad(a, ((0, n - a.shape[0]), (0, 0)))


def _rwkv_cols(a):
    c1 = 3 * MIX_A
    c2 = c1 + RWKV_DECAY_RANK
    c3 = c2 + RWKV_ICL_RANK
    return jnp.concatenate([a[:, :c1], _pad_cols(a[:, c1:c2], RANK_PAD), _pad_cols(a[:, c2:c3], RANK_PAD), a[:, c3:]], axis=1)


def _rwkv_cols_inv(a):
    return jnp.concatenate([a[:, :ZE_DW + RWKV_DECAY_RANK], a[:, ZE_DA:ZE_DA + RWKV_ICL_RANK], a[:, ZE_DG:ZE_U]], axis=1)


def _pad_seq(a, n_seq, t, tp):
    a = a.reshape(n_seq, t, a.shape[1])
    return a if tp == t else jnp.pad(a, ((0, 0), (0, tp - t), (0, 0)))


def _unpad_seq(a, n_seq, t, tp):
    a = a if tp == t else a[:, :t]
    return a.reshape(n_seq * t, a.shape[2])


def _s5_block_weights(zr, zi, b_re, b_im, c_re, c_im):
    g, n, c = S5_GROUPS, S5_STATE, S5_GROUP
    sg = S5_SLAB_GROUPS
    bb_re = zr[..., None] * b_re - zi[..., None] * b_im
    bb_im = zr[..., None] * b_im + zi[..., None] * b_re
    eye = jnp.eye(sg, dtype=F32)

    def in_blocks(bb):
        t = bb.reshape(S5_SLABS, sg, n, c)
        return jnp.einsum('sgnc,gh->sgchn', t, eye).reshape(S5_SLABS, sg * c, sg * n)

    def out_blocks(cc):
        t = cc.reshape(S5_SLABS, sg, c, n)
        return jnp.einsum('sgcn,gh->sgnhc', t, eye).reshape(S5_SLABS, sg * n, sg * c)

    return in_blocks(bb_re), in_blocks(bb_im), out_blocks(c_re), out_blocks(-c_im)


def _trunk(x, n_seq, t, st, w):
    tp = t if t >= MLSTM_CHUNK else SUBLANES * ((t + SUBLANES - 1) // SUBLANES)
    chunk = MLSTM_CHUNK if t >= MLSTM_CHUNK else tp
    valid = chunk if tp == t else t
    assert t % chunk == 0 or tp == chunk

    xn = _rmsnorm(x, w['ln_mix_e'], BF16)
    z = _mm_split(xn, w['w_in_e'], ZE_DW, w['w_in_e_tail'])
    rw = (w['mu'], w['w0'], w['w_up'], w['a0'], w['a_up'], w['g_up'], w['k_k'], w['k_a'], w['r_k'], w['lnx_w'], w['lnx_b'])
    if t < chunk and n_seq == LANES:
        z_tb = z.reshape(n_seq, t, ZE_W).transpose(1, 0, 2)
        ya, wkv_t = _rwkv_step(z_tb, _rwkv_cols(st['shift']), st['wkv'].transpose(1, 2, 3, 0), *rw)
        ya = ya.transpose(1, 0, 2).reshape(n_seq * t, MIX_A)
        wkv_t = wkv_t.transpose(3, 0, 1, 2)
    else:
        ya, wkv_t = _rwkv(_pad_seq(z, n_seq, t, tp), _rwkv_cols(st['shift']), st['wkv'], *rw,
                          n_seq=n_seq, seq_rows=tp, chunk=chunk, valid=valid)
        ya = _unpad_seq(ya, n_seq, t, tp)
    shift_t = _rwkv_cols_inv(z.reshape(n_seq, t, ZE_W)[:, -1, :ZE_U])
    yb, re_t, im_t = _s5_scan(z, ZE_U, w['s5_wbr'], w['s5_wbi'], w['s5_wcr'], w['s5_wci'], w['s5_ab_re'], w['s5_ab_im'],
                              w['s5_d'], st['s5_re'], st['s5_im'], n_seq=n_seq, seq_rows=t)
    yb = _glu(yb, w['s5_glu_w'], w['s5_glu_b'])
    x = _mm2_res(ya, yb, w['w_out_e'], x)
    hn = _rmsnorm(x, w['ln_ffn_e'], BF16)
    hid = _glu_up(hn, w['ffn_w1'], w['ffn_w3'])
    x = _mm_acc(hid, w['ffn_w2'], x)

    xn = _rmsnorm(x, w['ln_mix_o'], BF16)
    zo = _mm_split(xn, w['w_in_o'], ZO_OG, w['w_in_o_tail'])
    zop = _pad_seq(zo, n_seq, t, tp)
    yc, c_t, n_t, m_t = _mlstm(zop, st['c'], st['n'], st['m'], w['b_i'], w['b_f'], w['norm_w'],
                               n_seq=n_seq, seq_rows=tp, chunk=chunk, valid=valid)
    yc = _unpad_seq(yc, n_seq, t, tp)
    x = _mm_res(yc, w['w_out_o'], x)
    hn, comb = _rmsnorm_router(x, w['ln_ffn_o'], w['router_w'], w['router_b'])
    hid = _moe_up(hn, comb, w['exp_w1'], w['exp_w3'])
    x = _mm_acc(hid, w['exp_w2'], x)
    y = _rmsnorm(x, w['final_norm'], F32)

    g, ns = S5_GROUPS, S5_STATE
    return (y, wkv_t[None], shift_t[None], re_t.reshape(1, n_seq, g, ns), im_t.reshape(1, n_seq, g, ns),
            c_t[None], n_t[None], m_t[None])


def kernel(x_prompt, x_sample, state_rwkv_wkv, state_rwkv_shift, state_s5_re, state_s5_im, state_mlstm_c, state_mlstm_n, state_mlstm_m, ln_mix_e, w_in_e, rwkv_mu, rwkv_w0, rwkv_w_up, rwkv_a0, rwkv_a_up, rwkv_g_up, rwkv_k_k, rwkv_k_a, rwkv_r_k, rwkv_lnx_w, rwkv_lnx_b, s5_a_re, s5_a_im, s5_log_step, s5_b_re, s5_b_im, s5_c_re, s5_c_im, s5_d, s5_glu_w, s5_glu_b, w_out_e, ln_ffn_e, ffn_w1, ffn_w3, ffn_w2, ln_mix_o, w_in_o, mlstm_b_i, mlstm_b_f, mlstm_norm_w, w_out_o, ln_ffn_o, router_w, router_b, exp_w1, exp_w3, exp_w2, final_norm):
    assert ln_mix_e.shape[0] == 1 and ln_mix_o.shape[0] == 1
    d = D_MODEL
    ab_re, ab_im, zr, zi = _s5_prep(s5_a_re[0], s5_a_im[0], s5_log_step[0])
    wbr, wbi, wcr, wci = _s5_block_weights(zr, zi, s5_b_re[0], s5_b_im[0], s5_c_re[0], s5_c_im[0])
    wo = jnp.swapaxes(w_in_o[0], 0, 1)
    we = jnp.swapaxes(w_in_e[0], 0, 1)
    c_dw = 3 * MIX_A
    c_da = c_dw + RWKV_DECAY_RANK
    c_dg = c_da + RWKV_ICL_RANK
    nh = MLSTM_HEADS
    w = {
        'ln_mix_e': ln_mix_e[0],
        'w_in_e': we,
        'w_in_e_tail': jnp.concatenate([_pad_rows(we[c_dw:c_da], RANK_PAD), _pad_rows(we[c_da:c_dg], RANK_PAD),
                                        we[c_dg:]], axis=0),
        'mu': _rwkv_cols(rwkv_mu[0].reshape(1, RWKV_IN)),
        'w0': rwkv_w0[0], 'a0': rwkv_a0[0],
        'w_up': jnp.pad(rwkv_w_up[0], ((0, RANK_PAD - RWKV_DECAY_RANK), (0, 0))),
        'a_up': jnp.pad(rwkv_a_up[0], ((0, RANK_PAD - RWKV_ICL_RANK), (0, 0))),
        'g_up': rwkv_g_up[0], 'k_k': rwkv_k_k[0], 'k_a': rwkv_k_a[0], 'r_k': rwkv_r_k[0].reshape(MIX_A),
        'lnx_w': rwkv_lnx_w[0], 'lnx_b': rwkv_lnx_b[0],
        's5_wbr': wbr, 's5_wbi': wbi, 's5_wcr': wcr, 's5_wci': wci,
        's5_ab_re': ab_re, 's5_ab_im': ab_im, 's5_d': s5_d[0], 's5_glu_w': s5_glu_w[0], 's5_glu_b': s5_glu_b[0],
        'w_out_e': w_out_e[0], 'ln_ffn_e': ln_ffn_e[0],
        'ffn_w1': ffn_w1[0], 'ffn_w3': ffn_w3[0], 'ffn_w2': ffn_w2[0],
        'ln_mix_o': ln_mix_o[0],
        'w_in_o': wo,
        'w_in_o_tail': jnp.concatenate([wo[ZO_OG + 2 * nh:], _pad_rows(wo[ZO_OG:ZO_OG + 2 * nh], ZO_W - ZO_G)], axis=0),
        'b_i': mlstm_b_i[0], 'b_f': mlstm_b_f[0], 'norm_w': mlstm_norm_w[0],
        'w_out_o': w_out_o[0], 'ln_ffn_o': ln_ffn_o[0],
        'router_w': router_w[0], 'router_b': router_b[0],
        'exp_w1': exp_w1[0], 'exp_w3': exp_w3[0], 'exp_w2': exp_w2[0].reshape(N_EXPERTS * EXPERT_FF, d),
        'final_norm': final_norm,
    }
    bp, tp_, _ = x_prompt.shape
    bs, ts_, _ = x_sample.shape
    g, ns = S5_GROUPS, S5_STATE
    zero = lambda *s: jnp.zeros(s, F32)
    st_p = {'wkv': zero(bp, RWKV_HEADS, RWKV_HEAD, RWKV_HEAD), 'shift': zero(bp, RWKV_IN),
            's5_re': zero(bp, g * ns), 's5_im': zero(bp, g * ns),
            'c': zero(bp, nh, MLSTM_DV, MLSTM_DQK), 'n': zero(bp, nh, MLSTM_DQK), 'm': zero(bp, nh)}
    st_s = {'wkv': state_rwkv_wkv[0], 'shift': state_rwkv_shift[0],
            's5_re': state_s5_re[0].reshape(bs, g * ns), 's5_im': state_s5_im[0].reshape(bs, g * ns),
            'c': state_mlstm_c[0], 'n': state_mlstm_n[0], 'm': state_mlstm_m[0]}
    out_p = _trunk(x_prompt.reshape(bp * tp_, d), bp, tp_, st_p, w)
    out_s = _trunk(x_sample.reshape(bs * ts_, d), bs, ts_, st_s, w)
    y_p = out_p[0].reshape(bp, tp_, d)
    y_s = out_s[0].reshape(bs, ts_, d)
    return (y_p, y_s) + tuple(out_p[1:]) + tuple(out_s[1:])
```

```python
import functools
import math

import jax
import jax.numpy as jnp
from jax import lax
from jax.experimental import pallas as pl
from jax.experimental.pallas import tpu as pltpu

F32 = jnp.float32
BF16 = jnp.bfloat16

D_MODEL = 4096
MIX_A = D_MODEL // 2
MIX_B = D_MODEL - MIX_A
RWKV_HEAD = 64
RWKV_HEADS = MIX_A // RWKV_HEAD
RWKV_DECAY_RANK = 96
RWKV_ICL_RANK = 96
RWKV_GATE_RANK = 256
RWKV_IN = 3 * MIX_A + RWKV_DECAY_RANK + RWKV_ICL_RANK + RWKV_GATE_RANK
RWKV_LNX_EPS = 1e-5 * RWKV_HEAD
S5_GROUP = 16
S5_GROUPS = MIX_B // S5_GROUP
S5_STATE = 64
MLSTM_HEADS = 8
MLSTM_DQK = D_MODEL // 16
MLSTM_DV = D_MODEL // 8
MLSTM_CHUNK = 64
GATE_CAP = 15.0
FFN_DIM = 11008
N_EXPERTS = 8
EXPERT_FF = D_MODEL // 2
RMS_EPS = 1e-6

LANES = 128
SUBLANES = 8
VMEM_LIMIT = 56 * 1024 * 1024

RANK_PAD = LANES
ZE_DW = 3 * MIX_A
ZE_DA = ZE_DW + RANK_PAD
ZE_DG = ZE_DA + RANK_PAD
ZE_U = ZE_DG + RWKV_GATE_RANK
ZE_W = ZE_U + MIX_B
ZT_DW = 0
ZT_DA = ZT_DW + RANK_PAD
ZT_DG = ZT_DA + RANK_PAD
ZT_U = ZT_DG + RWKV_GATE_RANK
RW_CAT = 3 * LANES + 2 * RANK_PAD + RWKV_GATE_RANK
ZO_OG = 2 * MLSTM_HEADS * MLSTM_DQK + MLSTM_HEADS * MLSTM_DV
ZO_G = ZO_OG + MLSTM_HEADS * MLSTM_DV
ZO_W = ZO_G + 4 * LANES

S5_SLAB_GROUPS = 16
S5_SLABS = S5_GROUPS // S5_SLAB_GROUPS
S5_SLAB_IN = S5_SLAB_GROUPS * S5_GROUP
S5_SLAB_STATE = S5_SLAB_GROUPS * S5_STATE

NN = (((1,), (0,)), ((), ()))
NT = (((1,), (1,)), ((), ()))
TN = (((0,), (0,)), ((), ()))
NEG_BIG = -1e30


def _cparams(sem):
    return pltpu.CompilerParams(dimension_semantics=sem, vmem_limit_bytes=VMEM_LIMIT)


def _pick(n, cands):
    for c in cands:
        if n % c == 0:
            return c
    raise ValueError(f"no tile for {n}")


def _dot(a, b, dims=NN):
    return lax.dot_general(a, b, dims, preferred_element_type=F32)


def _split2(a):
    hi = a.astype(BF16)
    lo = (a - hi.astype(F32)).astype(BF16)
    return hi, lo


def _dot_hp(a, b, dims=NN):
    ah, al = _split2(a)
    bh, bl = _split2(b)
    return _dot(ah, bh, dims) + (_dot(ah, bl, dims) + _dot(al, bh, dims))


def _dot_const(c_bf16, a):
    h0 = a.astype(BF16)
    r1 = a - h0.astype(F32)
    h1 = r1.astype(BF16)
    h2 = (r1 - h1.astype(F32)).astype(BF16)
    return _dot(c_bf16, h0) + (_dot(c_bf16, h1) + _dot(c_bf16, h2))


def _sigmoid(x):
    return 1.0 / (1.0 + jnp.exp(-x))


def _softplus(x):
    return jnp.maximum(x, 0.0) + jnp.log(1.0 + jnp.exp(-jnp.abs(x)))


def _iota(shape, dim):
    return lax.broadcasted_iota(jnp.int32, shape, dim)


def _rmsnorm_body(x_ref, g_ref, o_ref):
    x = x_ref[...]
    ms = jnp.mean(x * x, axis=-1, keepdims=True)
    o_ref[...] = (x * lax.rsqrt(ms + RMS_EPS) * g_ref[...]).astype(o_ref.dtype)


def _rmsnorm(x, g, out_dtype):
    m, d = x.shape
    tm = _pick(m, (256, 128, 64, 32, 16, 8))
    return pl.pallas_call(
        _rmsnorm_body,
        grid=(m // tm,),
        in_specs=[pl.BlockSpec((tm, d), lambda i: (i, 0)), pl.BlockSpec((1, d), lambda i: (0, 0))],
        out_specs=pl.BlockSpec((tm, d), lambda i: (i, 0)),
        out_shape=jax.ShapeDtypeStruct((m, d), out_dtype),
        compiler_params=_cparams(("parallel",)),
        name="rmsnorm",
    )(x, g.reshape(1, d))


def _router_body(x_ref, g_ref, rw_ref, rb_ref, hn_ref, comb_ref):
    x = x_ref[...]
    ms = jnp.mean(x * x, axis=-1, keepdims=True)
    h = x * lax.rsqrt(ms + RMS_EPS) * g_ref[...]
    hn_ref[...] = h.astype(hn_ref.dtype)
    logits = _dot_hp(h, rw_ref[...]) + rb_ref[...]
    lane = _iota(logits.shape, 1)
    logits = jnp.where(lane < N_EXPERTS, logits, NEG_BIG)
    m1 = jnp.max(logits, axis=-1, keepdims=True)
    i1 = jnp.min(jnp.where(logits == m1, lane, LANES), axis=-1, keepdims=True)
    rest = jnp.where(lane == i1, NEG_BIG, logits)
    m2 = jnp.max(rest, axis=-1, keepdims=True)
    i2 = jnp.min(jnp.where(rest == m2, lane, LANES), axis=-1, keepdims=True)
    e = jnp.exp(m2 - m1)
    g1 = 1.0 / (1.0 + e)
    g2 = e / (1.0 + e)
    comb_ref[...] = jnp.where(lane == i1, g1, 0.0) + jnp.where(lane == i2, g2, 0.0)


def _rmsnorm_router(x, g, router_w, router_b):
    m, d = x.shape
    tm = _pick(m, (256, 128, 64, 32, 16, 8))
    rw = jnp.pad(router_w, ((0, 0), (0, LANES - N_EXPERTS)))
    rb = jnp.pad(router_b.reshape(1, N_EXPERTS), ((0, 0), (0, LANES - N_EXPERTS)))
    return pl.pallas_call(
        _router_body,
        grid=(m // tm,),
        in_specs=[pl.BlockSpec((tm, d), lambda i: (i, 0)), pl.BlockSpec((1, d), lambda i: (0, 0)),
                  pl.BlockSpec((d, LANES), lambda i: (0, 0)), pl.BlockSpec((1, LANES), lambda i: (0, 0))],
        out_specs=[pl.BlockSpec((tm, d), lambda i: (i, 0)), pl.BlockSpec((tm, LANES), lambda i: (i, 0))],
        out_shape=[jax.ShapeDtypeStruct((m, d), BF16), jax.ShapeDtypeStruct((m, LANES), F32)],
        compiler_params=_cparams(("parallel",)),
        name="rmsnorm_router",
    )(x, g.reshape(1, d), rw, rb)


def _mm_body(a_ref, w_ref, o_ref):
    o_ref[...] = _dot(a_ref[...], w_ref[...].astype(BF16)).astype(o_ref.dtype)


def _mm(a, w, n_out=None):
    m, k = a.shape
    n = w.shape[1] if n_out is None else n_out
    tm = _pick(m, (1024, 512, 256, 128, 64, 32, 16, 8))
    tn = _pick(n, (512, 256, 128))
    return pl.pallas_call(
        _mm_body,
        grid=(m // tm, n // tn),
        in_specs=[pl.BlockSpec((tm, k), lambda i, j: (i, 0)), pl.BlockSpec((k, tn), lambda i, j: (0, j))],
        out_specs=pl.BlockSpec((tm, tn), lambda i, j: (i, j)),
        out_shape=jax.ShapeDtypeStruct((m, n), F32),
        compiler_params=_cparams(("parallel", "arbitrary")),
        name="proj_in",
    )(a, w)


def _mm_nt_body(a_ref, wt_ref, o_ref):
    o_ref[...] = _dot(a_ref[...], wt_ref[...].astype(BF16), NT)


def _mm_nt(a, wt, n_rows):
    m, k = a.shape
    tm = _pick(m, (1024, 512, 256, 128, 64, 32, 16, 8))
    tn = 4 * LANES
    assert n_rows % tn == 0
    return pl.pallas_call(
        _mm_nt_body,
        grid=(m // tm, n_rows // tn),
        in_specs=[pl.BlockSpec((tm, k), lambda i, j: (i, 0)), pl.BlockSpec((tn, k), lambda i, j: (j, 0))],
        out_specs=pl.BlockSpec((tm, tn), lambda i, j: (i, j)),
        out_shape=jax.ShapeDtypeStruct((m, n_rows), F32),
        compiler_params=_cparams(("parallel", "arbitrary")),
        name="proj_in",
    )(a, wt)


def _mm_split(a, w_t, n_main, w_tail_t):
    return _mm_nt(a, w_t, n_main), _mm_nt(a, w_tail_t, w_tail_t.shape[0])


def _mm2_res_body(a0_ref, a1_ref, w0_ref, w1_ref, r_ref, o_ref):
    acc = _dot(a0_ref[...], w0_ref[...].astype(BF16)) + _dot(a1_ref[...], w1_ref[...].astype(BF16))
    o_ref[...] = r_ref[...] + acc


def _mm2_res(a0, a1, w, res):
    m, k0 = a0.shape
    k1 = a1.shape[1]
    assert k0 == k1
    n = w.shape[1]
    tm = _pick(m, (1024, 512, 256, 128, 64, 32, 16, 8))
    tn = _pick(n, (512, 256, 128))
    return pl.pallas_call(
        _mm2_res_body,
        grid=(m // tm, n // tn),
        in_specs=[pl.BlockSpec((tm, k0), lambda i, j: (i, 0)), pl.BlockSpec((tm, k1), lambda i, j: (i, 0)),
                  pl.BlockSpec((k0, tn), lambda i, j: (0, j)), pl.BlockSpec((k1, tn), lambda i, j: (1, j)),
                  pl.BlockSpec((tm, tn), lambda i, j: (i, j))],
        out_specs=pl.BlockSpec((tm, tn), lambda i, j: (i, j)),
        out_shape=jax.ShapeDtypeStruct((m, n), F32),
        compiler_params=_cparams(("parallel", "arbitrary")),
        name="proj_out",
    )(a0, a1, w, w, res)


def _mm_res_body(a_ref, w_ref, r_ref, o_ref):
    o_ref[...] = r_ref[...] + _dot(a_ref[...], w_ref[...].astype(BF16))


def _mm_res(a, w, res):
    m, k = a.shape
    n = w.shape[1]
    tm = _pick(m, (1024, 512, 256, 128, 64, 32, 16, 8))
    tn = _pick(n, (512, 256, 128))
    return pl.pallas_call(
        _mm_res_body,
        grid=(m // tm, n // tn),
        in_specs=[pl.BlockSpec((tm, k), lambda i, j: (i, 0)), pl.BlockSpec((k, tn), lambda i, j: (0, j)),
                  pl.BlockSpec((tm, tn), lambda i, j: (i, j))],
        out_specs=pl.BlockSpec((tm, tn), lambda i, j: (i, j)),
        out_shape=jax.ShapeDtypeStruct((m, n), F32),
        compiler_params=_cparams(("parallel", "arbitrary")),
        name="proj_out1",
    )(a, w, res)


def _glu_up_body(a_ref, w1_ref, w3_ref, o_ref):
    a = a_ref[...]
    h1 = _dot(a, w1_ref[...].astype(BF16))
    h3 = _dot(a, w3_ref[...].astype(BF16))
    o_ref[...] = (h1 * _sigmoid(h1) * h3).astype(o_ref.dtype)


def _glu_up(a, w1, w3):
    m, k = a.shape
    n = w1.shape[1]
    tm = _pick(m, (1024, 512, 256, 128, 64, 32, 16, 8))
    tn = _pick(n, (256, 128))
    return pl.pallas_call(
        _glu_up_body,
        grid=(m // tm, n // tn),
        in_specs=[pl.BlockSpec((tm, k), lambda i, j: (i, 0)), pl.BlockSpec((k, tn), lambda i, j: (0, j)),
                  pl.BlockSpec((k, tn), lambda i, j: (0, j))],
        out_specs=pl.BlockSpec((tm, tn), lambda i, j: (i, j)),
        out_shape=jax.ShapeDtypeStruct((m, n), BF16),
        compiler_params=_cparams(("parallel", "arbitrary")),
        name="ffn_up",
    )(a, w1, w3)


def _mm_acc_body(a_ref, w_ref, r_ref, o_ref, *, k_total, tk, rc):
    kk = pl.program_id(2)
    ragged = k_total % tk != 0
    valid = k_total - kk * tk
    w = w_ref[...]
    if ragged:
        w = jnp.where(_iota(w.shape, 0) < valid, w, 0.0)
    w = w.astype(BF16)

    @pl.when(kk == 0)
    def _():
        o_ref[...] = r_ref[...]

    a = a_ref[...]
    if ragged:
        a = jnp.where(_iota(a.shape, 1) < valid, a, jnp.zeros_like(a))
    for c0 in range(0, w.shape[1], rc):
        o_ref[:, c0:c0 + rc] += _dot(a, w[:, c0:c0 + rc])


def _mm_acc(a, w, res):
    m, k = a.shape
    n = w.shape[1]
    tm = _pick(m, (2048, 1024, 512, 256, 128, 64, 32, 16, 8))
    tn = _pick(n, (1024, 512, 256, 128))
    tk = 512
    nk = pl.cdiv(k, tk)
    return pl.pallas_call(
        functools.partial(_mm_acc_body, k_total=k, tk=tk, rc=min(tn, 2 * LANES)),
        grid=(m // tm, n // tn, nk),
        in_specs=[pl.BlockSpec((tm, tk), lambda i, j, kk: (i, kk)), pl.BlockSpec((tk, tn), lambda i, j, kk: (kk, j)),
                  pl.BlockSpec((tm, tn), lambda i, j, kk: (i, j))],
        out_specs=pl.BlockSpec((tm, tn), lambda i, j, kk: (i, j)),
        out_shape=jax.ShapeDtypeStruct((m, n), F32),
        compiler_params=_cparams(("parallel", "parallel", "arbitrary")),
        name="proj_down",
    )(a, w, res)


MOE_TM = 512
MOE_BLK = 256
MOE_ALIGN = 16


def _moe_rank_body(comb_ref, rank_ref, cnt_ref):
    comb = comb_ref[...]
    tt = comb.shape[0]
    lane = _iota(comb.shape, 1)
    sel = jnp.where((comb > 0.0) & (lane < N_EXPERTS), 1.0, 0.0)
    below = jnp.where(_iota((tt, tt), 0) > _iota((tt, tt), 1), 1.0, 0.0).astype(BF16)
    rank = _dot(below, sel.astype(BF16))
    rank_ref[...] = rank
    cnt_ref[...] = rank[tt - 1:tt, :] + sel[tt - 1:tt, :]


def _moe_rank(comb, tt):
    m = comb.shape[0]
    nt = m // tt
    return pl.pallas_call(
        _moe_rank_body,
        grid=(nt,),
        in_specs=[pl.BlockSpec((tt, LANES), lambda i: (i, 0))],
        out_specs=[pl.BlockSpec((tt, LANES), lambda i: (i, 0)), pl.BlockSpec((None, 1, LANES), lambda i: (i, 0, 0))],
        out_shape=[jax.ShapeDtypeStruct((m, LANES), F32), jax.ShapeDtypeStruct((nt, 1, LANES), F32)],
        compiler_params=_cparams(("parallel",)),
        name="moe_rank",
    )(comb)


def _moe_dispatch_body(seg_ref, base_ref, x_ref, rank_ref, comb_ref, xg_in_ref, xg_ref, buf, sem, *, tt):
    del xg_in_ref
    i = pl.program_id(0)
    x = x_ref[...]
    rank_t = rank_ref[...].T
    comb_t = comb_ref[...].T
    rr = _iota((MOE_BLK, tt), 0)

    def copy(slot, row0):
        return pltpu.make_async_copy(buf.at[slot], xg_ref.at[pl.ds(pl.multiple_of(row0, MOE_ALIGN), MOE_BLK)], sem.at[slot])

    def block(e, b, slot, base_e):
        hit = (rank_t[e:e + 1, :] == (rr + b * MOE_BLK).astype(F32)) & (comb_t[e:e + 1, :] > 0.0)
        onehot = jnp.where(hit, 1.0, 0.0).astype(BF16)
        buf[slot] = _dot(onehot, x).astype(buf.dtype)
        copy(slot, base_e + b * MOE_BLK).start()

    for e in range(N_EXPERTS):
        slot = e % 2
        seg_e = seg_ref[i * N_EXPERTS + e]
        base_e = base_ref[i * N_EXPERTS + e]
        if e >= 2:
            copy(slot, 0).wait()
        block(e, 0, slot, base_e)

        def extra(b, carry, e=e, slot=slot, base_e=base_e):
            copy(slot, 0).wait()
            block(e, b, slot, base_e)
            return carry

        lax.fori_loop(1, (seg_e + MOE_BLK - 1) // MOE_BLK, extra, 0)
    copy(0, 0).wait()
    copy(1, 0).wait()


def _moe_dispatch(hn, rank, comb, seg, base, p_max, tt):
    m, d = hn.shape
    return pl.pallas_call(
        functools.partial(_moe_dispatch_body, tt=tt),
        grid_spec=pltpu.PrefetchScalarGridSpec(
            num_scalar_prefetch=2,
            grid=(m // tt,),
            in_specs=[pl.BlockSpec((tt, d), lambda i, *_: (i, 0)), pl.BlockSpec((tt, LANES), lambda i, *_: (i, 0)),
                      pl.BlockSpec((tt, LANES), lambda i, *_: (i, 0)), pl.BlockSpec(memory_space=pl.ANY)],
            out_specs=pl.BlockSpec(memory_space=pl.ANY),
            scratch_shapes=[pltpu.VMEM((2, MOE_BLK, d), BF16), pltpu.SemaphoreType.DMA((2,))],
        ),
        out_shape=jax.ShapeDtypeStruct((p_max, d), BF16),
        input_output_aliases={5: 0},
        compiler_params=_cparams(("arbitrary",)),
        name="moe_dispatch",
    )(seg, base, hn, rank, comb, jnp.zeros((p_max, d), BF16))


def _moe_gup_body(te_ref, nv_ref, x_ref, w1_ref, w3_ref, o_ref):
    k = pl.program_id(1)

    @pl.when(k < nv_ref[0])
    def _():
        x = x_ref[...]
        h1 = _dot(x, w1_ref[...].astype(BF16))
        h3 = _dot(x, w3_ref[...].astype(BF16))
        o_ref[...] = (h1 * _sigmoid(h1) * h3).astype(o_ref.dtype)

    @pl.when(k >= nv_ref[0])
    def _():
        o_ref[...] = jnp.zeros_like(o_ref)


def _moe_gdown_body(te_ref, nv_ref, h_ref, w_ref, o_ref):
    k = pl.program_id(1)

    @pl.when(k < nv_ref[0])
    def _():
        o_ref[...] = _dot(h_ref[...], w_ref[...].astype(BF16)).astype(o_ref.dtype)

    @pl.when(k >= nv_ref[0])
    def _():
        o_ref[...] = jnp.zeros_like(o_ref)


def _moe_grouped(xg, te, nv, w1, w3, w2):
    p, d = xg.shape
    _, _, f = w1.shape
    tm = MOE_TM
    nk = p // tm
    tn = 4 * LANES
    rowi = lambda k, nv_: jnp.minimum(k, nv_[0] - 1)
    hid = pl.pallas_call(
        _moe_gup_body,
        grid_spec=pltpu.PrefetchScalarGridSpec(
            num_scalar_prefetch=2,
            grid=(f // tn, nk),
            in_specs=[pl.BlockSpec((tm, d), lambda j, k, te_, nv_: (rowi(k, nv_), 0)),
                      pl.BlockSpec((None, d, tn), lambda j, k, te_, nv_: (te_[rowi(k, nv_)], 0, j)),
                      pl.BlockSpec((None, d, tn), lambda j, k, te_, nv_: (te_[rowi(k, nv_)], 0, j))],
            out_specs=pl.BlockSpec((tm, tn), lambda j, k, te_, nv_: (k, j)),
        ),
        out_shape=jax.ShapeDtypeStruct((p, f), BF16),
        compiler_params=_cparams(("arbitrary", "arbitrary")),
        name="moe_up",
    )(te, nv, xg, w1, w3)
    tn2 = 8 * LANES
    return pl.pallas_call(
        _moe_gdown_body,
        grid_spec=pltpu.PrefetchScalarGridSpec(
            num_scalar_prefetch=2,
            grid=(d // tn2, nk),
            in_specs=[pl.BlockSpec((tm, f), lambda j, k, te_, nv_: (rowi(k, nv_), 0)),
                      pl.BlockSpec((None, f, tn2), lambda j, k, te_, nv_: (te_[rowi(k, nv_)], 0, j))],
            out_specs=pl.BlockSpec((tm, tn2), lambda j, k, te_, nv_: (k, j)),
        ),
        out_shape=jax.ShapeDtypeStruct((p, d), BF16),
        compiler_params=_cparams(("arbitrary", "arbitrary")),
        name="moe_down",
    )(te, nv, hid, w2)


def _moe_combine_body(seg_ref, base_ref, x_ref, rank_ref, comb_ref, y_ref, o_ref, buf, sem, *, tt):
    i = pl.program_id(0)
    rank = rank_ref[...]
    comb = comb_ref[...]
    cc = _iota((tt, MOE_BLK), 1)

    def copy(slot, row0):
        return pltpu.make_async_copy(y_ref.at[pl.ds(pl.multiple_of(row0, MOE_ALIGN), MOE_BLK)], buf.at[slot], sem.at[slot])

    def gathered(e, b, slot):
        hit = (rank[:, e:e + 1] == (cc + b * MOE_BLK).astype(F32)) & (comb[:, e:e + 1] > 0.0)
        return comb[:, e:e + 1] * _dot(jnp.where(hit, 1.0, 0.0).astype(BF16), buf[slot])

    o_ref[...] = x_ref[...]
    copy(0, base_ref[i * N_EXPERTS]).start()
    for e in range(N_EXPERTS):
        slot = e % 2
        seg_e = seg_ref[i * N_EXPERTS + e]
        base_e = base_ref[i * N_EXPERTS + e]
        if e + 1 < N_EXPERTS:
            copy(1 - slot, base_ref[i * N_EXPERTS + e + 1]).start()
        copy(slot, 0).wait()
        o_ref[...] += gathered(e, 0, slot)

        def extra(b, carry, e=e, slot=slot, base_e=base_e):
            copy(slot, base_e + b * MOE_BLK).start()
            copy(slot, 0).wait()
            o_ref[...] += gathered(e, b, slot)
            return carry

        lax.fori_loop(1, (seg_e + MOE_BLK - 1) // MOE_BLK, extra, 0)


def _moe_combine(x, rank, comb, y, seg, base, tt):
    m, d = x.shape
    return pl.pallas_call(
        functools.partial(_moe_combine_body, tt=tt),
        grid_spec=pltpu.PrefetchScalarGridSpec(
            num_scalar_prefetch=2,
            grid=(m // tt,),
            in_specs=[pl.BlockSpec((tt, d), lambda i, *_: (i, 0)), pl.BlockSpec((tt, LANES), lambda i, *_: (i, 0)),
                      pl.BlockSpec((tt, LANES), lambda i, *_: (i, 0)), pl.BlockSpec(memory_space=pl.ANY)],
            out_specs=pl.BlockSpec((tt, d), lambda i, *_: (i, 0)),
            scratch_shapes=[pltpu.VMEM((2, MOE_BLK, d), BF16), pltpu.SemaphoreType.DMA((2,))],
        ),
        out_shape=jax.ShapeDtypeStruct((m, d), F32),
        compiler_params=_cparams(("arbitrary",)),
        name="moe_combine",
    )(seg, base, x, rank, comb, y)


def _moe_sparse(x, hn, comb, w1, w3, w2):
    m, d = x.shape
    ne = N_EXPERTS
    tt = _pick(m, (512, 256, 128, 64, 32, 16))
    nt = m // tt
    tm = MOE_TM
    rank, cnt = _moe_rank(comb, tt)
    cnt = cnt[:, 0, :ne].astype(jnp.int32)
    seg = (cnt + MOE_ALIGN - 1) // MOE_ALIGN * MOE_ALIGN
    grp = (jnp.sum(seg, axis=0) + MOE_BLK + tm - 1) // tm * tm
    ends = jnp.cumsum(grp)
    base = (ends - grp)[None, :] + jnp.cumsum(seg, axis=0) - seg
    p_max = (2 * m + nt * ne * MOE_ALIGN + ne * (MOE_BLK + tm) + tm - 1) // tm * tm
    nk = p_max // tm
    te = jnp.minimum(jnp.sum(ends[None, :] <= (jnp.arange(nk, dtype=jnp.int32) * tm)[:, None], axis=1), ne - 1).astype(jnp.int32)
    nv = (ends[-1] // tm).astype(jnp.int32).reshape(1)
    seg = seg.reshape(-1)
    base = base.astype(jnp.int32).reshape(-1)
    xg = _moe_dispatch(hn, rank, comb, seg, base, p_max, tt)
    y = _moe_grouped(xg, te, nv, w1, w3, w2)
    return _moe_combine(x, rank, comb, y, seg, base, tt)


def _rwkv_body(zr_ref, zk_ref, zv_ref, zdw_ref, zda_ref, zdg_ref, mu_ref, sh_ref, s0_ref,
               w0_ref, wup_ref, a0_ref, aup_ref, gup_ref, kk_ref, ka_ref, rk_ref, lw_ref, lb_ref,
               y_ref, sT_ref, s_scr, carry_scr, *, bb, rt, chunk, valid, unroll):
    n = pl.program_id(2)
    n_last = pl.num_programs(2) - 1
    L = chunk
    L2 = 2 * L
    H = RWKV_HEAD
    nchunk = rt // L
    nsq = max(1, int(math.ceil(math.log2(L))))

    lane = _iota((1, LANES), 1)
    m0 = (lane < H).astype(F32)
    m1 = 1.0 - m0
    r2 = _iota((L2, L2), 0)
    c2 = _iota((L2, L2), 1)
    rh = jnp.where(r2 >= L, 1, 0)
    ch = jnp.where(c2 >= L, 1, 0)
    same = jnp.where(rh == ch, 1.0, 0.0)
    tdiff = (r2 - L * rh) - (c2 - L * ch)
    strict = same * jnp.where(tdiff > 0, 1.0, 0.0)
    incl = same * jnp.where(tdiff >= 0, 1.0, 0.0)
    rl = _iota((L, L), 0)
    cl = _iota((L, L), 1)
    tril = jnp.where(rl >= cl, 1.0, 0.0).astype(BF16)
    ri = _iota((LANES, LANES), 0)
    ci = _iota((LANES, LANES), 1)
    bd = jnp.where((ri >= H) == (ci >= H), 1.0, 0.0)
    bones = bd.astype(BF16)
    row = _iota((L, 1), 0)

    @pl.when(n == 0)
    def _():
        s_scr[...] = jnp.zeros_like(s_scr)
        carry_scr[...] = jnp.zeros_like(carry_scr)

    mu = mu_ref[...]
    w0 = w0_ref[...]
    a0 = a0_ref[...]
    kkw = kk_ref[...]
    kaw = ka_ref[...]
    rkw = rk_ref[...]
    lnw = lw_ref[...]
    lnb = lb_ref[...]

    def block_sum(x):
        return _dot_const_rhs(x, bones)

    def load(seq, r0, is_start):
        zc = jnp.concatenate([ref[seq, pl.ds(r0, L), :] for ref in (zr_ref, zk_ref, zv_ref, zdw_ref, zda_ref, zdg_ref)],
                             axis=1)
        first = jnp.where(is_start, sh_ref[pl.ds(seq, 1), :], carry_scr[seq, 0:1, :])
        s0 = s0_ref[seq, 0]
        s1 = s0_ref[seq, 1]
        zz = jnp.zeros((H, H), F32)
        s_init = jnp.concatenate([jnp.concatenate([s0, zz], axis=1), jnp.concatenate([zz, s1], axis=1)], axis=0)
        st = jnp.where(is_start, s_init, s_scr[seq])
        return zc, first, st

    def recur(seqs):
        pre = []
        for r, k, v, ka, kb, lw, cum, st in seqs:
            ce = jnp.exp(cum)
            cinv = jnp.exp(-cum)
            at = jnp.exp(cum - lw) * ka
            rt_ = ce * r
            bt = kb * cinv
            kt = k * cinv
            c_last = ce[L - 1:L, :]
            lhs = jnp.concatenate([at * m0, at * m1, rt_ * m0, rt_ * m1], axis=0)
            rhs = jnp.concatenate([bt * m0, bt * m1, kt * m0, kt * m1, st], axis=0)
            bk = jnp.concatenate([bt * c_last, kt * c_last], axis=0)
            pre.append((lhs, rhs, bk, c_last))
        gms = [_dot_x(p_[0], p_[1], NT) for p_ in pre]
        wvs = [_dot_x(jnp.concatenate([gm[0:L2, L2:2 * L2] * strict, gm[L2:, L2:2 * L2] * incl], axis=0),
                      jnp.concatenate([s[2], s[2]], axis=0)) for gm, s in zip(gms, seqs)]
        xs = [gm[0:L2, 2 * L2:] + wv[0:L2] for gm, wv in zip(gms, wvs)]
        ps = [gm[0:L2, 0:L2] * strict for gm in gms]
        for q in range(nsq):
            if q + 1 < nsq:
                xps = [_dot_x(p, jnp.concatenate([x, p], axis=1)) for x, p in zip(xs, ps)]
                xs = [x + xp[:, 0:LANES] for x, xp in zip(xs, xps)]
                ps = [xp[:, LANES:] for xp in xps]
            else:
                xs = [x + _dot_x(p, x) for x, p in zip(xs, ps)]
        us = [x[0:L, :] * m0 + x[L:L2, :] * m1 for x in xs]
        rus = [_dot_x(gm[L2:, 0:L2] * incl, jnp.concatenate([u, u], axis=0)) for gm, u in zip(gms, us)]
        sus = [_dot_x(jnp.concatenate([u, s[2]], axis=0), p_[2], TN) for u, s, p_ in zip(us, seqs, pre)]
        out = []
        for gm, wv, ru, su, s, p_ in zip(gms, wvs, rus, sus, seqs, pre):
            y2 = gm[L2:, 2 * L2:] + wv[L2:] + ru
            out.append((y2[0:L, :] * m0 + y2[L:L2, :] * m1, s[7] * p_[3] + bd * su))
        return out

    def compute_all(loaded):
        nu = len(loaded)
        stack = lambda xs: xs[0] if nu == 1 else jnp.concatenate(xs, axis=0)
        part = lambda x, u: x[u * L:(u + 1) * L]
        zc = stack([x[0] for x in loaded])
        zprev = stack([jnp.where(row == 0, x[1], pltpu.roll(x[0], 1, 0)) for x in loaded])
        zs = zc + mu * (zprev - zc)
        r = zs[:, 0:LANES]
        k = zs[:, LANES:2 * LANES]
        v = zs[:, 2 * LANES:3 * LANES]
        dw = zs[:, 3 * LANES:4 * LANES]
        da = zs[:, 4 * LANES:5 * LANES]
        dg = zs[:, 5 * LANES:]

        w_log = -_softplus(-(w0 + _dot_hp(jnp.tanh(dw), wup_ref[...]))) - 0.5
        lw = -jnp.exp(w_log)
        a = _sigmoid(a0 + _dot_hp(da, aup_ref[...]))
        g = _dot_hp(_sigmoid(dg), gup_ref[...])
        kk = k * kkw
        k = k * (1.0 + (a - 1.0) * kaw)
        rows = nu * L
        sums = block_sum(jnp.concatenate([kk * kk, r * k * rkw], axis=0))
        kk = kk / jnp.maximum(jnp.sqrt(sums[0:rows]), 1e-12)
        bonus = sums[rows:] * v
        ka = -kk
        kb = kk * a
        if valid < L:
            ok = stack([row < valid] * nu)
            lw = jnp.where(ok, lw, 0.0)
            r = jnp.where(ok, r, 0.0)
            k = jnp.where(ok, k, 0.0)
            v = jnp.where(ok, v, 0.0)
            ka = jnp.where(ok, ka, 0.0)
            kb = jnp.where(ok, kb, 0.0)

        lw_wide = lw if nu == 1 else jnp.concatenate([part(lw, u) for u in range(nu)], axis=1)
        cum_wide = _dot_const(tril, lw_wide)
        res = recur([(part(r, u), part(k, u), part(v, u), part(ka, u), part(kb, u), part(lw, u),
                      cum_wide[:, u * LANES:(u + 1) * LANES], loaded[u][2]) for u in range(nu)])
        sts = [x[1] for x in res]
        y = stack([x[0] for x in res])
        mean = block_sum(y) * (1.0 / H)
        yc = y - mean
        var = block_sum(yc * yc) * (1.0 / H)
        yn = yc * lax.rsqrt(var + RWKV_LNX_EPS) * lnw + lnb
        out = (yn + bonus) * g
        return [(part(out, u).astype(y_ref.dtype), sts[u], loaded[u][0][valid - 1:valid, :]) for u in range(nu)]

    def store(seq, r0, is_end, y, st, last_row):
        y_ref[seq, pl.ds(r0, L), :] = y
        s_scr[seq] = st
        carry_scr[seq, 0:1, :] = last_row

        @pl.when(is_end)
        def _():
            sT_ref[seq, 0] = st[0:H, 0:H]
            sT_ref[seq, 1] = st[H:2 * H, H:2 * H]

    def step(it, carry):
        sg = it // nchunk
        ci_ = it - sg * nchunk
        r0 = pl.multiple_of(ci_ * L, L)
        is_start = jnp.logical_and(n == 0, ci_ == 0)
        is_end = jnp.logical_and(n == n_last, ci_ == nchunk - 1)
        seqs = [sg * unroll + u for u in range(unroll)]
        loaded = [load(s, r0, is_start) for s in seqs]
        done = compute_all(loaded)
        for s, d in zip(seqs, done):
            store(s, r0, is_end, *d)
        return carry

    lax.fori_loop(0, (bb // unroll) * nchunk, step, 0)


def _dot_x(a, b, dims=NN):
    return _dot(a.astype(BF16), b.astype(BF16), dims)


def _dot_const_rhs(a, c_bf16):
    h0 = a.astype(BF16)
    r1 = a - h0.astype(F32)
    h1 = r1.astype(BF16)
    h2 = (r1 - h1.astype(F32)).astype(BF16)
    return _dot(h0, c_bf16) + (_dot(h1, c_bf16) + _dot(h2, c_bf16))


def _rwkv(z, zt, shift_p, wkv0, mu_p, w0, wup_p, a0, aup_p, g_up, k_k, k_a, r_k, lnx_w, lnx_b,
          *, n_seq, seq_rows, chunk, valid):
    unroll = _pick(n_seq, (4, 2, 1))
    if seq_rows > chunk:
        rt = _pick(seq_rows, (256, 128, 64))
        bb = unroll
    else:
        rt = seq_rows
        bb = _pick(n_seq, (32, 16, 8, 4, 2, 1))
    assert rt % chunk == 0 and bb % unroll == 0 and (bb == n_seq or bb % SUBLANES == 0)
    sblk = lambda c, g, n: (g, c, 0, 0)
    npair = RWKV_HEADS // 2
    nrb = LANES

    def cat(a):
        rows = a.shape[0]
        rkv = a[:, :3 * MIX_A].reshape(rows, 3, npair, LANES).transpose(2, 0, 1, 3).reshape(npair, rows, 3 * LANES)
        tail = jnp.broadcast_to(a[None, :, 3 * MIX_A:], (npair, rows, ZE_U - 3 * MIX_A))
        return jnp.concatenate([rkv, tail], axis=2)

    mu_cat = cat(mu_p.reshape(1, ZE_U))
    sh_cat = cat(shift_p)
    row2 = lambda a: a.reshape(1, MIX_A)
    vec = pl.BlockSpec((1, LANES), lambda c, g, n: (0, c))
    kern = functools.partial(_rwkv_body, bb=bb, rt=rt, chunk=chunk, valid=valid, unroll=unroll)
    y, s_t = pl.pallas_call(
        kern,
        grid=(npair, n_seq // bb, seq_rows // rt),
        in_specs=[
            pl.BlockSpec((bb, rt, nrb), lambda c, g, n: (g, n, c)),
            pl.BlockSpec((bb, rt, nrb), lambda c, g, n: (g, n, npair + c)),
            pl.BlockSpec((bb, rt, nrb), lambda c, g, n: (g, n, 2 * npair + c)),
            pl.BlockSpec((bb, rt, RANK_PAD), lambda c, g, n: (g, n, ZT_DW // RANK_PAD)),
            pl.BlockSpec((bb, rt, RANK_PAD), lambda c, g, n: (g, n, ZT_DA // RANK_PAD)),
            pl.BlockSpec((bb, rt, RWKV_GATE_RANK), lambda c, g, n: (g, n, ZT_DG // RWKV_GATE_RANK)),
            pl.BlockSpec((None, 1, RW_CAT), lambda c, g, n: (c, 0, 0)),
            pl.BlockSpec((None, bb, RW_CAT), lambda c, g, n: (c, g, 0)),
            pl.BlockSpec((bb, 2, RWKV_HEAD, RWKV_HEAD), sblk),
            vec,
            pl.BlockSpec((RANK_PAD, LANES), lambda c, g, n: (0, c)),
            vec,
            pl.BlockSpec((RANK_PAD, LANES), lambda c, g, n: (0, c)),
            pl.BlockSpec((RWKV_GATE_RANK, LANES), lambda c, g, n: (0, c)),
            vec, vec, vec, vec, vec,
        ],
        out_specs=[pl.BlockSpec((bb, rt, LANES), lambda c, g, n: (g, n, c)),
                   pl.BlockSpec((bb, 2, RWKV_HEAD, RWKV_HEAD), sblk)],
        out_shape=[jax.ShapeDtypeStruct((n_seq, seq_rows, MIX_A), BF16),
                   jax.ShapeDtypeStruct((n_seq, RWKV_HEADS, RWKV_HEAD, RWKV_HEAD), F32)],
        scratch_shapes=[pltpu.VMEM((bb, LANES, LANES), F32), pltpu.VMEM((bb, SUBLANES, RW_CAT), F32)],
        compiler_params=_cparams(("parallel", "parallel", "arbitrary")),
        name="rwkv7",
    )(z, z, z, zt, zt, zt, mu_cat, sh_cat, wkv0,
      row2(w0), wup_p, row2(a0), aup_p, g_up, row2(k_k), row2(k_a), row2(r_k), row2(lnx_w), row2(lnx_b))
    return y, s_t


def _rwkv_cat(a):
    npair = RWKV_HEADS // 2
    rows = a.shape[0]
    rkv = a[:, :3 * MIX_A].reshape(rows, 3, npair, LANES).transpose(2, 0, 1, 3).reshape(npair, rows, 3 * LANES)
    tail = jnp.broadcast_to(a[None, :, 3 * MIX_A:], (npair, rows, ZE_U - 3 * MIX_A))
    return jnp.concatenate([rkv, tail], axis=2)


def _rwkv_step_body(zr_ref, zk_ref, zv_ref, zdw_ref, zda_ref, zdg_ref, mu_ref, sh_ref, s0_ref,
                    w0_ref, wup_ref, a0_ref, aup_ref, gup_ref, kk_ref, ka_ref, rk_ref, lw_ref, lb_ref,
                    y_ref, sT_ref, op_scr, yt_scr, *, steps, nb):
    H = RWKV_HEAD
    rows = steps * nb
    ri = _iota((LANES, LANES), 0)
    ci = _iota((LANES, LANES), 1)
    bones = jnp.where((ri >= H) == (ci >= H), 1.0, 0.0).astype(BF16)
    block_sum = lambda x: _dot_const_rhs(x, bones)

    zc = jnp.concatenate([ref[...].reshape(rows, ref.shape[2]) for ref in (zr_ref, zk_ref, zv_ref, zdw_ref, zda_ref, zdg_ref)],
                         axis=1)
    zprev = jnp.concatenate([sh_ref[...], zc[0:rows - nb]], axis=0) if steps > 1 else sh_ref[...]
    zs = zc + mu_ref[...] * (zprev - zc)
    r = zs[:, 0:LANES]
    k = zs[:, LANES:2 * LANES]
    v = zs[:, 2 * LANES:3 * LANES]
    dw = zs[:, 3 * LANES:4 * LANES]
    da = zs[:, 4 * LANES:5 * LANES]
    dg = zs[:, 5 * LANES:]
    w_log = -_softplus(-(w0_ref[...] + _dot_hp(jnp.tanh(dw), wup_ref[...]))) - 0.5
    decay = jnp.exp(-jnp.exp(w_log))
    a = _sigmoid(a0_ref[...] + _dot_hp(da, aup_ref[...]))
    g = _dot_hp(_sigmoid(dg), gup_ref[...])
    kk = k * kk_ref[...]
    k = k * (1.0 + (a - 1.0) * ka_ref[...])
    sums = block_sum(jnp.concatenate([kk * kk, r * k * rk_ref[...]], axis=0))
    kk = kk / jnp.maximum(jnp.sqrt(sums[0:rows]), 1e-12)
    bonus = sums[rows:] * v

    for qi, x in enumerate((decay, -kk, kk * a, k, r, v)):
        for t in range(steps):
            op_scr[qi, t] = x[t * nb:(t + 1) * nb, :].T

    for t in range(steps):
        src = s0_ref if t == 0 else sT_ref
        for hh in range(2):
            lo = hh * H
            w_t = op_scr[0, t, lo:lo + H, :]
            ka_t = op_scr[1, t, lo:lo + H, :]
            kb_t = op_scr[2, t, lo:lo + H, :]
            k_t = op_scr[3, t, lo:lo + H, :]
            r_t = op_scr[4, t, lo:lo + H, :]

            def value_row(i, carry, src=src, hh=hh, lo=lo, t=t, w_t=w_t, ka_t=ka_t, kb_t=kb_t, k_t=k_t, r_t=r_t):
                s_i = src[hh, i]
                sa = jnp.sum(s_i * ka_t, axis=0, keepdims=True)
                v_i = op_scr[5, t, pl.ds(lo + i, 1), :]
                s_n = s_i * w_t + sa * kb_t + v_i * k_t
                sT_ref[hh, i] = s_n
                yt_scr[t, pl.ds(lo + i, 1), :] = jnp.sum(s_n * r_t, axis=0, keepdims=True)
                return carry

            lax.fori_loop(0, H, value_row, 0, unroll=4)

    y = jnp.concatenate([yt_scr[t].T for t in range(steps)], axis=0)
    mean = block_sum(y) * (1.0 / H)
    yc = y - mean
    var = block_sum(yc * yc) * (1.0 / H)
    yn = yc * lax.rsqrt(var + RWKV_LNX_EPS) * lw_ref[...] + lb_ref[...]
    y_ref[...] = ((yn + bonus) * g).reshape(steps, nb, LANES).astype(y_ref.dtype)


def _rwkv_step(z_tb, zt_tb, shift_p, wkv0_t, mu_p, w0, wup_p, a0, aup_p, g_up, k_k, k_a, r_k, lnx_w, lnx_b):
    steps, nb, _ = z_tb.shape
    assert nb == LANES
    npair = RWKV_HEADS // 2
    row2 = lambda a: a.reshape(1, MIX_A)
    vec = pl.BlockSpec((1, LANES), lambda c: (0, c))
    zspec = lambda width, idx: pl.BlockSpec((steps, nb, width), lambda c: (0, 0, idx(c)))
    sspec = pl.BlockSpec((2, RWKV_HEAD, RWKV_HEAD, nb), lambda c: (c, 0, 0, 0))
    kern = functools.partial(_rwkv_step_body, steps=steps, nb=nb)
    return pl.pallas_call(
        kern,
        grid=(npair,),
        in_specs=[
            zspec(LANES, lambda c: c), zspec(LANES, lambda c: npair + c), zspec(LANES, lambda c: 2 * npair + c),
            zspec(RANK_PAD, lambda c: ZT_DW // RANK_PAD), zspec(RANK_PAD, lambda c: ZT_DA // RANK_PAD),
            zspec(RWKV_GATE_RANK, lambda c: ZT_DG // RWKV_GATE_RANK),
            pl.BlockSpec((None, 1, RW_CAT), lambda c: (c, 0, 0)),
            pl.BlockSpec((None, nb, RW_CAT), lambda c: (c, 0, 0)),
            sspec,
            vec,
            pl.BlockSpec((RANK_PAD, LANES), lambda c: (0, c)),
            vec,
            pl.BlockSpec((RANK_PAD, LANES), lambda c: (0, c)),
            pl.BlockSpec((RWKV_GATE_RANK, LANES), lambda c: (0, c)),
            vec, vec, vec, vec, vec,
        ],
        out_specs=[pl.BlockSpec((steps, nb, LANES), lambda c: (0, 0, c)), sspec],
        out_shape=[jax.ShapeDtypeStruct((steps, nb, MIX_A), BF16),
                   jax.ShapeDtypeStruct((RWKV_HEADS, RWKV_HEAD, RWKV_HEAD, nb), F32)],
        scratch_shapes=[pltpu.VMEM((6, steps, LANES, nb), F32), pltpu.VMEM((steps, LANES, nb), F32)],
        compiler_params=_cparams(("parallel",)),
        name="rwkv7_step",
    )(z_tb, z_tb, z_tb, zt_tb, zt_tb, zt_tb, _rwkv_cat(mu_p.reshape(1, ZE_U)), _rwkv_cat(shift_p), wkv0_t,
      row2(w0), wup_p, row2(a0), aup_p, g_up, row2(k_k), row2(k_a), row2(r_k), row2(lnx_w), row2(lnx_b))


def _s5_prep_body(are_ref, aim_ref, ls_ref, abre_ref, abim_ref, zr_ref, zi_ref):
    lam_re = are_ref[...]
    lam_im = aim_ref[...]
    dt = jnp.exp(ls_ref[...])
    mag = jnp.exp(lam_re * dt)
    ab_re = mag * jnp.cos(lam_im * dt)
    ab_im = mag * jnp.sin(lam_im * dt)
    inv = 1.0 / (lam_re * lam_re + lam_im * lam_im)
    abre_ref[...] = ab_re
    abim_ref[...] = ab_im
    zr_ref[...] = ((ab_re - 1.0) * lam_re + ab_im * lam_im) * inv
    zi_ref[...] = (ab_im * lam_re - (ab_re - 1.0) * lam_im) * inv


def _s5_prep(a_re, a_im, log_step):
    g, n = a_re.shape
    sd = jax.ShapeDtypeStruct((g, n), F32)
    return pl.pallas_call(_s5_prep_body, out_shape=[sd, sd, sd, sd], name="s5_discretise")(
        a_re, a_im, jnp.broadcast_to(log_step.reshape(g, 1), (g, n)))


def _gelu_tanh(x):
    return 0.5 * x * (1.0 + jnp.tanh(math.sqrt(2.0 / math.pi) * (x + 0.044715 * (x * x * x))))


def _s5_body(u_ref, wbr_ref, wbi_ref, wcr_ref, wci_ref, abr_ref, abi_ref, d_ref, h0r_ref, h0i_ref,
             y_ref, hTr_ref, hTi_ref, hr_scr, hi_scr, pr_scr, pi_scr, cr_scr, ci_scr, u_st, y_st, *, rt, seq_rows):
    n = pl.program_id(1)
    long_seq = seq_rows >= rt
    grp = SUBLANES if long_seq else rt // seq_rows
    steps = rt // grp
    ar1 = abr_ref[...]
    ai1 = abi_ref[...]
    ar = jnp.broadcast_to(ar1, (grp, ar1.shape[1]))
    ai = jnp.broadcast_to(ai1, (grp, ai1.shape[1]))
    nhalf = u_ref.shape[1] // LANES
    for hf in range(nhalf):
        u_st[hf] = u_ref[:, hf * LANES:(hf + 1) * LANES]
    u = jnp.concatenate(
        [jnp.concatenate([u_st[hf, pl.ds(j, grp, stride=steps), :] for hf in range(nhalf)], axis=1) for j in range(steps)],
        axis=0)
    ub = u.astype(BF16)
    hr_scr[...] = _dot(ub, wbr_ref[...].astype(BF16))
    hi_scr[...] = _dot(ub, wbi_ref[...].astype(BF16))

    def rows(j):
        return pl.ds(j * grp, grp) if isinstance(j, int) else pl.ds(pl.multiple_of(j * grp, grp), grp)

    def scan_step(j, c):
        hr, hi = c
        nr = ar * hr - ai * hi + hr_scr[rows(j), :]
        ni = ar * hi + ai * hr + hi_scr[rows(j), :]
        hr_scr[rows(j), :] = nr
        hi_scr[rows(j), :] = ni
        return nr, ni

    if long_seq:
        @pl.when(n == 0)
        def _():
            def pw_step(j, c):
                qr, qi = c
                pr_scr[pl.ds(j, 1), :] = qr
                pi_scr[pl.ds(j, 1), :] = qi
                return qr * ar1 - qi * ai1, qr * ai1 + qi * ar1
            lax.fori_loop(0, steps, pw_step, (ar1, ai1))
            cr_scr[...] = jnp.zeros_like(cr_scr)
            ci_scr[...] = jnp.zeros_like(ci_scr)

        b = (n * rt) // seq_rows
        is_start = ((n * rt) % seq_rows) == 0
        h_in_r = jnp.where(is_start, h0r_ref[pl.ds(b, 1), :], cr_scr[0:1, :])
        h_in_i = jnp.where(is_start, h0i_ref[pl.ds(b, 1), :], ci_scr[0:1, :])
        zero = jnp.zeros((grp, hr_scr.shape[1]), F32)
        er, ei = lax.fori_loop(0, steps, scan_step, (zero, zero), unroll=4)
        pS_r = pr_scr[steps - 1:steps, :]
        pS_i = pi_scr[steps - 1:steps, :]
        cr_rows, ci_rows = [h_in_r], [h_in_i]
        for s in range(grp):
            pr_, pi_ = cr_rows[-1], ci_rows[-1]
            cr_rows.append(er[s:s + 1, :] + pS_r * pr_ - pS_i * pi_)
            ci_rows.append(ei[s:s + 1, :] + pS_r * pi_ + pS_i * pr_)
        c_r = jnp.concatenate(cr_rows[:grp], axis=0)
        c_i = jnp.concatenate(ci_rows[:grp], axis=0)
        cr_scr[0:1, :] = cr_rows[grp]
        ci_scr[0:1, :] = ci_rows[grp]
        hTr_ref[pl.ds(b, 1), :] = cr_rows[grp]
        hTi_ref[pl.ds(b, 1), :] = ci_rows[grp]

        def fix_step(j, carry):
            qr = pr_scr[pl.ds(j, 1), :]
            qi = pi_scr[pl.ds(j, 1), :]
            hr_scr[rows(j), :] = hr_scr[rows(j), :] + (qr * c_r - qi * c_i)
            hi_scr[rows(j), :] = hi_scr[rows(j), :] + (qr * c_i + qi * c_r)
            return carry

        lax.fori_loop(0, steps, fix_step, 0, unroll=4)
    else:
        hr, hi = h0r_ref[...], h0i_ref[...]
        for t in range(steps):
            hr, hi = scan_step(t, (hr, hi))
        hTr_ref[...] = hr
        hTi_ref[...] = hi

    y = _dot(hr_scr[...].astype(BF16), wcr_ref[...].astype(BF16)) + _dot(hi_scr[...].astype(BF16), wci_ref[...].astype(BF16))
    y = _gelu_tanh(y + d_ref[...] * u)
    for hf in range(nhalf):
        y_st[hf] = y[:, hf * LANES:(hf + 1) * LANES]
    for hf in range(nhalf):
        for s in range(grp):
            y_ref[s * steps:(s + 1) * steps, hf * LANES:(hf + 1) * LANES] = y_st[hf, pl.ds(s, steps, stride=grp), :]


def _s5_scan(z, col0, wbr, wbi, wcr, wci, ab_re, ab_im, d_skip, h0_re, h0_im, *, n_seq, seq_rows):
    m = z.shape[0]
    st = S5_SLAB_STATE
    if seq_rows >= 8 * SUBLANES:
        rt = _pick(seq_rows, (512, 256, 128, 64))
        hspec = pl.BlockSpec((n_seq, st), lambda s, n: (0, s))
        pw_rows = rt // SUBLANES
    else:
        rt = seq_rows * _pick(n_seq, (128, 64, 32, 16, 8))
        hspec = pl.BlockSpec((rt // seq_rows, st), lambda s, n: (n, s))
        pw_rows = SUBLANES
    win = S5_SLAB_IN
    cb0 = col0 // win
    kern = functools.partial(_s5_body, rt=rt, seq_rows=seq_rows)
    wspec_b = pl.BlockSpec((None, win, st), lambda s, n: (s, 0, 0))
    wspec_c = pl.BlockSpec((None, st, win), lambda s, n: (s, 0, 0))
    vspec = pl.BlockSpec((1, st), lambda s, n: (0, s))
    return pl.pallas_call(
        kern,
        grid=(S5_SLABS, m // rt),
        in_specs=[pl.BlockSpec((rt, win), lambda s, n: (n, cb0 + s)), wspec_b, wspec_b, wspec_c, wspec_c,
                  vspec, vspec, pl.BlockSpec((1, win), lambda s, n: (0, s)), hspec, hspec],
        out_specs=[pl.BlockSpec((rt, win), lambda s, n: (n, s)), hspec, hspec],
        out_shape=[jax.ShapeDtypeStruct((m, MIX_B), F32),
                   jax.ShapeDtypeStruct((n_seq, S5_GROUPS * S5_STATE), F32),
                   jax.ShapeDtypeStruct((n_seq, S5_GROUPS * S5_STATE), F32)],
        scratch_shapes=[pltpu.VMEM((rt, st), F32), pltpu.VMEM((rt, st), F32),
                        pltpu.VMEM((pw_rows, st), F32), pltpu.VMEM((pw_rows, st), F32),
                        pltpu.VMEM((SUBLANES, st), F32), pltpu.VMEM((SUBLANES, st), F32),
                        pltpu.VMEM((win // LANES, rt, LANES), F32), pltpu.VMEM((win // LANES, rt, LANES), F32)],
        compiler_params=_cparams(("parallel", "arbitrary")),
        name="s5_scan",
    )(z, wbr, wbi, wcr, wci, ab_re.reshape(1, -1), ab_im.reshape(1, -1), d_skip.reshape(1, -1), h0_re, h0_im)


def _glu_body(a_ref, w_ref, b_ref, y_ref, o_ref):
    t = _dot(a_ref[...].astype(BF16), w_ref[...].astype(BF16)) + b_ref[...]
    o_ref[...] = (y_ref[...] * _sigmoid(t)).astype(o_ref.dtype)


def _glu(y, w, b):
    m, k = y.shape
    n = w.shape[1]
    tm = _pick(m, (1024, 512, 256, 128, 64, 32, 16, 8))
    tn = _pick(n, (512, 256, 128))
    return pl.pallas_call(
        _glu_body,
        grid=(m // tm, n // tn),
        in_specs=[pl.BlockSpec((tm, k), lambda i, j: (i, 0)), pl.BlockSpec((k, tn), lambda i, j: (0, j)),
                  pl.BlockSpec((1, tn), lambda i, j: (0, j)), pl.BlockSpec((tm, tn), lambda i, j: (i, j))],
        out_specs=pl.BlockSpec((tm, tn), lambda i, j: (i, j)),
        out_shape=jax.ShapeDtypeStruct((m, n), BF16),
        compiler_params=_cparams(("parallel", "arbitrary")),
        name="s5_glu",
    )(y, w, b.reshape(1, n), y)


def _mlstm_body(q_ref, k_ref, v_ref, og_ref, gt_ref, gb_ref, nw_ref, c0_ref, n0_ref, m0_ref,
                y_ref, cT_ref, nT_ref, mT_ref, c_scr, n_scr, m_scr, *, bb, rt, chunk, valid, unroll):
    h = pl.program_id(0)
    n = pl.program_id(2)
    n_last = pl.num_programs(2) - 1
    L = chunk
    nchunk = rt // L
    rl = _iota((L, L), 0)
    cl = _iota((L, L), 1)
    causal = rl >= cl
    tril = jnp.where(causal, 1.0, 0.0).astype(BF16)
    row = _iota((L, 1), 0)
    lane = _iota((L, LANES), 1)
    gbias = gb_ref[...]
    nw = nw_ref[...]
    scale = MLSTM_DQK ** -0.5

    def load(seq, r0, is_start):
        c_old = jnp.where(is_start, c0_ref[seq, 0], c_scr[seq])
        n_old = jnp.where(is_start, n0_ref[seq, 0], n_scr[seq, 0:1, :])
        m_prev = jnp.where(is_start, m0_ref[seq, 0], m_scr[seq, 0:1, 0:1])
        rows = pl.ds(r0, L)
        return (q_ref[seq, rows, :], k_ref[seq, rows, :], v_ref[seq, rows, :], og_ref[seq, rows, :],
                gt_ref[seq, rows, :], c_old, n_old, m_prev)

    def compute_all(loaded):
        nu = len(loaded)
        igs, lfs = [], []
        for q, k, v, og, gt, c_old, n_old, m_prev in loaded:
            gt = gt + gbias
            ig_raw = jnp.sum(jnp.where(lane == h, gt, 0.0), axis=-1, keepdims=True)
            fg_raw = jnp.sum(jnp.where(lane == MLSTM_HEADS + h, gt, 0.0), axis=-1, keepdims=True)
            ig = GATE_CAP * jnp.tanh(ig_raw / GATE_CAP)
            lf = -_softplus(-(GATE_CAP * jnp.tanh(fg_raw / GATE_CAP)))
            if valid < L:
                ok = row < valid
                ig = jnp.where(ok, ig, NEG_BIG)
                lf = jnp.where(ok, lf, 0.0)
            igs.append(ig)
            lfs.append(lf)
        lf_mat = jnp.zeros((L, LANES), F32)
        for u in range(nu):
            lf_mat = jnp.where(lane == u, lfs[u], lf_mat)
        b_mat = _dot_const(tril, lf_mat)
        pack = b_mat
        for u in range(nu):
            pack = jnp.where(lane == nu + u, igs[u], pack)
        pack_t = pack.T
        gates, qbs, kbs, ks = [], [], [], []
        for u, (q, k, v, og, gt, c_old, n_old, m_prev) in enumerate(loaded):
            bcol = b_mat[:, u:u + 1]
            b_row = pack_t[u:u + 1, :]
            ig_row = pack_t[nu + u:nu + u + 1, :]
            log_d = jnp.where(causal, bcol - b_row + ig_row, NEG_BIG)
            log_p = bcol + m_prev
            m_tok = jnp.maximum(log_p, jnp.max(log_d, axis=-1, keepdims=True))
            gates.append((jnp.exp(log_d - m_tok), jnp.exp(log_p - m_tok), m_tok, bcol))
            k = k * scale
            ks.append(k)
            qbs.append(q.astype(BF16))
            kbs.append(k.astype(BF16))
        ss = [_dot(qb, kb, NT) * g_[0] for qb, kb, g_ in zip(qbs, kbs, gates)]
        svs = [_dot(s.astype(BF16), x[2].astype(BF16)) for s, x in zip(ss, loaded)]
        qcs = [_dot(qb, x[5].astype(BF16), NT) for qb, x in zip(qbs, loaded)]
        upd = []
        for u, (q, k, v, og, gt, c_old, n_old, m_prev) in enumerate(loaded):
            d, p, m_tok, bcol = gates[u]
            m_new = m_tok[L - 1:L, :]
            b_last = bcol[L - 1:L, :]
            w_col = jnp.exp(b_last - bcol + igs[u] - m_new)
            cs = jnp.exp(b_last + m_prev - m_new)
            upd.append((w_col, cs, m_new))
        vks = [_dot((x[2] * w_[0]).astype(BF16), kb, TN) for x, w_, kb in zip(loaded, upd, kbs)]
        out = []
        for u, (q, k, v, og, gt, c_old, n_old, m_prev) in enumerate(loaded):
            d, p, m_tok, bcol = gates[u]
            w_col, cs, m_new = upd[u]
            num = svs[u] + p * qcs[u]
            den = jnp.sum(ss[u], axis=-1, keepdims=True) + p * jnp.sum(q * n_old, axis=-1, keepdims=True)
            hh = num / jnp.maximum(jnp.abs(den), jnp.exp(-m_tok))
            c_new = cs * c_old + vks[u]
            n_new = cs * n_old + jnp.sum(w_col * ks[u], axis=0, keepdims=True)
            hn = hh * lax.rsqrt(jnp.mean(hh * hh, axis=-1, keepdims=True) + RMS_EPS)
            out.append(((hn * nw * _sigmoid(og)).astype(y_ref.dtype), c_new, n_new, m_new))
        return out

    def store(seq, r0, is_end, y, c_new, n_new, m_new):
        y_ref[seq, pl.ds(r0, L), :] = y
        c_scr[seq] = c_new
        n_scr[seq, 0:1, :] = n_new
        m_scr[seq, 0:1, :] = jnp.broadcast_to(m_new, (1, LANES))

        @pl.when(is_end)
        def _():
            cT_ref[seq, 0] = c_new
            nT_ref[seq, 0] = n_new
            mT_ref[seq, 0] = m_new

    def step(it, carry):
        sg = it // nchunk
        ci_ = it - sg * nchunk
        r0 = pl.multiple_of(ci_ * L, L)
        is_start = jnp.logical_and(n == 0, ci_ == 0)
        is_end = jnp.logical_and(n == n_last, ci_ == nchunk - 1)
        seqs = [sg * unroll + u for u in range(unroll)]
        loaded = [load(s, r0, is_start) for s in seqs]
        done = compute_all(loaded)
        for s, d in zip(seqs, done):
            store(s, r0, is_end, *d)
        return carry

    @pl.when(n == 0)
    def _():
        c_scr[...] = jnp.zeros_like(c_scr)
        n_scr[...] = jnp.zeros_like(n_scr)
        m_scr[...] = jnp.zeros_like(m_scr)

    lax.fori_loop(0, (bb // unroll) * nchunk, step, 0)


def _mlstm(z, zt, c0, n0, m0, b_i, b_f, norm_w, *, n_seq, seq_rows, chunk, valid):
    nh, dqk, dv = MLSTM_HEADS, MLSTM_DQK, MLSTM_DV
    unroll = _pick(n_seq, (4, 2, 1))
    if seq_rows > chunk:
        rt = _pick(seq_rows, (256, 128, 64))
        bb = unroll
    else:
        rt = seq_rows
        bb = _pick(n_seq, (8, 4, 2, 1))
    assert rt % chunk == 0 and bb % unroll == 0
    sblk = lambda h, g, n: (g, h, 0, 0)
    gbias = jnp.pad(jnp.concatenate([b_i, b_f]).reshape(1, 2 * nh), ((0, 0), (0, LANES - 2 * nh)))
    n0r = n0.reshape(n_seq, nh, 1, dqk)
    m0r = m0.reshape(n_seq, nh, 1, 1)
    kern = functools.partial(_mlstm_body, bb=bb, rt=rt, chunk=chunk, valid=valid, unroll=unroll)
    y, c_t, n_t, m_t = pl.pallas_call(
        kern,
        grid=(nh, n_seq // bb, seq_rows // rt),
        in_specs=[
            pl.BlockSpec((bb, rt, dqk), lambda h, g, n: (g, n, h)),
            pl.BlockSpec((bb, rt, dqk), lambda h, g, n: (g, n, nh + h)),
            pl.BlockSpec((bb, rt, dv), lambda h, g, n: (g, n, (2 * nh * dqk) // dv + h)),
            pl.BlockSpec((bb, rt, dv), lambda h, g, n: (g, n, h)),
            pl.BlockSpec((bb, rt, LANES), lambda h, g, n: (g, n, (ZO_G - ZO_OG) // LANES)),
            pl.BlockSpec((1, LANES), lambda h, g, n: (0, 0)),
            pl.BlockSpec((1, dv), lambda h, g, n: (0, h)),
            pl.BlockSpec((bb, 1, dv, dqk), sblk),
            pl.BlockSpec((bb, 1, 1, dqk), sblk),
            pl.BlockSpec((bb, 1, 1, 1), sblk),
        ],
        out_specs=[pl.BlockSpec((bb, rt, dv), lambda h, g, n: (g, n, h)),
                   pl.BlockSpec((bb, 1, dv, dqk), sblk),
                   pl.BlockSpec((bb, 1, 1, dqk), sblk),
                   pl.BlockSpec((bb, 1, 1, 1), sblk)],
        out_shape=[jax.ShapeDtypeStruct((n_seq, seq_rows, nh * dv), BF16),
                   jax.ShapeDtypeStruct((n_seq, nh, dv, dqk), F32),
                   jax.ShapeDtypeStruct((n_seq, nh, 1, dqk), F32),
                   jax.ShapeDtypeStruct((n_seq, nh, 1, 1), F32)],
        scratch_shapes=[pltpu.VMEM((bb, dv, dqk), F32), pltpu.VMEM((bb, SUBLANES, dqk), F32),
                        pltpu.VMEM((bb, SUBLANES, LANES), F32)],
        compiler_params=_cparams(("parallel", "parallel", "arbitrary")),
        name="mlstm",
    )(z, z, z, zt, zt, gbias, norm_w.reshape(1, nh * dv), c0, n0r, m0r)
    return y, c_t, n_t.reshape(n_seq, nh, dqk), m_t.reshape(n_seq, nh)


def _pad_cols(a, n):
    return jnp.pad(a, ((0, 0), (0, n - a.shape[1])))


def _pad_rows(a, n):
    return jnp.pad(a, ((0, n - a.shape[0]), (0, 0)))


def _rwkv_cols(a):
    c1 = 3 * MIX_A
    c2 = c1 + RWKV_DECAY_RANK
    c3 = c2 + RWKV_ICL_RANK
    return jnp.concatenate([a[:, :c1], _pad_cols(a[:, c1:c2], RANK_PAD), _pad_cols(a[:, c2:c3], RANK_PAD), a[:, c3:]], axis=1)


def _rwkv_cols_inv(a):
    return jnp.concatenate([a[:, :ZE_DW + RWKV_DECAY_RANK], a[:, ZE_DA:ZE_DA + RWKV_ICL_RANK], a[:, ZE_DG:ZE_U]], axis=1)


def _pad_seq(a, n_seq, t, tp):
    a = a.reshape(n_seq, t, a.shape[1])
    return a if tp == t else jnp.pad(a, ((0, 0), (0, tp - t), (0, 0)))


def _unpad_seq(a, n_seq, t, tp):
    a = a if tp == t else a[:, :t]
    return a.reshape(n_seq * t, a.shape[2])


def _s5_block_weights(zr, zi, b_re, b_im, c_re, c_im):
    g, n, c = S5_GROUPS, S5_STATE, S5_GROUP
    sg = S5_SLAB_GROUPS
    bb_re = zr[..., None] * b_re - zi[..., None] * b_im
    bb_im = zr[..., None] * b_im + zi[..., None] * b_re
    eye = jnp.eye(sg, dtype=F32)

    def in_blocks(bb):
        t = bb.reshape(S5_SLABS, sg, n, c)
        return jnp.einsum('sgnc,gh->sgchn', t, eye).reshape(S5_SLABS, sg * c, sg * n)

    def out_blocks(cc):
        t = cc.reshape(S5_SLABS, sg, c, n)
        return jnp.einsum('sgcn,gh->sgnhc', t, eye).reshape(S5_SLABS, sg * n, sg * c)

    return in_blocks(bb_re), in_blocks(bb_im), out_blocks(c_re), out_blocks(-c_im)


def _trunk(x, n_seq, t, st, w):
    tp = t if t >= MLSTM_CHUNK else SUBLANES * ((t + SUBLANES - 1) // SUBLANES)
    chunk = MLSTM_CHUNK if t >= MLSTM_CHUNK else tp
    valid = chunk if tp == t else t
    assert t % chunk == 0 or tp == chunk

    xn = _rmsnorm(x, w['ln_mix_e'], BF16)
    z, zt = _mm_split(xn, w['w_in_e'], ZE_DW, w['w_in_e_tail'])
    rw = (w['mu'], w['w0'], w['w_up'], w['a0'], w['a_up'], w['g_up'], w['k_k'], w['k_a'], w['r_k'], w['lnx_w'], w['lnx_b'])
    if t < chunk and n_seq == LANES:
        tb = lambda a: a.reshape(n_seq, t, a.shape[1]).transpose(1, 0, 2)
        ya, wkv_t = _rwkv_step(tb(z), tb(zt), _rwkv_cols(st['shift']), st['wkv'].transpose(1, 2, 3, 0), *rw)
        ya = ya.transpose(1, 0, 2).reshape(n_seq * t, MIX_A)
        wkv_t = wkv_t.transpose(3, 0, 1, 2)
    else:
        ya, wkv_t = _rwkv(_pad_seq(z, n_seq, t, tp), _pad_seq(zt, n_seq, t, tp), _rwkv_cols(st['shift']), st['wkv'], *rw,
                          n_seq=n_seq, seq_rows=tp, chunk=chunk, valid=valid)
        ya = _unpad_seq(ya, n_seq, t, tp)
    last = lambda a: a.reshape(n_seq, t, a.shape[1])[:, -1]
    shift_t = _rwkv_cols_inv(jnp.concatenate([last(z), last(zt)[:, :ZT_U]], axis=1))
    yb, re_t, im_t = _s5_scan(zt, ZT_U, w['s5_wbr'], w['s5_wbi'], w['s5_wcr'], w['s5_wci'], w['s5_ab_re'], w['s5_ab_im'],
                              w['s5_d'], st['s5_re'], st['s5_im'], n_seq=n_seq, seq_rows=t)
    yb = _glu(yb, w['s5_glu_w'], w['s5_glu_b'])
    x = _mm2_res(ya, yb, w['w_out_e'], x)
    hn = _rmsnorm(x, w['ln_ffn_e'], BF16)
    hid = _glu_up(hn, w['ffn_w1'], w['ffn_w3'])
    x = _mm_acc(hid, w['ffn_w2'], x)

    xn = _rmsnorm(x, w['ln_mix_o'], BF16)
    zo, zot = _mm_split(xn, w['w_in_o'], ZO_OG, w['w_in_o_tail'])
    yc, c_t, n_t, m_t = _mlstm(_pad_seq(zo, n_seq, t, tp), _pad_seq(zot, n_seq, t, tp), st['c'], st['n'], st['m'], w['b_i'], w['b_f'], w['norm_w'],
                               n_seq=n_seq, seq_rows=tp, chunk=chunk, valid=valid)
    yc = _unpad_seq(yc, n_seq, t, tp)
    x = _mm_res(yc, w['w_out_o'], x)
    hn, comb = _rmsnorm_router(x, w['ln_ffn_o'], w['router_w'], w['router_b'])
    x = _moe_sparse(x, hn, comb, w['exp_w1'], w['exp_w3'], w['exp_w2'])
    y = _rmsnorm(x, w['final_norm'], F32)

    g, ns = S5_GROUPS, S5_STATE
    return (y, wkv_t[None], shift_t[None], re_t.reshape(1, n_seq, g, ns), im_t.reshape(1, n_seq, g, ns),
            c_t[None], n_t[None], m_t[None])


def kernel(x_prompt, x_sample, state_rwkv_wkv, state_rwkv_shift, state_s5_re, state_s5_im, state_mlstm_c, state_mlstm_n, state_mlstm_m, ln_mix_e, w_in_e, rwkv_mu, rwkv_w0, rwkv_w_up, rwkv_a0, rwkv_a_up, rwkv_g_up, rwkv_k_k, rwkv_k_a, rwkv_r_k, rwkv_lnx_w, rwkv_lnx_b, s5_a_re, s5_a_im, s5_log_step, s5_b_re, s5_b_im, s5_c_re, s5_c_im, s5_d, s5_glu_w, s5_glu_b, w_out_e, ln_ffn_e, ffn_w1, ffn_w3, ffn_w2, ln_mix_o, w_in_o, mlstm_b_i, mlstm_b_f, mlstm_norm_w, w_out_o, ln_ffn_o, router_w, router_b, exp_w1, exp_w3, exp_w2, final_norm):
    assert ln_mix_e.shape[0] == 1 and ln_mix_o.shape[0] == 1
    d = D_MODEL
    ab_re, ab_im, zr, zi = _s5_prep(s5_a_re[0], s5_a_im[0], s5_log_step[0])
    wbr, wbi, wcr, wci = _s5_block_weights(zr, zi, s5_b_re[0], s5_b_im[0], s5_c_re[0], s5_c_im[0])
    wo = jnp.swapaxes(w_in_o[0], 0, 1)
    we = jnp.swapaxes(w_in_e[0], 0, 1)
    c_dw = 3 * MIX_A
    c_da = c_dw + RWKV_DECAY_RANK
    c_dg = c_da + RWKV_ICL_RANK
    nh = MLSTM_HEADS
    w = {
        'ln_mix_e': ln_mix_e[0],
        'w_in_e': we,
        'w_in_e_tail': jnp.concatenate([_pad_rows(we[c_dw:c_da], RANK_PAD), _pad_rows(we[c_da:c_dg], RANK_PAD),
                                        we[c_dg:]], axis=0),
        'mu': _rwkv_cols(rwkv_mu[0].reshape(1, RWKV_IN)),
        'w0': rwkv_w0[0], 'a0': rwkv_a0[0],
        'w_up': jnp.pad(rwkv_w_up[0], ((0, RANK_PAD - RWKV_DECAY_RANK), (0, 0))),
        'a_up': jnp.pad(rwkv_a_up[0], ((0, RANK_PAD - RWKV_ICL_RANK), (0, 0))),
        'g_up': rwkv_g_up[0], 'k_k': rwkv_k_k[0], 'k_a': rwkv_k_a[0], 'r_k': rwkv_r_k[0].reshape(MIX_A),
        'lnx_w': rwkv_lnx_w[0], 'lnx_b': rwkv_lnx_b[0],
        's5_wbr': wbr, 's5_wbi': wbi, 's5_wcr': wcr, 's5_wci': wci,
        's5_ab_re': ab_re, 's5_ab_im': ab_im, 's5_d': s5_d[0], 's5_glu_w': s5_glu_w[0], 's5_glu_b': s5_glu_b[0],
        'w_out_e': w_out_e[0], 'ln_ffn_e': ln_ffn_e[0],
        'ffn_w1': ffn_w1[0], 'ffn_w3': ffn_w3[0], 'ffn_w2': ffn_w2[0],
        'ln_mix_o': ln_mix_o[0],
        'w_in_o': wo,
        'w_in_o_tail': jnp.concatenate([wo[ZO_OG + 2 * nh:], _pad_rows(wo[ZO_OG:ZO_OG + 2 * nh], ZO_W - ZO_G)], axis=0),
        'b_i': mlstm_b_i[0], 'b_f': mlstm_b_f[0], 'norm_w': mlstm_norm_w[0],
        'w_out_o': w_out_o[0], 'ln_ffn_o': ln_ffn_o[0],
        'router_w': router_w[0], 'router_b': router_b[0],
        'exp_w1': exp_w1[0], 'exp_w3': exp_w3[0], 'exp_w2': exp_w2[0],
        'final_norm': final_norm,
    }
    bp, tp_, _ = x_prompt.shape
    bs, ts_, _ = x_sample.shape
    g, ns = S5_GROUPS, S5_STATE
    zero = lambda *s: jnp.zeros(s, F32)
    st_p = {'wkv': zero(bp, RWKV_HEADS, RWKV_HEAD, RWKV_HEAD), 'shift': zero(bp, RWKV_IN),
            's5_re': zero(bp, g * ns), 's5_im': zero(bp, g * ns),
            'c': zero(bp, nh, MLSTM_DV, MLSTM_DQK), 'n': zero(bp, nh, MLSTM_DQK), 'm': zero(bp, nh)}
    st_s = {'wkv': state_rwkv_wkv[0], 'shift': state_rwkv_shift[0],
            's5_re': state_s5_re[0].reshape(bs, g * ns), 's5_im': state_s5_im[0].reshape(bs, g * ns),
            'c': state_mlstm_c[0], 'n': state_mlstm_n[0], 'm': state_mlstm_m[0]}
    out_p = _trunk(x_prompt.reshape(bp * tp_, d), bp, tp_, st_p, w)
    out_s = _trunk(x_sample.reshape(bs * ts_, d), bs, ts_, st_s, w)
    y_p = out_p[0].reshape(bp, tp_, d)
    y_s = out_s[0].reshape(bs, ts_, d)
    return (y_p, y_s) + tuple(out_p[1:]) + tuple(out_s[1:])
```

```python
import functools
import math

import jax
import jax.numpy as jnp
from jax import lax
from jax.experimental import pallas as pl
from jax.experimental.pallas import tpu as pltpu

F32 = jnp.float32
BF16 = jnp.bfloat16

D_MODEL = 4096
MIX_A = D_MODEL // 2
MIX_B = D_MODEL - MIX_A
RWKV_HEAD = 64
RWKV_HEADS = MIX_A // RWKV_HEAD
RWKV_DECAY_RANK = 96
RWKV_ICL_RANK = 96
RWKV_GATE_RANK = 256
RWKV_IN = 3 * MIX_A + RWKV_DECAY_RANK + RWKV_ICL_RANK + RWKV_GATE_RANK
RWKV_LNX_EPS = 1e-5 * RWKV_HEAD
S5_GROUP = 16
S5_GROUPS = MIX_B // S5_GROUP
S5_STATE = 64
MLSTM_HEADS = 8
MLSTM_DQK = D_MODEL // 16
MLSTM_DV = D_MODEL // 8
MLSTM_CHUNK = 64
GATE_CAP = 15.0
FFN_DIM = 11008
N_EXPERTS = 8
EXPERT_FF = D_MODEL // 2
RMS_EPS = 1e-6

LANES = 128
SUBLANES = 8
VMEM_LIMIT = 56 * 1024 * 1024

RANK_PAD = LANES
ZE_DW = 3 * MIX_A
ZE_DA = ZE_DW + RANK_PAD
ZE_DG = ZE_DA + RANK_PAD
ZE_U = ZE_DG + RWKV_GATE_RANK
ZE_W = ZE_U + MIX_B
ZT_DW = 0
ZT_DA = ZT_DW + RANK_PAD
ZT_DG = ZT_DA + RANK_PAD
ZT_U = ZT_DG + RWKV_GATE_RANK
RW_CAT = 3 * LANES + 2 * RANK_PAD + RWKV_GATE_RANK
ZO_OG = 2 * MLSTM_HEADS * MLSTM_DQK + MLSTM_HEADS * MLSTM_DV
ZO_G = ZO_OG + MLSTM_HEADS * MLSTM_DV
ZO_W = ZO_G + 4 * LANES

S5_SLAB_GROUPS = 16
S5_SLABS = S5_GROUPS // S5_SLAB_GROUPS
S5_SLAB_IN = S5_SLAB_GROUPS * S5_GROUP
S5_SLAB_STATE = S5_SLAB_GROUPS * S5_STATE

NN = (((1,), (0,)), ((), ()))
NT = (((1,), (1,)), ((), ()))
TN = (((0,), (0,)), ((), ()))
NEG_BIG = -1e30


def _cparams(sem):
    return pltpu.CompilerParams(dimension_semantics=sem, vmem_limit_bytes=VMEM_LIMIT)


def _pick(n, cands):
    for c in cands:
        if n % c == 0:
            return c
    raise ValueError(f"no tile for {n}")


def _dot(a, b, dims=NN):
    return lax.dot_general(a, b, dims, preferred_element_type=F32)


def _hp_pair(a):
    hi = a.astype(BF16).astype(F32)
    return hi, a - hi


def _hp_lhs(a, axis=1):
    hi, lo = _hp_pair(a)
    return jnp.concatenate([hi, hi, lo], axis=axis).astype(BF16)


def _hp_rhs(b, axis=0):
    hi, lo = _hp_pair(b)
    return jnp.concatenate([hi, lo, hi], axis=axis).astype(BF16)


def _dot_hp(a, b, dims=NN):
    return _dot(_hp_lhs(a, dims[0][0][0]), _hp_rhs(b, dims[0][1][0]), dims)


def _split3(a):
    h0 = a.astype(BF16).astype(F32)
    r1 = a - h0
    h1 = r1.astype(BF16).astype(F32)
    return h0, h1, r1 - h1


def _dot_const(c_bf16, a):
    return _dot(jnp.concatenate([c_bf16] * 3, axis=1), jnp.concatenate(_split3(a), axis=0).astype(BF16))


def _sigmoid(x):
    return 1.0 / (1.0 + jnp.exp(-x))


def _softplus(x):
    return jnp.maximum(x, 0.0) + jnp.log(1.0 + jnp.exp(-jnp.abs(x)))


def _iota(shape, dim):
    return lax.broadcasted_iota(jnp.int32, shape, dim)


def _rmsnorm_body(x_ref, g_ref, o_ref):
    x = x_ref[...]
    ms = jnp.mean(x * x, axis=-1, keepdims=True)
    o_ref[...] = (x * lax.rsqrt(ms + RMS_EPS) * g_ref[...]).astype(o_ref.dtype)


def _rmsnorm(x, g, out_dtype):
    m, d = x.shape
    tm = _pick(m, (256, 128, 64, 32, 16, 8))
    return pl.pallas_call(
        _rmsnorm_body,
        grid=(m // tm,),
        in_specs=[pl.BlockSpec((tm, d), lambda i: (i, 0)), pl.BlockSpec((1, d), lambda i: (0, 0))],
        out_specs=pl.BlockSpec((tm, d), lambda i: (i, 0)),
        out_shape=jax.ShapeDtypeStruct((m, d), out_dtype),
        compiler_params=_cparams(("parallel",)),
        name="rmsnorm",
    )(x, g.reshape(1, d))


def _router_body(x_ref, g_ref, rw_ref, rb_ref, hn_ref, comb_ref):
    x = x_ref[...]
    ms = jnp.mean(x * x, axis=-1, keepdims=True)
    h = x * lax.rsqrt(ms + RMS_EPS) * g_ref[...]
    hn_ref[...] = h.astype(hn_ref.dtype)
    logits = _dot_hp(h, rw_ref[...]) + rb_ref[...]
    lane = _iota(logits.shape, 1)
    logits = jnp.where(lane < N_EXPERTS, logits, NEG_BIG)
    m1 = jnp.max(logits, axis=-1, keepdims=True)
    i1 = jnp.min(jnp.where(logits == m1, lane, LANES), axis=-1, keepdims=True)
    rest = jnp.where(lane == i1, NEG_BIG, logits)
    m2 = jnp.max(rest, axis=-1, keepdims=True)
    i2 = jnp.min(jnp.where(rest == m2, lane, LANES), axis=-1, keepdims=True)
    e = jnp.exp(m2 - m1)
    g1 = 1.0 / (1.0 + e)
    g2 = e / (1.0 + e)
    comb_ref[...] = jnp.where(lane == i1, g1, 0.0) + jnp.where(lane == i2, g2, 0.0)


def _rmsnorm_router(x, g, router_w, router_b):
    m, d = x.shape
    tm = _pick(m, (256, 128, 64, 32, 16, 8))
    rw = jnp.pad(router_w, ((0, 0), (0, LANES - N_EXPERTS)))
    rb = jnp.pad(router_b.reshape(1, N_EXPERTS), ((0, 0), (0, LANES - N_EXPERTS)))
    return pl.pallas_call(
        _router_body,
        grid=(m // tm,),
        in_specs=[pl.BlockSpec((tm, d), lambda i: (i, 0)), pl.BlockSpec((1, d), lambda i: (0, 0)),
                  pl.BlockSpec((d, LANES), lambda i: (0, 0)), pl.BlockSpec((1, LANES), lambda i: (0, 0))],
        out_specs=[pl.BlockSpec((tm, d), lambda i: (i, 0)), pl.BlockSpec((tm, LANES), lambda i: (i, 0))],
        out_shape=[jax.ShapeDtypeStruct((m, d), BF16), jax.ShapeDtypeStruct((m, LANES), F32)],
        compiler_params=_cparams(("parallel",)),
        name="rmsnorm_router",
    )(x, g.reshape(1, d), rw, rb)


def _mm_body(a_ref, w_ref, o_ref):
    o_ref[...] = _dot(a_ref[...], w_ref[...].astype(BF16)).astype(o_ref.dtype)


def _mm(a, w, n_out=None):
    m, k = a.shape
    n = w.shape[1] if n_out is None else n_out
    tm = _pick(m, (1024, 512, 256, 128, 64, 32, 16, 8))
    tn = _pick(n, (512, 256, 128))
    return pl.pallas_call(
        _mm_body,
        grid=(m // tm, n // tn),
        in_specs=[pl.BlockSpec((tm, k), lambda i, j: (i, 0)), pl.BlockSpec((k, tn), lambda i, j: (0, j))],
        out_specs=pl.BlockSpec((tm, tn), lambda i, j: (i, j)),
        out_shape=jax.ShapeDtypeStruct((m, n), F32),
        compiler_params=_cparams(("parallel", "arbitrary")),
        name="proj_in",
    )(a, w)


def _mm_nt_body(a_ref, wt_ref, o_ref):
    o_ref[...] = _dot(a_ref[...], wt_ref[...].astype(BF16), NT)


def _mm_nt(a, wt, n_rows):
    m, k = a.shape
    tm = _pick(m, (1024, 512, 256, 128, 64, 32, 16, 8))
    tn = 4 * LANES
    assert n_rows % tn == 0
    return pl.pallas_call(
        _mm_nt_body,
        grid=(m // tm, n_rows // tn),
        in_specs=[pl.BlockSpec((tm, k), lambda i, j: (i, 0)), pl.BlockSpec((tn, k), lambda i, j: (j, 0))],
        out_specs=pl.BlockSpec((tm, tn), lambda i, j: (i, j)),
        out_shape=jax.ShapeDtypeStruct((m, n_rows), F32),
        compiler_params=_cparams(("parallel", "arbitrary")),
        name="proj_in",
    )(a, wt)


def _mm_split(a, w_t, n_main, w_tail_t):
    return _mm_nt(a, w_t, n_main), _mm_nt(a, w_tail_t, w_tail_t.shape[0])


def _mm2_res_body(a0_ref, a1_ref, w0_ref, w1_ref, r_ref, o_ref):
    acc = _dot(a0_ref[...], w0_ref[...].astype(BF16)) + _dot(a1_ref[...], w1_ref[...].astype(BF16))
    o_ref[...] = r_ref[...] + acc


def _mm2_res(a0, a1, w, res):
    m, k0 = a0.shape
    k1 = a1.shape[1]
    assert k0 == k1
    n = w.shape[1]
    tm = _pick(m, (1024, 512, 256, 128, 64, 32, 16, 8))
    tn = _pick(n, (512, 256, 128))
    return pl.pallas_call(
        _mm2_res_body,
        grid=(m // tm, n // tn),
        in_specs=[pl.BlockSpec((tm, k0), lambda i, j: (i, 0)), pl.BlockSpec((tm, k1), lambda i, j: (i, 0)),
                  pl.BlockSpec((k0, tn), lambda i, j: (0, j)), pl.BlockSpec((k1, tn), lambda i, j: (1, j)),
                  pl.BlockSpec((tm, tn), lambda i, j: (i, j))],
        out_specs=pl.BlockSpec((tm, tn), lambda i, j: (i, j)),
        out_shape=jax.ShapeDtypeStruct((m, n), F32),
        compiler_params=_cparams(("parallel", "arbitrary")),
        name="proj_out",
    )(a0, a1, w, w, res)


def _mm_res_body(a_ref, w_ref, r_ref, o_ref):
    o_ref[...] = r_ref[...] + _dot(a_ref[...], w_ref[...].astype(BF16))


def _mm_res(a, w, res):
    m, k = a.shape
    n = w.shape[1]
    tm = _pick(m, (1024, 512, 256, 128, 64, 32, 16, 8))
    tn = _pick(n, (512, 256, 128))
    return pl.pallas_call(
        _mm_res_body,
        grid=(m // tm, n // tn),
        in_specs=[pl.BlockSpec((tm, k), lambda i, j: (i, 0)), pl.BlockSpec((k, tn), lambda i, j: (0, j)),
                  pl.BlockSpec((tm, tn), lambda i, j: (i, j))],
        out_specs=pl.BlockSpec((tm, tn), lambda i, j: (i, j)),
        out_shape=jax.ShapeDtypeStruct((m, n), F32),
        compiler_params=_cparams(("parallel", "arbitrary")),
        name="proj_out1",
    )(a, w, res)


def _glu_up_body(a_ref, w1_ref, w3_ref, o_ref):
    a = a_ref[...]
    h1 = _dot(a, w1_ref[...].astype(BF16))
    h3 = _dot(a, w3_ref[...].astype(BF16))
    o_ref[...] = (h1 * _sigmoid(h1) * h3).astype(o_ref.dtype)


def _glu_up(a, w1, w3):
    m, k = a.shape
    n = w1.shape[1]
    tm = _pick(m, (1024, 512, 256, 128, 64, 32, 16, 8))
    tn = _pick(n, (256, 128))
    return pl.pallas_call(
        _glu_up_body,
        grid=(m // tm, n // tn),
        in_specs=[pl.BlockSpec((tm, k), lambda i, j: (i, 0)), pl.BlockSpec((k, tn), lambda i, j: (0, j)),
                  pl.BlockSpec((k, tn), lambda i, j: (0, j))],
        out_specs=pl.BlockSpec((tm, tn), lambda i, j: (i, j)),
        out_shape=jax.ShapeDtypeStruct((m, n), BF16),
        compiler_params=_cparams(("parallel", "arbitrary")),
        name="ffn_up",
    )(a, w1, w3)


def _mm_acc_body(a_ref, w_ref, r_ref, o_ref, *, k_total, tk, rc):
    kk = pl.program_id(2)
    ragged = k_total % tk != 0
    valid = k_total - kk * tk
    w = w_ref[...]
    if ragged:
        w = jnp.where(_iota(w.shape, 0) < valid, w, 0.0)
    w = w.astype(BF16)

    @pl.when(kk == 0)
    def _():
        o_ref[...] = r_ref[...]

    a = a_ref[...]
    if ragged:
        a = jnp.where(_iota(a.shape, 1) < valid, a, jnp.zeros_like(a))
    for c0 in range(0, w.shape[1], rc):
        o_ref[:, c0:c0 + rc] += _dot(a, w[:, c0:c0 + rc])


def _mm_acc(a, w, res):
    m, k = a.shape
    n = w.shape[1]
    tm = _pick(m, (2048, 1024, 512, 256, 128, 64, 32, 16, 8))
    tn = _pick(n, (1024, 512, 256, 128))
    tk = 512
    nk = pl.cdiv(k, tk)
    return pl.pallas_call(
        functools.partial(_mm_acc_body, k_total=k, tk=tk, rc=min(tn, 2 * LANES)),
        grid=(m // tm, n // tn, nk),
        in_specs=[pl.BlockSpec((tm, tk), lambda i, j, kk: (i, kk)), pl.BlockSpec((tk, tn), lambda i, j, kk: (kk, j)),
                  pl.BlockSpec((tm, tn), lambda i, j, kk: (i, j))],
        out_specs=pl.BlockSpec((tm, tn), lambda i, j, kk: (i, j)),
        out_shape=jax.ShapeDtypeStruct((m, n), F32),
        compiler_params=_cparams(("parallel", "parallel", "arbitrary")),
        name="proj_down",
    )(a, w, res)


MOE_TM = 512
MOE_BLK = 256
MOE_ALIGN = 16


def _moe_rank_body(comb_ref, rank_ref, cnt_ref):
    comb = comb_ref[...]
    tt = comb.shape[0]
    lane = _iota(comb.shape, 1)
    sel = jnp.where((comb > 0.0) & (lane < N_EXPERTS), 1.0, 0.0)
    below = jnp.where(_iota((tt, tt), 0) > _iota((tt, tt), 1), 1.0, 0.0).astype(BF16)
    rank = _dot(below, sel.astype(BF16))
    rank_ref[...] = rank
    cnt_ref[...] = rank[tt - 1:tt, :] + sel[tt - 1:tt, :]


def _moe_rank(comb, tt):
    m = comb.shape[0]
    nt = m // tt
    return pl.pallas_call(
        _moe_rank_body,
        grid=(nt,),
        in_specs=[pl.BlockSpec((tt, LANES), lambda i: (i, 0))],
        out_specs=[pl.BlockSpec((tt, LANES), lambda i: (i, 0)), pl.BlockSpec((None, 1, LANES), lambda i: (i, 0, 0))],
        out_shape=[jax.ShapeDtypeStruct((m, LANES), F32), jax.ShapeDtypeStruct((nt, 1, LANES), F32)],
        compiler_params=_cparams(("parallel",)),
        name="moe_rank",
    )(comb)


def _moe_dispatch_body(seg_ref, base_ref, x_ref, rank_ref, comb_ref, xg_in_ref, xg_ref, buf, sem, *, tt):
    del xg_in_ref
    i = pl.program_id(0)
    x = x_ref[...]
    rank_t = rank_ref[...].T
    comb_t = comb_ref[...].T
    rr = _iota((MOE_BLK, tt), 0)

    def copy(slot, row0):
        return pltpu.make_async_copy(buf.at[slot], xg_ref.at[pl.ds(pl.multiple_of(row0, MOE_ALIGN), MOE_BLK)], sem.at[slot])

    def block(e, b, slot, base_e):
        hit = (rank_t[e:e + 1, :] == (rr + b * MOE_BLK).astype(F32)) & (comb_t[e:e + 1, :] > 0.0)
        onehot = jnp.where(hit, 1.0, 0.0).astype(BF16)
        buf[slot] = _dot(onehot, x).astype(buf.dtype)
        copy(slot, base_e + b * MOE_BLK).start()

    for e in range(N_EXPERTS):
        slot = e % 2
        seg_e = seg_ref[i * N_EXPERTS + e]
        base_e = base_ref[i * N_EXPERTS + e]
        if e >= 2:
            copy(slot, 0).wait()
        block(e, 0, slot, base_e)

        def extra(b, carry, e=e, slot=slot, base_e=base_e):
            copy(slot, 0).wait()
            block(e, b, slot, base_e)
            return carry

        lax.fori_loop(1, (seg_e + MOE_BLK - 1) // MOE_BLK, extra, 0)
    copy(0, 0).wait()
    copy(1, 0).wait()


def _moe_dispatch(hn, rank, comb, seg, base, p_max, tt):
    m, d = hn.shape
    return pl.pallas_call(
        functools.partial(_moe_dispatch_body, tt=tt),
        grid_spec=pltpu.PrefetchScalarGridSpec(
            num_scalar_prefetch=2,
            grid=(m // tt,),
            in_specs=[pl.BlockSpec((tt, d), lambda i, *_: (i, 0)), pl.BlockSpec((tt, LANES), lambda i, *_: (i, 0)),
                      pl.BlockSpec((tt, LANES), lambda i, *_: (i, 0)), pl.BlockSpec(memory_space=pl.ANY)],
            out_specs=pl.BlockSpec(memory_space=pl.ANY),
            scratch_shapes=[pltpu.VMEM((2, MOE_BLK, d), BF16), pltpu.SemaphoreType.DMA((2,))],
        ),
        out_shape=jax.ShapeDtypeStruct((p_max, d), BF16),
        input_output_aliases={5: 0},
        compiler_params=_cparams(("arbitrary",)),
        name="moe_dispatch",
    )(seg, base, hn, rank, comb, jnp.zeros((p_max, d), BF16))


def _moe_gup_body(te_ref, nv_ref, x_ref, w1_ref, w3_ref, o_ref):
    k = pl.program_id(1)

    @pl.when(k < nv_ref[0])
    def _():
        x = x_ref[...]
        h1 = _dot(x, w1_ref[...].astype(BF16))
        h3 = _dot(x, w3_ref[...].astype(BF16))
        o_ref[...] = (h1 * _sigmoid(h1) * h3).astype(o_ref.dtype)

    @pl.when(k >= nv_ref[0])
    def _():
        o_ref[...] = jnp.zeros_like(o_ref)


def _moe_gdown_body(te_ref, nv_ref, h_ref, w_ref, o_ref):
    k = pl.program_id(1)

    @pl.when(k < nv_ref[0])
    def _():
        o_ref[...] = _dot(h_ref[...], w_ref[...].astype(BF16)).astype(o_ref.dtype)

    @pl.when(k >= nv_ref[0])
    def _():
        o_ref[...] = jnp.zeros_like(o_ref)


def _moe_grouped(xg, te, nv, w1, w3, w2):
    p, d = xg.shape
    _, _, f = w1.shape
    tm = MOE_TM
    nk = p // tm
    tn = 4 * LANES
    rowi = lambda k, nv_: jnp.minimum(k, nv_[0] - 1)
    hid = pl.pallas_call(
        _moe_gup_body,
        grid_spec=pltpu.PrefetchScalarGridSpec(
            num_scalar_prefetch=2,
            grid=(f // tn, nk),
            in_specs=[pl.BlockSpec((tm, d), lambda j, k, te_, nv_: (rowi(k, nv_), 0)),
                      pl.BlockSpec((None, d, tn), lambda j, k, te_, nv_: (te_[rowi(k, nv_)], 0, j)),
                      pl.BlockSpec((None, d, tn), lambda j, k, te_, nv_: (te_[rowi(k, nv_)], 0, j))],
            out_specs=pl.BlockSpec((tm, tn), lambda j, k, te_, nv_: (k, j)),
        ),
        out_shape=jax.ShapeDtypeStruct((p, f), BF16),
        compiler_params=_cparams(("arbitrary", "arbitrary")),
        name="moe_up",
    )(te, nv, xg, w1, w3)
    tn2 = 8 * LANES
    return pl.pallas_call(
        _moe_gdown_body,
        grid_spec=pltpu.PrefetchScalarGridSpec(
            num_scalar_prefetch=2,
            grid=(d // tn2, nk),
            in_specs=[pl.BlockSpec((tm, f), lambda j, k, te_, nv_: (rowi(k, nv_), 0)),
                      pl.BlockSpec((None, f, tn2), lambda j, k, te_, nv_: (te_[rowi(k, nv_)], 0, j))],
            out_specs=pl.BlockSpec((tm, tn2), lambda j, k, te_, nv_: (k, j)),
        ),
        out_shape=jax.ShapeDtypeStruct((p, d), BF16),
        compiler_params=_cparams(("arbitrary", "arbitrary")),
        name="moe_down",
    )(te, nv, hid, w2)


def _moe_combine_body(seg_ref, base_ref, x_ref, rank_ref, comb_ref, g_ref, y_ref, o_ref, buf, sem, *, tt, tile0):
    i = pl.program_id(0) + tile0
    rank = rank_ref[...]
    comb = comb_ref[...]
    cc = _iota((tt, MOE_BLK), 1)

    def copy(slot, row0):
        return pltpu.make_async_copy(y_ref.at[pl.ds(pl.multiple_of(row0, MOE_ALIGN), MOE_BLK)], buf.at[slot], sem.at[slot])

    def gathered(e, b, slot):
        hit = (rank[:, e:e + 1] == (cc + b * MOE_BLK).astype(F32)) & (comb[:, e:e + 1] > 0.0)
        return comb[:, e:e + 1] * _dot(jnp.where(hit, 1.0, 0.0).astype(BF16), buf[slot])

    o_ref[...] = x_ref[...]
    copy(0, base_ref[i * N_EXPERTS]).start()
    for e in range(N_EXPERTS):
        slot = e % 2
        seg_e = seg_ref[i * N_EXPERTS + e]
        base_e = base_ref[i * N_EXPERTS + e]
        if e + 1 < N_EXPERTS:
            copy(1 - slot, base_ref[i * N_EXPERTS + e + 1]).start()
        copy(slot, 0).wait()
        o_ref[...] += gathered(e, 0, slot)

        def extra(b, carry, e=e, slot=slot, base_e=base_e):
            copy(slot, base_e + b * MOE_BLK).start()
            copy(slot, 0).wait()
            o_ref[...] += gathered(e, b, slot)
            return carry

        lax.fori_loop(1, (seg_e + MOE_BLK - 1) // MOE_BLK, extra, 0)
    acc = o_ref[...]
    ms = jnp.mean(acc * acc, axis=-1, keepdims=True)
    o_ref[...] = acc * lax.rsqrt(ms + RMS_EPS) * g_ref[...]


def _moe_combine(x, rank, comb, y, seg, base, tt, norm_g, tile0):
    m, d = x.shape
    return pl.pallas_call(
        functools.partial(_moe_combine_body, tt=tt, tile0=tile0),
        grid_spec=pltpu.PrefetchScalarGridSpec(
            num_scalar_prefetch=2,
            grid=(m // tt,),
            in_specs=[pl.BlockSpec((tt, d), lambda i, *_: (i, 0)), pl.BlockSpec((tt, LANES), lambda i, *_: (i + tile0, 0)),
                      pl.BlockSpec((tt, LANES), lambda i, *_: (i + tile0, 0)), pl.BlockSpec((1, d), lambda i, *_: (0, 0)),
                      pl.BlockSpec(memory_space=pl.ANY)],
            out_specs=pl.BlockSpec((tt, d), lambda i, *_: (i, 0)),
            scratch_shapes=[pltpu.VMEM((2, MOE_BLK, d), BF16), pltpu.SemaphoreType.DMA((2,))],
        ),
        out_shape=jax.ShapeDtypeStruct((m, d), F32),
        compiler_params=_cparams(("arbitrary",)),
        name="moe_combine",
    )(seg, base, x, rank, comb, norm_g.reshape(1, d), y)


def _moe_sparse(xs, hns, combs, w1, w3, w2, norm_g):
    hn = jnp.concatenate(hns, axis=0)
    comb = jnp.concatenate(combs, axis=0)
    m, d = hn.shape
    ne = N_EXPERTS
    tt = _pick(math.gcd(*[x.shape[0] for x in xs]), (512, 256, 128, 64, 32, 16))
    nt = m // tt
    tm = MOE_TM
    rank, cnt = _moe_rank(comb, tt)
    cnt = cnt[:, 0, :ne].astype(jnp.int32)
    seg = (cnt + MOE_ALIGN - 1) // MOE_ALIGN * MOE_ALIGN
    grp = (jnp.sum(seg, axis=0) + MOE_BLK + tm - 1) // tm * tm
    ends = jnp.cumsum(grp)
    base = (ends - grp)[None, :] + jnp.cumsum(seg, axis=0) - seg
    p_max = (2 * m + nt * ne * MOE_ALIGN + ne * (MOE_BLK + tm) + tm - 1) // tm * tm
    nk = p_max // tm
    te = jnp.minimum(jnp.sum(ends[None, :] <= (jnp.arange(nk, dtype=jnp.int32) * tm)[:, None], axis=1), ne - 1).astype(jnp.int32)
    nv = (ends[-1] // tm).astype(jnp.int32).reshape(1)
    seg = seg.reshape(-1)
    base = base.astype(jnp.int32).reshape(-1)
    xg = _moe_dispatch(hn, rank, comb, seg, base, p_max, tt)
    y = _moe_grouped(xg, te, nv, w1, w3, w2)
    outs, tile0 = [], 0
    for x in xs:
        outs.append(_moe_combine(x, rank, comb, y, seg, base, tt, norm_g, tile0))
        tile0 += x.shape[0] // tt
    return outs


def _rwkv_body(zr_ref, zk_ref, zv_ref, zdw_ref, zda_ref, zdg_ref, mu_ref, sh_ref, s0_ref,
               w0_ref, wup_ref, a0_ref, aup_ref, gup_ref, kk_ref, ka_ref, rk_ref, lw_ref, lb_ref,
               y_ref, sT_ref, s_scr, carry_scr, *, bb, rt, chunk, valid, unroll):
    n = pl.program_id(2)
    n_last = pl.num_programs(2) - 1
    L = chunk
    L2 = 2 * L
    H = RWKV_HEAD
    nchunk = rt // L
    nsq = max(1, int(math.ceil(math.log2(L))))

    lane = _iota((1, LANES), 1)
    m0 = (lane < H).astype(F32)
    m1 = 1.0 - m0
    r2 = _iota((L2, L2), 0)
    c2 = _iota((L2, L2), 1)
    rh = jnp.where(r2 >= L, 1, 0)
    ch = jnp.where(c2 >= L, 1, 0)
    same = jnp.where(rh == ch, 1.0, 0.0)
    tdiff = (r2 - L * rh) - (c2 - L * ch)
    strict = same * jnp.where(tdiff > 0, 1.0, 0.0)
    incl = same * jnp.where(tdiff >= 0, 1.0, 0.0)
    rl = _iota((L, L), 0)
    cl = _iota((L, L), 1)
    tril = jnp.where(rl >= cl, 1.0, 0.0).astype(BF16)
    ri = _iota((LANES, LANES), 0)
    ci = _iota((LANES, LANES), 1)
    bd = jnp.where((ri >= H) == (ci >= H), 1.0, 0.0)
    bones = bd.astype(BF16)
    row = _iota((L, 1), 0)

    @pl.when(n == 0)
    def _():
        s_scr[...] = jnp.zeros_like(s_scr)
        carry_scr[...] = jnp.zeros_like(carry_scr)

    mu = mu_ref[...]
    w0 = w0_ref[...]
    a0 = a0_ref[...]
    wup_b = _hp_rhs(wup_ref[...])
    aup_b = _hp_rhs(aup_ref[...])
    gup_b = _hp_rhs(gup_ref[...])
    kkw = kk_ref[...]
    kaw = ka_ref[...]
    rkw = rk_ref[...]
    lnw = lw_ref[...]
    lnb = lb_ref[...]

    def block_sum(x):
        return _dot_const_rhs(x, bones)

    def load(seq, r0, is_start):
        zc = jnp.concatenate([ref[seq, pl.ds(r0, L), :] for ref in (zr_ref, zk_ref, zv_ref, zdw_ref, zda_ref, zdg_ref)],
                             axis=1)
        first = jnp.where(is_start, sh_ref[pl.ds(seq, 1), :], carry_scr[seq, 0:1, :])
        s0 = s0_ref[seq, 0]
        s1 = s0_ref[seq, 1]
        zz = jnp.zeros((H, H), F32)
        s_init = jnp.concatenate([jnp.concatenate([s0, zz], axis=1), jnp.concatenate([zz, s1], axis=1)], axis=0)
        st = jnp.where(is_start, s_init, s_scr[seq])
        return zc, first, st

    def recur(seqs):
        pre = []
        for r, k, v, ka, kb, lw, cum, st in seqs:
            ce = jnp.exp(cum)
            cinv = jnp.exp(-cum)
            at = jnp.exp(cum - lw) * ka
            rt_ = ce * r
            bt = kb * cinv
            kt = k * cinv
            c_last = ce[L - 1:L, :]
            lhs = jnp.concatenate([at * m0, at * m1, rt_ * m0, rt_ * m1], axis=0)
            rhs = jnp.concatenate([bt * m0, bt * m1, kt * m0, kt * m1, st], axis=0)
            bk = jnp.concatenate([bt * c_last, kt * c_last], axis=0)
            pre.append((lhs, rhs, bk, c_last))
        gms = [_dot_x(p_[0], p_[1], NT) for p_ in pre]
        wvs = [_dot_x(jnp.concatenate([gm[0:L2, L2:2 * L2] * strict, gm[L2:, L2:2 * L2] * incl], axis=0),
                      jnp.concatenate([s[2], s[2]], axis=0)) for gm, s in zip(gms, seqs)]
        xs = [gm[0:L2, 2 * L2:] + wv[0:L2] for gm, wv in zip(gms, wvs)]
        ps = [gm[0:L2, 0:L2] * strict for gm in gms]
        for q in range(nsq):
            if q + 1 < nsq:
                xps = [_dot_x(p, jnp.concatenate([x, p], axis=1)) for x, p in zip(xs, ps)]
                xs = [x + xp[:, 0:LANES] for x, xp in zip(xs, xps)]
                ps = [xp[:, LANES:] for xp in xps]
            else:
                xs = [x + _dot_x(p, x) for x, p in zip(xs, ps)]
        us = [x[0:L, :] * m0 + x[L:L2, :] * m1 for x in xs]
        rus = [_dot_x(gm[L2:, 0:L2] * incl, jnp.concatenate([u, u], axis=0)) for gm, u in zip(gms, us)]
        sus = [_dot_x(jnp.concatenate([u, s[2]], axis=0), p_[2], TN) for u, s, p_ in zip(us, seqs, pre)]
        out = []
        for gm, wv, ru, su, s, p_ in zip(gms, wvs, rus, sus, seqs, pre):
            y2 = gm[L2:, 2 * L2:] + wv[L2:] + ru
            out.append((y2[0:L, :] * m0 + y2[L:L2, :] * m1, s[7] * p_[3] + bd * su))
        return out

    def compute_all(loaded):
        nu = len(loaded)
        stack = lambda xs: xs[0] if nu == 1 else jnp.concatenate(xs, axis=0)
        part = lambda x, u: x[u * L:(u + 1) * L]
        zc = stack([x[0] for x in loaded])
        zprev = stack([jnp.where(row == 0, x[1], pltpu.roll(x[0], 1, 0)) for x in loaded])
        zs = zc + mu * (zprev - zc)
        r = zs[:, 0:LANES]
        k = zs[:, LANES:2 * LANES]
        v = zs[:, 2 * LANES:3 * LANES]
        dw = zs[:, 3 * LANES:4 * LANES]
        da = zs[:, 4 * LANES:5 * LANES]
        dg = zs[:, 5 * LANES:]

        w_log = -_softplus(-(w0 + _dot(_hp_lhs(jnp.tanh(dw)), wup_b))) - 0.5
        lw = -jnp.exp(w_log)
        a = _sigmoid(a0 + _dot(_hp_lhs(da), aup_b))
        g = _dot(_hp_lhs(_sigmoid(dg)), gup_b)
        kk = k * kkw
        k = k * (1.0 + (a - 1.0) * kaw)
        rows = nu * L
        sums = block_sum(jnp.concatenate([kk * kk, r * k * rkw], axis=0))
        kk = kk / jnp.maximum(jnp.sqrt(sums[0:rows]), 1e-12)
        bonus = sums[rows:] * v
        ka = -kk
        kb = kk * a
        if valid < L:
            ok = stack([row < valid] * nu)
            lw = jnp.where(ok, lw, 0.0)
            r = jnp.where(ok, r, 0.0)
            k = jnp.where(ok, k, 0.0)
            v = jnp.where(ok, v, 0.0)
            ka = jnp.where(ok, ka, 0.0)
            kb = jnp.where(ok, kb, 0.0)

        lw_wide = lw if nu == 1 else jnp.concatenate([part(lw, u) for u in range(nu)], axis=1)
        cum_wide = _dot_const(tril, lw_wide)
        res = recur([(part(r, u), part(k, u), part(v, u), part(ka, u), part(kb, u), part(lw, u),
                      cum_wide[:, u * LANES:(u + 1) * LANES], loaded[u][2]) for u in range(nu)])
        sts = [x[1] for x in res]
        y = stack([x[0] for x in res])
        mean = block_sum(y) * (1.0 / H)
        yc = y - mean
        var = block_sum(yc * yc) * (1.0 / H)
        yn = yc * lax.rsqrt(var + RWKV_LNX_EPS) * lnw + lnb
        out = (yn + bonus) * g
        return [(part(out, u).astype(y_ref.dtype), sts[u], loaded[u][0][valid - 1:valid, :]) for u in range(nu)]

    def store(seq, r0, is_end, y, st, last_row):
        y_ref[seq, pl.ds(r0, L), :] = y
        s_scr[seq] = st
        carry_scr[seq, 0:1, :] = last_row

        @pl.when(is_end)
        def _():
            sT_ref[seq, 0] = st[0:H, 0:H]
            sT_ref[seq, 1] = st[H:2 * H, H:2 * H]

    def step(it, carry):
        sg = it // nchunk
        ci_ = it - sg * nchunk
        r0 = pl.multiple_of(ci_ * L, L)
        is_start = jnp.logical_and(n == 0, ci_ == 0)
        is_end = jnp.logical_and(n == n_last, ci_ == nchunk - 1)
        seqs = [sg * unroll + u for u in range(unroll)]
        loaded = [load(s, r0, is_start) for s in seqs]
        done = compute_all(loaded)
        for s, d in zip(seqs, done):
            store(s, r0, is_end, *d)
        return carry

    lax.fori_loop(0, (bb // unroll) * nchunk, step, 0)


def _dot_x(a, b, dims=NN):
    return _dot(a.astype(BF16), b.astype(BF16), dims)


def _dot_const_rhs(a, c_bf16):
    return _dot(jnp.concatenate(_split3(a), axis=1).astype(BF16), jnp.concatenate([c_bf16] * 3, axis=0))


def _rwkv(z, zt, shift_p, wkv0, mu_p, w0, wup_p, a0, aup_p, g_up, k_k, k_a, r_k, lnx_w, lnx_b,
          *, n_seq, seq_rows, chunk, valid):
    unroll = _pick(n_seq, (4, 2, 1))
    if seq_rows > chunk:
        rt = _pick(seq_rows, (256, 128, 64))
        bb = unroll
    else:
        rt = seq_rows
        bb = _pick(n_seq, (32, 16, 8, 4, 2, 1))
    assert rt % chunk == 0 and bb % unroll == 0 and (bb == n_seq or bb % SUBLANES == 0)
    sblk = lambda c, g, n: (g, c, 0, 0)
    npair = RWKV_HEADS // 2
    nrb = LANES

    def cat(a):
        rows = a.shape[0]
        rkv = a[:, :3 * MIX_A].reshape(rows, 3, npair, LANES).transpose(2, 0, 1, 3).reshape(npair, rows, 3 * LANES)
        tail = jnp.broadcast_to(a[None, :, 3 * MIX_A:], (npair, rows, ZE_U - 3 * MIX_A))
        return jnp.concatenate([rkv, tail], axis=2)

    mu_cat = cat(mu_p.reshape(1, ZE_U))
    sh_cat = cat(shift_p)
    row2 = lambda a: a.reshape(1, MIX_A)
    vec = pl.BlockSpec((1, LANES), lambda c, g, n: (0, c))
    kern = functools.partial(_rwkv_body, bb=bb, rt=rt, chunk=chunk, valid=valid, unroll=unroll)
    y, s_t = pl.pallas_call(
        kern,
        grid=(npair, n_seq // bb, seq_rows // rt),
        in_specs=[
            pl.BlockSpec((bb, rt, nrb), lambda c, g, n: (g, n, c)),
            pl.BlockSpec((bb, rt, nrb), lambda c, g, n: (g, n, npair + c)),
            pl.BlockSpec((bb, rt, nrb), lambda c, g, n: (g, n, 2 * npair + c)),
            pl.BlockSpec((bb, rt, RANK_PAD), lambda c, g, n: (g, n, ZT_DW // RANK_PAD)),
            pl.BlockSpec((bb, rt, RANK_PAD), lambda c, g, n: (g, n, ZT_DA // RANK_PAD)),
            pl.BlockSpec((bb, rt, RWKV_GATE_RANK), lambda c, g, n: (g, n, ZT_DG // RWKV_GATE_RANK)),
            pl.BlockSpec((None, 1, RW_CAT), lambda c, g, n: (c, 0, 0)),
            pl.BlockSpec((None, bb, RW_CAT), lambda c, g, n: (c, g, 0)),
            pl.BlockSpec((bb, 2, RWKV_HEAD, RWKV_HEAD), sblk),
            vec,
            pl.BlockSpec((RANK_PAD, LANES), lambda c, g, n: (0, c)),
            vec,
            pl.BlockSpec((RANK_PAD, LANES), lambda c, g, n: (0, c)),
            pl.BlockSpec((RWKV_GATE_RANK, LANES), lambda c, g, n: (0, c)),
            vec, vec, vec, vec, vec,
        ],
        out_specs=[pl.BlockSpec((bb, rt, LANES), lambda c, g, n: (g, n, c)),
                   pl.BlockSpec((bb, 2, RWKV_HEAD, RWKV_HEAD), sblk)],
        out_shape=[jax.ShapeDtypeStruct((n_seq, seq_rows, MIX_A), BF16),
                   jax.ShapeDtypeStruct((n_seq, RWKV_HEADS, RWKV_HEAD, RWKV_HEAD), F32)],
        scratch_shapes=[pltpu.VMEM((bb, LANES, LANES), F32), pltpu.VMEM((bb, SUBLANES, RW_CAT), F32)],
        compiler_params=_cparams(("parallel", "parallel", "arbitrary")),
        name="rwkv7",
    )(z, z, z, zt, zt, zt, mu_cat, sh_cat, wkv0,
      row2(w0), wup_p, row2(a0), aup_p, g_up, row2(k_k), row2(k_a), row2(r_k), row2(lnx_w), row2(lnx_b))
    return y, s_t


def _rwkv_cat(a):
    npair = RWKV_HEADS // 2
    rows = a.shape[0]
    rkv = a[:, :3 * MIX_A].reshape(rows, 3, npair, LANES).transpose(2, 0, 1, 3).reshape(npair, rows, 3 * LANES)
    tail = jnp.broadcast_to(a[None, :, 3 * MIX_A:], (npair, rows, ZE_U - 3 * MIX_A))
    return jnp.concatenate([rkv, tail], axis=2)


def _rwkv_step_body(zr_ref, zk_ref, zv_ref, zdw_ref, zda_ref, zdg_ref, mu_ref, sh_ref, s0_ref,
                    w0_ref, wup_ref, a0_ref, aup_ref, gup_ref, kk_ref, ka_ref, rk_ref, lw_ref, lb_ref,
                    y_ref, sT_ref, op_scr, yt_scr, *, steps, nb):
    H = RWKV_HEAD
    rows = steps * nb
    ri = _iota((LANES, LANES), 0)
    ci = _iota((LANES, LANES), 1)
    bones = jnp.where((ri >= H) == (ci >= H), 1.0, 0.0).astype(BF16)
    block_sum = lambda x: _dot_const_rhs(x, bones)

    zc = jnp.concatenate([ref[...].reshape(rows, ref.shape[2]) for ref in (zr_ref, zk_ref, zv_ref, zdw_ref, zda_ref, zdg_ref)],
                         axis=1)
    zprev = jnp.concatenate([sh_ref[...], zc[0:rows - nb]], axis=0) if steps > 1 else sh_ref[...]
    zs = zc + mu_ref[...] * (zprev - zc)
    r = zs[:, 0:LANES]
    k = zs[:, LANES:2 * LANES]
    v = zs[:, 2 * LANES:3 * LANES]
    dw = zs[:, 3 * LANES:4 * LANES]
    da = zs[:, 4 * LANES:5 * LANES]
    dg = zs[:, 5 * LANES:]
    w_log = -_softplus(-(w0_ref[...] + _dot_hp(jnp.tanh(dw), wup_ref[...]))) - 0.5
    decay = jnp.exp(-jnp.exp(w_log))
    a = _sigmoid(a0_ref[...] + _dot_hp(da, aup_ref[...]))
    g = _dot_hp(_sigmoid(dg), gup_ref[...])
    kk = k * kk_ref[...]
    k = k * (1.0 + (a - 1.0) * ka_ref[...])
    sums = block_sum(jnp.concatenate([kk * kk, r * k * rk_ref[...]], axis=0))
    kk = kk / jnp.maximum(jnp.sqrt(sums[0:rows]), 1e-12)
    bonus = sums[rows:] * v

    for qi, x in enumerate((decay, -kk, kk * a, k, r, v)):
        for t in range(steps):
            op_scr[qi, t] = x[t * nb:(t + 1) * nb, :].T

    for t in range(steps):
        src = s0_ref if t == 0 else sT_ref
        for hh in range(2):
            lo = hh * H
            w_t = op_scr[0, t, lo:lo + H, :]
            ka_t = op_scr[1, t, lo:lo + H, :]
            kb_t = op_scr[2, t, lo:lo + H, :]
            k_t = op_scr[3, t, lo:lo + H, :]
            r_t = op_scr[4, t, lo:lo + H, :]

            def value_row(i, carry, src=src, hh=hh, lo=lo, t=t, w_t=w_t, ka_t=ka_t, kb_t=kb_t, k_t=k_t, r_t=r_t):
                s_i = src[hh, i]
                sa = jnp.sum(s_i * ka_t, axis=0, keepdims=True)
                v_i = op_scr[5, t, pl.ds(lo + i, 1), :]
                s_n = s_i * w_t + sa * kb_t + v_i * k_t
                sT_ref[hh, i] = s_n
                yt_scr[t, pl.ds(lo + i, 1), :] = jnp.sum(s_n * r_t, axis=0, keepdims=True)
                return carry

            lax.fori_loop(0, H, value_row, 0, unroll=4)

    y = jnp.concatenate([yt_scr[t].T for t in range(steps)], axis=0)
    mean = block_sum(y) * (1.0 / H)
    yc = y - mean
    var = block_sum(yc * yc) * (1.0 / H)
    yn = yc * lax.rsqrt(var + RWKV_LNX_EPS) * lw_ref[...] + lb_ref[...]
    y_ref[...] = ((yn + bonus) * g).reshape(steps, nb, LANES).astype(y_ref.dtype)


def _rwkv_step(z_tb, zt_tb, shift_p, wkv0_t, mu_p, w0, wup_p, a0, aup_p, g_up, k_k, k_a, r_k, lnx_w, lnx_b):
    steps, nb, _ = z_tb.shape
    assert nb == LANES
    npair = RWKV_HEADS // 2
    row2 = lambda a: a.reshape(1, MIX_A)
    vec = pl.BlockSpec((1, LANES), lambda c: (0, c))
    zspec = lambda width, idx: pl.BlockSpec((steps, nb, width), lambda c: (0, 0, idx(c)))
    sspec = pl.BlockSpec((2, RWKV_HEAD, RWKV_HEAD, nb), lambda c: (c, 0, 0, 0))
    kern = functools.partial(_rwkv_step_body, steps=steps, nb=nb)
    return pl.pallas_call(
        kern,
        grid=(npair,),
        in_specs=[
            zspec(LANES, lambda c: c), zspec(LANES, lambda c: npair + c), zspec(LANES, lambda c: 2 * npair + c),
            zspec(RANK_PAD, lambda c: ZT_DW // RANK_PAD), zspec(RANK_PAD, lambda c: ZT_DA // RANK_PAD),
            zspec(RWKV_GATE_RANK, lambda c: ZT_DG // RWKV_GATE_RANK),
            pl.BlockSpec((None, 1, RW_CAT), lambda c: (c, 0, 0)),
            pl.BlockSpec((None, nb, RW_CAT), lambda c: (c, 0, 0)),
            sspec,
            vec,
            pl.BlockSpec((RANK_PAD, LANES), lambda c: (0, c)),
            vec,
            pl.BlockSpec((RANK_PAD, LANES), lambda c: (0, c)),
            pl.BlockSpec((RWKV_GATE_RANK, LANES), lambda c: (0, c)),
            vec, vec, vec, vec, vec,
        ],
        out_specs=[pl.BlockSpec((steps, nb, LANES), lambda c: (0, 0, c)), sspec],
        out_shape=[jax.ShapeDtypeStruct((steps, nb, MIX_A), BF16),
                   jax.ShapeDtypeStruct((RWKV_HEADS, RWKV_HEAD, RWKV_HEAD, nb), F32)],
        scratch_shapes=[pltpu.VMEM((6, steps, LANES, nb), F32), pltpu.VMEM((steps, LANES, nb), F32)],
        compiler_params=_cparams(("parallel",)),
        name="rwkv7_step",
    )(z_tb, z_tb, z_tb, zt_tb, zt_tb, zt_tb, _rwkv_cat(mu_p.reshape(1, ZE_U)), _rwkv_cat(shift_p), wkv0_t,
      row2(w0), wup_p, row2(a0), aup_p, g_up, row2(k_k), row2(k_a), row2(r_k), row2(lnx_w), row2(lnx_b))


def _s5_prep_body(are_ref, aim_ref, ls_ref, abre_ref, abim_ref, zr_ref, zi_ref):
    lam_re = are_ref[...]
    lam_im = aim_ref[...]
    dt = jnp.exp(ls_ref[...])
    mag = jnp.exp(lam_re * dt)
    ab_re = mag * jnp.cos(lam_im * dt)
    ab_im = mag * jnp.sin(lam_im * dt)
    inv = 1.0 / (lam_re * lam_re + lam_im * lam_im)
    abre_ref[...] = ab_re
    abim_ref[...] = ab_im
    zr_ref[...] = ((ab_re - 1.0) * lam_re + ab_im * lam_im) * inv
    zi_ref[...] = (ab_im * lam_re - (ab_re - 1.0) * lam_im) * inv


def _s5_prep(a_re, a_im, log_step):
    g, n = a_re.shape
    sd = jax.ShapeDtypeStruct((g, n), F32)
    return pl.pallas_call(_s5_prep_body, out_shape=[sd, sd, sd, sd], name="s5_discretise")(
        a_re, a_im, jnp.broadcast_to(log_step.reshape(g, 1), (g, n)))


def _gelu_tanh(x):
    return 0.5 * x * (1.0 + jnp.tanh(math.sqrt(2.0 / math.pi) * (x + 0.044715 * (x * x * x))))


def _s5_body(u_ref, wbr_ref, wbi_ref, wcr_ref, wci_ref, abr_ref, abi_ref, d_ref, h0r_ref, h0i_ref,
             y_ref, hTr_ref, hTi_ref, hr_scr, hi_scr, pr_scr, pi_scr, cr_scr, ci_scr, u_st, y_st, *, rt, seq_rows):
    n = pl.program_id(1)
    long_seq = seq_rows >= rt
    grp = SUBLANES if long_seq else rt // seq_rows
    steps = rt // grp
    ar1 = abr_ref[...]
    ai1 = abi_ref[...]
    ar = jnp.broadcast_to(ar1, (grp, ar1.shape[1]))
    ai = jnp.broadcast_to(ai1, (grp, ai1.shape[1]))
    nhalf = u_ref.shape[1] // LANES
    for hf in range(nhalf):
        u_st[hf] = u_ref[:, hf * LANES:(hf + 1) * LANES]
    u = jnp.concatenate(
        [jnp.concatenate([u_st[hf, pl.ds(j, grp, stride=steps), :] for hf in range(nhalf)], axis=1) for j in range(steps)],
        axis=0)
    ub = u.astype(BF16)
    hr_scr[...] = _dot(ub, wbr_ref[...].astype(BF16))
    hi_scr[...] = _dot(ub, wbi_ref[...].astype(BF16))

    def rows(j):
        return pl.ds(j * grp, grp) if isinstance(j, int) else pl.ds(pl.multiple_of(j * grp, grp), grp)

    def scan_step(j, c):
        hr, hi = c
        nr = ar * hr - ai * hi + hr_scr[rows(j), :]
        ni = ar * hi + ai * hr + hi_scr[rows(j), :]
        hr_scr[rows(j), :] = nr
        hi_scr[rows(j), :] = ni
        return nr, ni

    if long_seq:
        @pl.when(n == 0)
        def _():
            def pw_step(j, c):
                qr, qi = c
                pr_scr[pl.ds(j, 1), :] = qr
                pi_scr[pl.ds(j, 1), :] = qi
                return qr * ar1 - qi * ai1, qr * ai1 + qi * ar1
            lax.fori_loop(0, steps, pw_step, (ar1, ai1))
            cr_scr[...] = jnp.zeros_like(cr_scr)
            ci_scr[...] = jnp.zeros_like(ci_scr)

        b = (n * rt) // seq_rows
        is_start = ((n * rt) % seq_rows) == 0
        h_in_r = jnp.where(is_start, h0r_ref[pl.ds(b, 1), :], cr_scr[0:1, :])
        h_in_i = jnp.where(is_start, h0i_ref[pl.ds(b, 1), :], ci_scr[0:1, :])
        zero = jnp.zeros((grp, hr_scr.shape[1]), F32)
        er, ei = lax.fori_loop(0, steps, scan_step, (zero, zero), unroll=4)
        pS_r = pr_scr[steps - 1:steps, :]
        pS_i = pi_scr[steps - 1:steps, :]
        cr_rows, ci_rows = [h_in_r], [h_in_i]
        for s in range(grp):
            pr_, pi_ = cr_rows[-1], ci_rows[-1]
            cr_rows.append(er[s:s + 1, :] + pS_r * pr_ - pS_i * pi_)
            ci_rows.append(ei[s:s + 1, :] + pS_r * pi_ + pS_i * pr_)
        c_r = jnp.concatenate(cr_rows[:grp], axis=0)
        c_i = jnp.concatenate(ci_rows[:grp], axis=0)
        cr_scr[0:1, :] = cr_rows[grp]
        ci_scr[0:1, :] = ci_rows[grp]
        hTr_ref[pl.ds(b, 1), :] = cr_rows[grp]
        hTi_ref[pl.ds(b, 1), :] = ci_rows[grp]

        def fix_step(j, carry):
            qr = pr_scr[pl.ds(j, 1), :]
            qi = pi_scr[pl.ds(j, 1), :]
            hr_scr[rows(j), :] = hr_scr[rows(j), :] + (qr * c_r - qi * c_i)
            hi_scr[rows(j), :] = hi_scr[rows(j), :] + (qr * c_i + qi * c_r)
            return carry

        lax.fori_loop(0, steps, fix_step, 0, unroll=4)
    else:
        hr, hi = h0r_ref[...], h0i_ref[...]
        for t in range(steps):
            hr, hi = scan_step(t, (hr, hi))
        hTr_ref[...] = hr
        hTi_ref[...] = hi

    y = _dot(hr_scr[...].astype(BF16), wcr_ref[...].astype(BF16)) + _dot(hi_scr[...].astype(BF16), wci_ref[...].astype(BF16))
    y = _gelu_tanh(y + d_ref[...] * u)
    for hf in range(nhalf):
        y_st[hf] = y[:, hf * LANES:(hf + 1) * LANES]
    for hf in range(nhalf):
        for s in range(grp):
            y_ref[s * steps:(s + 1) * steps, hf * LANES:(hf + 1) * LANES] = y_st[hf, pl.ds(s, steps, stride=grp), :]


def _s5_scan(z, col0, wbr, wbi, wcr, wci, ab_re, ab_im, d_skip, h0_re, h0_im, *, n_seq, seq_rows):
    m = z.shape[0]
    st = S5_SLAB_STATE
    if seq_rows >= 8 * SUBLANES:
        rt = _pick(seq_rows, (512, 256, 128, 64))
        hspec = pl.BlockSpec((n_seq, st), lambda s, n: (0, s))
        pw_rows = rt // SUBLANES
    else:
        rt = seq_rows * _pick(n_seq, (128, 64, 32, 16, 8))
        hspec = pl.BlockSpec((rt // seq_rows, st), lambda s, n: (n, s))
        pw_rows = SUBLANES
    win = S5_SLAB_IN
    cb0 = col0 // win
    kern = functools.partial(_s5_body, rt=rt, seq_rows=seq_rows)
    wspec_b = pl.BlockSpec((None, win, st), lambda s, n: (s, 0, 0))
    wspec_c = pl.BlockSpec((None, st, win), lambda s, n: (s, 0, 0))
    vspec = pl.BlockSpec((1, st), lambda s, n: (0, s))
    return pl.pallas_call(
        kern,
        grid=(S5_SLABS, m // rt),
        in_specs=[pl.BlockSpec((rt, win), lambda s, n: (n, cb0 + s)), wspec_b, wspec_b, wspec_c, wspec_c,
                  vspec, vspec, pl.BlockSpec((1, win), lambda s, n: (0, s)), hspec, hspec],
        out_specs=[pl.BlockSpec((rt, win), lambda s, n: (n, s)), hspec, hspec],
        out_shape=[jax.ShapeDtypeStruct((m, MIX_B), F32),
                   jax.ShapeDtypeStruct((n_seq, S5_GROUPS * S5_STATE), F32),
                   jax.ShapeDtypeStruct((n_seq, S5_GROUPS * S5_STATE), F32)],
        scratch_shapes=[pltpu.VMEM((rt, st), F32), pltpu.VMEM((rt, st), F32),
                        pltpu.VMEM((pw_rows, st), F32), pltpu.VMEM((pw_rows, st), F32),
                        pltpu.VMEM((SUBLANES, st), F32), pltpu.VMEM((SUBLANES, st), F32),
                        pltpu.VMEM((win // LANES, rt, LANES), F32), pltpu.VMEM((win // LANES, rt, LANES), F32)],
        compiler_params=_cparams(("parallel", "arbitrary")),
        name="s5_scan",
    )(z, wbr, wbi, wcr, wci, ab_re.reshape(1, -1), ab_im.reshape(1, -1), d_skip.reshape(1, -1), h0_re, h0_im)


def _glu_body(a_ref, w_ref, b_ref, y_ref, o_ref):
    t = _dot(a_ref[...].astype(BF16), w_ref[...].astype(BF16)) + b_ref[...]
    o_ref[...] = (y_ref[...] * _sigmoid(t)).astype(o_ref.dtype)


def _glu(y, w, b):
    m, k = y.shape
    n = w.shape[1]
    tm = _pick(m, (1024, 512, 256, 128, 64, 32, 16, 8))
    tn = _pick(n, (512, 256, 128))
    return pl.pallas_call(
        _glu_body,
        grid=(m // tm, n // tn),
        in_specs=[pl.BlockSpec((tm, k), lambda i, j: (i, 0)), pl.BlockSpec((k, tn), lambda i, j: (0, j)),
                  pl.BlockSpec((1, tn), lambda i, j: (0, j)), pl.BlockSpec((tm, tn), lambda i, j: (i, j))],
        out_specs=pl.BlockSpec((tm, tn), lambda i, j: (i, j)),
        out_shape=jax.ShapeDtypeStruct((m, n), BF16),
        compiler_params=_cparams(("parallel", "arbitrary")),
        name="s5_glu",
    )(y, w, b.reshape(1, n), y)


def _mlstm_body(q_ref, k_ref, v_ref, og_ref, gt_ref, gb_ref, nw_ref, c0_ref, n0_ref, m0_ref,
                y_ref, cT_ref, nT_ref, mT_ref, c_scr, n_scr, m_scr, *, bb, rt, chunk, valid, unroll):
    h = pl.program_id(0)
    n = pl.program_id(2)
    n_last = pl.num_programs(2) - 1
    L = chunk
    nchunk = rt // L
    rl = _iota((L, L), 0)
    cl = _iota((L, L), 1)
    causal = rl >= cl
    tril = jnp.where(causal, 1.0, 0.0).astype(BF16)
    row = _iota((L, 1), 0)
    lane = _iota((L, LANES), 1)
    gbias = gb_ref[...]
    nw = nw_ref[...]
    scale = MLSTM_DQK ** -0.5

    def load(seq, r0, is_start):
        c_old = jnp.where(is_start, c0_ref[seq, 0], c_scr[seq])
        n_old = jnp.where(is_start, n0_ref[seq, 0], n_scr[seq, 0:1, :])
        m_prev = jnp.where(is_start, m0_ref[seq, 0], m_scr[seq, 0:1, 0:1])
        rows = pl.ds(r0, L)
        return (q_ref[seq, rows, :], k_ref[seq, rows, :], v_ref[seq, rows, :], og_ref[seq, rows, :],
                gt_ref[seq, rows, :], c_old, n_old, m_prev)

    def compute_all(loaded):
        nu = len(loaded)
        igs, lfs = [], []
        for q, k, v, og, gt, c_old, n_old, m_prev in loaded:
            gt = gt + gbias
            ig_raw = jnp.sum(jnp.where(lane == h, gt, 0.0), axis=-1, keepdims=True)
            fg_raw = jnp.sum(jnp.where(lane == MLSTM_HEADS + h, gt, 0.0), axis=-1, keepdims=True)
            ig = GATE_CAP * jnp.tanh(ig_raw / GATE_CAP)
            lf = -_softplus(-(GATE_CAP * jnp.tanh(fg_raw / GATE_CAP)))
            if valid < L:
                ok = row < valid
                ig = jnp.where(ok, ig, NEG_BIG)
                lf = jnp.where(ok, lf, 0.0)
            igs.append(ig)
            lfs.append(lf)
        lf_mat = jnp.zeros((L, LANES), F32)
        for u in range(nu):
            lf_mat = jnp.where(lane == u, lfs[u], lf_mat)
        b_mat = _dot_const(tril, lf_mat)
        pack = b_mat
        for u in range(nu):
            pack = jnp.where(lane == nu + u, igs[u], pack)
        pack_t = pack.T
        gates, qbs, kbs, ks = [], [], [], []
        for u, (q, k, v, og, gt, c_old, n_old, m_prev) in enumerate(loaded):
            bcol = b_mat[:, u:u + 1]
            b_row = pack_t[u:u + 1, :]
            ig_row = pack_t[nu + u:nu + u + 1, :]
            log_d = jnp.where(causal, bcol - b_row + ig_row, NEG_BIG)
            log_p = bcol + m_prev
            m_tok = jnp.maximum(log_p, jnp.max(log_d, axis=-1, keepdims=True))
            gates.append((jnp.exp(log_d - m_tok), jnp.exp(log_p - m_tok), m_tok, bcol))
            k = k * scale
            ks.append(k)
            qbs.append(q.astype(BF16))
            kbs.append(k.astype(BF16))
        ss = [_dot(qb, kb, NT) * g_[0] for qb, kb, g_ in zip(qbs, kbs, gates)]
        svs = [_dot(s.astype(BF16), x[2].astype(BF16)) for s, x in zip(ss, loaded)]
        qcs = [_dot(qb, x[5].astype(BF16), NT) for qb, x in zip(qbs, loaded)]
        upd = []
        for u, (q, k, v, og, gt, c_old, n_old, m_prev) in enumerate(loaded):
            d, p, m_tok, bcol = gates[u]
            m_new = m_tok[L - 1:L, :]
            b_last = bcol[L - 1:L, :]
            w_col = jnp.exp(b_last - bcol + igs[u] - m_new)
            cs = jnp.exp(b_last + m_prev - m_new)
            upd.append((w_col, cs, m_new))
        vks = [_dot((x[2] * w_[0]).astype(BF16), kb, TN) for x, w_, kb in zip(loaded, upd, kbs)]
        out = []
        for u, (q, k, v, og, gt, c_old, n_old, m_prev) in enumerate(loaded):
            d, p, m_tok, bcol = gates[u]
            w_col, cs, m_new = upd[u]
            num = svs[u] + p * qcs[u]
            den = jnp.sum(ss[u], axis=-1, keepdims=True) + p * jnp.sum(q * n_old, axis=-1, keepdims=True)
            hh = num / jnp.maximum(jnp.abs(den), jnp.exp(-m_tok))
            c_new = cs * c_old + vks[u]
            n_new = cs * n_old + jnp.sum(w_col * ks[u], axis=0, keepdims=True)
            hn = hh * lax.rsqrt(jnp.mean(hh * hh, axis=-1, keepdims=True) + RMS_EPS)
            out.append(((hn * nw * _sigmoid(og)).astype(y_ref.dtype), c_new, n_new, m_new))
        return out

    def store(seq, r0, is_end, y, c_new, n_new, m_new):
        y_ref[seq, pl.ds(r0, L), :] = y
        c_scr[seq] = c_new
        n_scr[seq, 0:1, :] = n_new
        m_scr[seq, 0:1, :] = jnp.broadcast_to(m_new, (1, LANES))

        @pl.when(is_end)
        def _():
            cT_ref[seq, 0] = c_new
            nT_ref[seq, 0] = n_new
            mT_ref[seq, 0] = m_new

    def step(it, carry):
        sg = it // nchunk
        ci_ = it - sg * nchunk
        r0 = pl.multiple_of(ci_ * L, L)
        is_start = jnp.logical_and(n == 0, ci_ == 0)
        is_end = jnp.logical_and(n == n_last, ci_ == nchunk - 1)
        seqs = [sg * unroll + u for u in range(unroll)]
        loaded = [load(s, r0, is_start) for s in seqs]
        done = compute_all(loaded)
        for s, d in zip(seqs, done):
            store(s, r0, is_end, *d)
        return carry

    @pl.when(n == 0)
    def _():
        c_scr[...] = jnp.zeros_like(c_scr)
        n_scr[...] = jnp.zeros_like(n_scr)
        m_scr[...] = jnp.zeros_like(m_scr)

    lax.fori_loop(0, (bb // unroll) * nchunk, step, 0)


def _mlstm(z, zt, c0, n0, m0, b_i, b_f, norm_w, *, n_seq, seq_rows, chunk, valid):
    nh, dqk, dv = MLSTM_HEADS, MLSTM_DQK, MLSTM_DV
    unroll = _pick(n_seq, (4, 2, 1))
    if seq_rows > chunk:
        rt = _pick(seq_rows, (256, 128, 64))
        bb = unroll
    else:
        rt = seq_rows
        bb = _pick(n_seq, (8, 4, 2, 1))
    assert rt % chunk == 0 and bb % unroll == 0
    sblk = lambda h, g, n: (g, h, 0, 0)
    gbias = jnp.pad(jnp.concatenate([b_i, b_f]).reshape(1, 2 * nh), ((0, 0), (0, LANES - 2 * nh)))
    n0r = n0.reshape(n_seq, nh, 1, dqk)
    m0r = m0.reshape(n_seq, nh, 1, 1)
    kern = functools.partial(_mlstm_body, bb=bb, rt=rt, chunk=chunk, valid=valid, unroll=unroll)
    y, c_t, n_t, m_t = pl.pallas_call(
        kern,
        grid=(nh, n_seq // bb, seq_rows // rt),
        in_specs=[
            pl.BlockSpec((bb, rt, dqk), lambda h, g, n: (g, n, h)),
            pl.BlockSpec((bb, rt, dqk), lambda h, g, n: (g, n, nh + h)),
            pl.BlockSpec((bb, rt, dv), lambda h, g, n: (g, n, (2 * nh * dqk) // dv + h)),
            pl.BlockSpec((bb, rt, dv), lambda h, g, n: (g, n, h)),
            pl.BlockSpec((bb, rt, LANES), lambda h, g, n: (g, n, (ZO_G - ZO_OG) // LANES)),
            pl.BlockSpec((1, LANES), lambda h, g, n: (0, 0)),
            pl.BlockSpec((1, dv), lambda h, g, n: (0, h)),
            pl.BlockSpec((bb, 1, dv, dqk), sblk),
            pl.BlockSpec((bb, 1, 1, dqk), sblk),
            pl.BlockSpec((bb, 1, 1, 1), sblk),
        ],
        out_specs=[pl.BlockSpec((bb, rt, dv), lambda h, g, n: (g, n, h)),
                   pl.BlockSpec((bb, 1, dv, dqk), sblk),
                   pl.BlockSpec((bb, 1, 1, dqk), sblk),
                   pl.BlockSpec((bb, 1, 1, 1), sblk)],
        out_shape=[jax.ShapeDtypeStruct((n_seq, seq_rows, nh * dv), BF16),
                   jax.ShapeDtypeStruct((n_seq, nh, dv, dqk), F32),
                   jax.ShapeDtypeStruct((n_seq, nh, 1, dqk), F32),
                   jax.ShapeDtypeStruct((n_seq, nh, 1, 1), F32)],
        scratch_shapes=[pltpu.VMEM((bb, dv, dqk), F32), pltpu.VMEM((bb, SUBLANES, dqk), F32),
                        pltpu.VMEM((bb, SUBLANES, LANES), F32)],
        compiler_params=_cparams(("parallel", "parallel", "arbitrary")),
        name="mlstm",
    )(z, z, z, zt, zt, gbias, norm_w.reshape(1, nh * dv), c0, n0r, m0r)
    return y, c_t, n_t.reshape(n_seq, nh, dqk), m_t.reshape(n_seq, nh)


def _pad_cols(a, n):
    return jnp.pad(a, ((0, 0), (0, n - a.shape[1])))


def _pad_rows(a, n):
    return jnp.pad(a, ((0, n - a.shape[0]), (0, 0)))


def _rwkv_cols(a):
    c1 = 3 * MIX_A
    c2 = c1 + RWKV_DECAY_RANK
    c3 = c2 + RWKV_ICL_RANK
    return jnp.concatenate([a[:, :c1], _pad_cols(a[:, c1:c2], RANK_PAD), _pad_cols(a[:, c2:c3], RANK_PAD), a[:, c3:]], axis=1)


def _rwkv_cols_inv(a):
    return jnp.concatenate([a[:, :ZE_DW + RWKV_DECAY_RANK], a[:, ZE_DA:ZE_DA + RWKV_ICL_RANK], a[:, ZE_DG:ZE_U]], axis=1)


def _pad_seq(a, n_seq, t, tp):
    a = a.reshape(n_seq, t, a.shape[1])
    return a if tp == t else jnp.pad(a, ((0, 0), (0, tp - t), (0, 0)))


def _unpad_seq(a, n_seq, t, tp):
    a = a if tp == t else a[:, :t]
    return a.reshape(n_seq * t, a.shape[2])


def _s5_block_weights(zr, zi, b_re, b_im, c_re, c_im):
    g, n, c = S5_GROUPS, S5_STATE, S5_GROUP
    sg = S5_SLAB_GROUPS
    bb_re = zr[..., None] * b_re - zi[..., None] * b_im
    bb_im = zr[..., None] * b_im + zi[..., None] * b_re
    eye = jnp.eye(sg, dtype=F32)

    def in_blocks(bb):
        t = bb.reshape(S5_SLABS, sg, n, c)
        return jnp.einsum('sgnc,gh->sgchn', t, eye).reshape(S5_SLABS, sg * c, sg * n)

    def out_blocks(cc):
        t = cc.reshape(S5_SLABS, sg, c, n)
        return jnp.einsum('sgcn,gh->sgnhc', t, eye).reshape(S5_SLABS, sg * n, sg * c)

    return in_blocks(bb_re), in_blocks(bb_im), out_blocks(c_re), out_blocks(-c_im)


def _trunk(x, n_seq, t, st, w):
    tp = t if t >= MLSTM_CHUNK else SUBLANES * ((t + SUBLANES - 1) // SUBLANES)
    chunk = MLSTM_CHUNK if t >= MLSTM_CHUNK else tp
    valid = chunk if tp == t else t
    assert t % chunk == 0 or tp == chunk

    xn = _rmsnorm(x, w['ln_mix_e'], BF16)
    z, zt = _mm_split(xn, w['w_in_e'], ZE_DW, w['w_in_e_tail'])
    rw = (w['mu'], w['w0'], w['w_up'], w['a0'], w['a_up'], w['g_up'], w['k_k'], w['k_a'], w['r_k'], w['lnx_w'], w['lnx_b'])
    if t < chunk and n_seq == LANES:
        tb = lambda a: a.reshape(n_seq, t, a.shape[1]).transpose(1, 0, 2)
        ya, wkv_t = _rwkv_step(tb(z), tb(zt), _rwkv_cols(st['shift']), st['wkv'].transpose(1, 2, 3, 0), *rw)
        ya = ya.transpose(1, 0, 2).reshape(n_seq * t, MIX_A)
        wkv_t = wkv_t.transpose(3, 0, 1, 2)
    else:
        ya, wkv_t = _rwkv(_pad_seq(z, n_seq, t, tp), _pad_seq(zt, n_seq, t, tp), _rwkv_cols(st['shift']), st['wkv'], *rw,
                          n_seq=n_seq, seq_rows=tp, chunk=chunk, valid=valid)
        ya = _unpad_seq(ya, n_seq, t, tp)
    last = lambda a: a.reshape(n_seq, t, a.shape[1])[:, -1]
    shift_t = _rwkv_cols_inv(jnp.concatenate([last(z), last(zt)[:, :ZT_U]], axis=1))
    yb, re_t, im_t = _s5_scan(zt, ZT_U, w['s5_wbr'], w['s5_wbi'], w['s5_wcr'], w['s5_wci'], w['s5_ab_re'], w['s5_ab_im'],
                              w['s5_d'], st['s5_re'], st['s5_im'], n_seq=n_seq, seq_rows=t)
    yb = _glu(yb, w['s5_glu_w'], w['s5_glu_b'])
    x = _mm2_res(ya, yb, w['w_out_e'], x)
    hn = _rmsnorm(x, w['ln_ffn_e'], BF16)
    hid = _glu_up(hn, w['ffn_w1'], w['ffn_w3'])
    x = _mm_acc(hid, w['ffn_w2'], x)

    xn = _rmsnorm(x, w['ln_mix_o'], BF16)
    zo, zot = _mm_split(xn, w['w_in_o'], ZO_OG, w['w_in_o_tail'])
    yc, c_t, n_t, m_t = _mlstm(_pad_seq(zo, n_seq, t, tp), _pad_seq(zot, n_seq, t, tp), st['c'], st['n'], st['m'], w['b_i'], w['b_f'], w['norm_w'],
                               n_seq=n_seq, seq_rows=tp, chunk=chunk, valid=valid)
    yc = _unpad_seq(yc, n_seq, t, tp)
    x = _mm_res(yc, w['w_out_o'], x)
    hn, comb = _rmsnorm_router(x, w['ln_ffn_o'], w['router_w'], w['router_b'])

    g, ns = S5_GROUPS, S5_STATE
    states = (wkv_t[None], shift_t[None], re_t.reshape(1, n_seq, g, ns), im_t.reshape(1, n_seq, g, ns),
              c_t[None], n_t[None], m_t[None])
    return (x, hn, comb), states


def kernel(x_prompt, x_sample, state_rwkv_wkv, state_rwkv_shift, state_s5_re, state_s5_im, state_mlstm_c, state_mlstm_n, state_mlstm_m, ln_mix_e, w_in_e, rwkv_mu, rwkv_w0, rwkv_w_up, rwkv_a0, rwkv_a_up, rwkv_g_up, rwkv_k_k, rwkv_k_a, rwkv_r_k, rwkv_lnx_w, rwkv_lnx_b, s5_a_re, s5_a_im, s5_log_step, s5_b_re, s5_b_im, s5_c_re, s5_c_im, s5_d, s5_glu_w, s5_glu_b, w_out_e, ln_ffn_e, ffn_w1, ffn_w3, ffn_w2, ln_mix_o, w_in_o, mlstm_b_i, mlstm_b_f, mlstm_norm_w, w_out_o, ln_ffn_o, router_w, router_b, exp_w1, exp_w3, exp_w2, final_norm):
    assert ln_mix_e.shape[0] == 1 and ln_mix_o.shape[0] == 1
    d = D_MODEL
    ab_re, ab_im, zr, zi = _s5_prep(s5_a_re[0], s5_a_im[0], s5_log_step[0])
    wbr, wbi, wcr, wci = _s5_block_weights(zr, zi, s5_b_re[0], s5_b_im[0], s5_c_re[0], s5_c_im[0])
    wo = jnp.swapaxes(w_in_o[0], 0, 1)
    we = jnp.swapaxes(w_in_e[0], 0, 1)
    c_dw = 3 * MIX_A
    c_da = c_dw + RWKV_DECAY_RANK
    c_dg = c_da + RWKV_ICL_RANK
    nh = MLSTM_HEADS
    w = {
        'ln_mix_e': ln_mix_e[0],
        'w_in_e': we,
        'w_in_e_tail': jnp.concatenate([_pad_rows(we[c_dw:c_da], RANK_PAD), _pad_rows(we[c_da:c_dg], RANK_PAD),
                                        we[c_dg:]], axis=0),
        'mu': _rwkv_cols(rwkv_mu[0].reshape(1, RWKV_IN)),
        'w0': rwkv_w0[0], 'a0': rwkv_a0[0],
        'w_up': jnp.pad(rwkv_w_up[0], ((0, RANK_PAD - RWKV_DECAY_RANK), (0, 0))),
        'a_up': jnp.pad(rwkv_a_up[0], ((0, RANK_PAD - RWKV_ICL_RANK), (0, 0))),
        'g_up': rwkv_g_up[0], 'k_k': rwkv_k_k[0], 'k_a': rwkv_k_a[0], 'r_k': rwkv_r_k[0].reshape(MIX_A),
        'lnx_w': rwkv_lnx_w[0], 'lnx_b': rwkv_lnx_b[0],
        's5_wbr': wbr, 's5_wbi': wbi, 's5_wcr': wcr, 's5_wci': wci,
        's5_ab_re': ab_re, 's5_ab_im': ab_im, 's5_d': s5_d[0], 's5_glu_w': s5_glu_w[0], 's5_glu_b': s5_glu_b[0],
        'w_out_e': w_out_e[0], 'ln_ffn_e': ln_ffn_e[0],
        'ffn_w1': ffn_w1[0], 'ffn_w3': ffn_w3[0], 'ffn_w2': ffn_w2[0],
        'ln_mix_o': ln_mix_o[0],
        'w_in_o': wo,
        'w_in_o_tail': jnp.concatenate([wo[ZO_OG + 2 * nh:], _pad_rows(wo[ZO_OG:ZO_OG + 2 * nh], ZO_W - ZO_G)], axis=0),
        'b_i': mlstm_b_i[0], 'b_f': mlstm_b_f[0], 'norm_w': mlstm_norm_w[0],
        'w_out_o': w_out_o[0], 'ln_ffn_o': ln_ffn_o[0],
        'router_w': router_w[0], 'router_b': router_b[0],
        'exp_w1': exp_w1[0], 'exp_w3': exp_w3[0], 'exp_w2': exp_w2[0],
        'final_norm': final_norm,
    }
    bp, tp_, _ = x_prompt.shape
    bs, ts_, _ = x_sample.shape
    g, ns = S5_GROUPS, S5_STATE
    zero = lambda *s: jnp.zeros(s, F32)
    st_p = {'wkv': zero(bp, RWKV_HEADS, RWKV_HEAD, RWKV_HEAD), 'shift': zero(bp, RWKV_IN),
            's5_re': zero(bp, g * ns), 's5_im': zero(bp, g * ns),
            'c': zero(bp, nh, MLSTM_DV, MLSTM_DQK), 'n': zero(bp, nh, MLSTM_DQK), 'm': zero(bp, nh)}
    st_s = {'wkv': state_rwkv_wkv[0], 'shift': state_rwkv_shift[0],
            's5_re': state_s5_re[0].reshape(bs, g * ns), 's5_im': state_s5_im[0].reshape(bs, g * ns),
            'c': state_mlstm_c[0], 'n': state_mlstm_n[0], 'm': state_mlstm_m[0]}
    (x_p, hn_p, comb_p), states_p = _trunk(x_prompt.reshape(bp * tp_, d), bp, tp_, st_p, w)
    (x_s, hn_s, comb_s), states_s = _trunk(x_sample.reshape(bs * ts_, d), bs, ts_, st_s, w)
    y_p, y_s = _moe_sparse([x_p, x_s], [hn_p, hn_s], [comb_p, comb_s], w['exp_w1'], w['exp_w3'], w['exp_w2'], w['final_norm'])
    return (y_p.reshape(bp, tp_, d), y_s.reshape(bs, ts_, d)) + tuple(states_p) + tuple(states_s)
```

```python
import functools
import math

import jax
import jax.numpy as jnp
from jax import lax
from jax.experimental import pallas as pl
from jax.experimental.pallas import tpu as pltpu

F32 = jnp.float32
BF16 = jnp.bfloat16

D_MODEL = 4096
MIX_A = D_MODEL // 2
MIX_B = D_MODEL - MIX_A
RWKV_HEAD = 64
RWKV_HEADS = MIX_A // RWKV_HEAD
RWKV_DECAY_RANK = 96
RWKV_ICL_RANK = 96
RWKV_GATE_RANK = 256
RWKV_IN = 3 * MIX_A + RWKV_DECAY_RANK + RWKV_ICL_RANK + RWKV_GATE_RANK
RWKV_LNX_EPS = 1e-5 * RWKV_HEAD
S5_GROUP = 16
S5_GROUPS = MIX_B // S5_GROUP
S5_STATE = 64
MLSTM_HEADS = 8
MLSTM_DQK = D_MODEL // 16
MLSTM_DV = D_MODEL // 8
MLSTM_CHUNK = 64
GATE_CAP = 15.0
FFN_DIM = 11008
N_EXPERTS = 8
EXPERT_FF = D_MODEL // 2
RMS_EPS = 1e-6

LANES = 128
SUBLANES = 8
VMEM_LIMIT = 56 * 1024 * 1024

RANK_PAD = LANES
ZE_DW = 3 * MIX_A
ZE_DA = ZE_DW + RANK_PAD
ZE_DG = ZE_DA + RANK_PAD
ZE_U = ZE_DG + RWKV_GATE_RANK
ZE_W = ZE_U + MIX_B
ZT_DW = 0
ZT_DA = ZT_DW + RANK_PAD
ZT_DG = ZT_DA + RANK_PAD
ZT_U = ZT_DG + RWKV_GATE_RANK
RW_CAT = 3 * LANES + 2 * RANK_PAD + RWKV_GATE_RANK
ZO_OG = 2 * MLSTM_HEADS * MLSTM_DQK + MLSTM_HEADS * MLSTM_DV
ZO_G = ZO_OG + MLSTM_HEADS * MLSTM_DV
ZO_W = ZO_G + 4 * LANES

S5_SLAB_GROUPS = 16
S5_SLABS = S5_GROUPS // S5_SLAB_GROUPS
S5_SLAB_IN = S5_SLAB_GROUPS * S5_GROUP
S5_SLAB_STATE = S5_SLAB_GROUPS * S5_STATE

NN = (((1,), (0,)), ((), ()))
NT = (((1,), (1,)), ((), ()))
TN = (((0,), (0,)), ((), ()))
NEG_BIG = -1e30


def _cparams(sem):
    return pltpu.CompilerParams(dimension_semantics=sem, vmem_limit_bytes=VMEM_LIMIT)


def _pick(n, cands):
    for c in cands:
        if n % c == 0:
            return c
    raise ValueError(f"no tile for {n}")


def _dot(a, b, dims=NN):
    return lax.dot_general(a, b, dims, preferred_element_type=F32)


def _hp_pair(a):
    hi = a.astype(BF16).astype(F32)
    return hi, a - hi


def _hp_lhs(a, axis=1):
    hi, lo = _hp_pair(a)
    return jnp.concatenate([hi, hi, lo], axis=axis).astype(BF16)


def _hp_rhs(b, axis=0):
    hi, lo = _hp_pair(b)
    return jnp.concatenate([hi, lo, hi], axis=axis).astype(BF16)


def _dot_hp(a, b, dims=NN):
    return _dot(_hp_lhs(a, dims[0][0][0]), _hp_rhs(b, dims[0][1][0]), dims)


def _split3(a):
    h0 = a.astype(BF16).astype(F32)
    r1 = a - h0
    h1 = r1.astype(BF16).astype(F32)
    return h0, h1, r1 - h1


def _dot_const(c_bf16, a):
    return _dot(jnp.concatenate([c_bf16] * 3, axis=1), jnp.concatenate(_split3(a), axis=0).astype(BF16))


def _sigmoid(x):
    return 1.0 / (1.0 + jnp.exp(-x))


def _softplus(x):
    return jnp.maximum(x, 0.0) + jnp.log(1.0 + jnp.exp(-jnp.abs(x)))


def _iota(shape, dim):
    return lax.broadcasted_iota(jnp.int32, shape, dim)


def _rmsnorm_body(x_ref, g_ref, o_ref):
    x = x_ref[...]
    ms = jnp.mean(x * x, axis=-1, keepdims=True)
    o_ref[...] = (x * lax.rsqrt(ms + RMS_EPS) * g_ref[...]).astype(o_ref.dtype)


def _rmsnorm(x, g, out_dtype):
    m, d = x.shape
    tm = _pick(m, (256, 128, 64, 32, 16, 8))
    return pl.pallas_call(
        _rmsnorm_body,
        grid=(m // tm,),
        in_specs=[pl.BlockSpec((tm, d), lambda i: (i, 0)), pl.BlockSpec((1, d), lambda i: (0, 0))],
        out_specs=pl.BlockSpec((tm, d), lambda i: (i, 0)),
        out_shape=jax.ShapeDtypeStruct((m, d), out_dtype),
        compiler_params=_cparams(("parallel",)),
        name="rmsnorm",
    )(x, g.reshape(1, d))


def _router_body(x_ref, g_ref, rw_ref, rb_ref, hn_ref, comb_ref):
    x = x_ref[...]
    ms = jnp.mean(x * x, axis=-1, keepdims=True)
    h = x * lax.rsqrt(ms + RMS_EPS) * g_ref[...]
    hn_ref[...] = h.astype(hn_ref.dtype)
    logits = _dot_hp(h, rw_ref[...]) + rb_ref[...]
    lane = _iota(logits.shape, 1)
    logits = jnp.where(lane < N_EXPERTS, logits, NEG_BIG)
    m1 = jnp.max(logits, axis=-1, keepdims=True)
    i1 = jnp.min(jnp.where(logits == m1, lane, LANES), axis=-1, keepdims=True)
    rest = jnp.where(lane == i1, NEG_BIG, logits)
    m2 = jnp.max(rest, axis=-1, keepdims=True)
    i2 = jnp.min(jnp.where(rest == m2, lane, LANES), axis=-1, keepdims=True)
    e = jnp.exp(m2 - m1)
    g1 = 1.0 / (1.0 + e)
    g2 = e / (1.0 + e)
    comb_ref[...] = jnp.where(lane == i1, g1, 0.0) + jnp.where(lane == i2, g2, 0.0)


def _rmsnorm_router(x, g, router_w, router_b):
    m, d = x.shape
    tm = _pick(m, (256, 128, 64, 32, 16, 8))
    rw = jnp.pad(router_w, ((0, 0), (0, LANES - N_EXPERTS)))
    rb = jnp.pad(router_b.reshape(1, N_EXPERTS), ((0, 0), (0, LANES - N_EXPERTS)))
    return pl.pallas_call(
        _router_body,
        grid=(m // tm,),
        in_specs=[pl.BlockSpec((tm, d), lambda i: (i, 0)), pl.BlockSpec((1, d), lambda i: (0, 0)),
                  pl.BlockSpec((d, LANES), lambda i: (0, 0)), pl.BlockSpec((1, LANES), lambda i: (0, 0))],
        out_specs=[pl.BlockSpec((tm, d), lambda i: (i, 0)), pl.BlockSpec((tm, LANES), lambda i: (i, 0))],
        out_shape=[jax.ShapeDtypeStruct((m, d), BF16), jax.ShapeDtypeStruct((m, LANES), F32)],
        compiler_params=_cparams(("parallel",)),
        name="rmsnorm_router",
    )(x, g.reshape(1, d), rw, rb)


def _mm_body(a_ref, w_ref, o_ref):
    o_ref[...] = _dot(a_ref[...], w_ref[...].astype(BF16)).astype(o_ref.dtype)


def _mm(a, w, n_out=None):
    m, k = a.shape
    n = w.shape[1] if n_out is None else n_out
    tm = _pick(m, (1024, 512, 256, 128, 64, 32, 16, 8))
    tn = _pick(n, (512, 256, 128))
    return pl.pallas_call(
        _mm_body,
        grid=(m // tm, n // tn),
        in_specs=[pl.BlockSpec((tm, k), lambda i, j: (i, 0)), pl.BlockSpec((k, tn), lambda i, j: (0, j))],
        out_specs=pl.BlockSpec((tm, tn), lambda i, j: (i, j)),
        out_shape=jax.ShapeDtypeStruct((m, n), F32),
        compiler_params=_cparams(("parallel", "arbitrary")),
        name="proj_in",
    )(a, w)


def _mm_nt_body(a_ref, wt_ref, o_ref):
    o_ref[...] = _dot(a_ref[...], wt_ref[...].astype(BF16), NT)


def _mm_nt(a, wt, n_rows):
    m, k = a.shape
    tm = _pick(m, (1024, 512, 256, 128, 64, 32, 16, 8))
    tn = 4 * LANES
    assert n_rows % tn == 0
    return pl.pallas_call(
        _mm_nt_body,
        grid=(m // tm, n_rows // tn),
        in_specs=[pl.BlockSpec((tm, k), lambda i, j: (i, 0)), pl.BlockSpec((tn, k), lambda i, j: (j, 0))],
        out_specs=pl.BlockSpec((tm, tn), lambda i, j: (i, j)),
        out_shape=jax.ShapeDtypeStruct((m, n_rows), F32),
        compiler_params=_cparams(("parallel", "arbitrary")),
        name="proj_in",
    )(a, wt)


def _mm_split(a, w_t, n_main, w_tail_t):
    return _mm_nt(a, w_t, n_main), _mm_nt(a, w_tail_t, w_tail_t.shape[0])


def _mm2_res_body(a0_ref, a1_ref, w0_ref, w1_ref, r_ref, o_ref):
    acc = _dot(a0_ref[...], w0_ref[...].astype(BF16)) + _dot(a1_ref[...], w1_ref[...].astype(BF16))
    o_ref[...] = r_ref[...] + acc


def _mm2_res(a0, a1, w, res):
    m, k0 = a0.shape
    k1 = a1.shape[1]
    assert k0 == k1
    n = w.shape[1]
    tm = _pick(m, (1024, 512, 256, 128, 64, 32, 16, 8))
    tn = _pick(n, (512, 256, 128))
    return pl.pallas_call(
        _mm2_res_body,
        grid=(m // tm, n // tn),
        in_specs=[pl.BlockSpec((tm, k0), lambda i, j: (i, 0)), pl.BlockSpec((tm, k1), lambda i, j: (i, 0)),
                  pl.BlockSpec((k0, tn), lambda i, j: (0, j)), pl.BlockSpec((k1, tn), lambda i, j: (1, j)),
                  pl.BlockSpec((tm, tn), lambda i, j: (i, j))],
        out_specs=pl.BlockSpec((tm, tn), lambda i, j: (i, j)),
        out_shape=jax.ShapeDtypeStruct((m, n), F32),
        compiler_params=_cparams(("parallel", "arbitrary")),
        name="proj_out",
    )(a0, a1, w, w, res)


def _mm_res_body(a_ref, w_ref, r_ref, o_ref):
    o_ref[...] = r_ref[...] + _dot(a_ref[...], w_ref[...].astype(BF16))


def _mm_res(a, w, res):
    m, k = a.shape
    n = w.shape[1]
    tm = _pick(m, (1024, 512, 256, 128, 64, 32, 16, 8))
    tn = _pick(n, (512, 256, 128))
    return pl.pallas_call(
        _mm_res_body,
        grid=(m // tm, n // tn),
        in_specs=[pl.BlockSpec((tm, k), lambda i, j: (i, 0)), pl.BlockSpec((k, tn), lambda i, j: (0, j)),
                  pl.BlockSpec((tm, tn), lambda i, j: (i, j))],
        out_specs=pl.BlockSpec((tm, tn), lambda i, j: (i, j)),
        out_shape=jax.ShapeDtypeStruct((m, n), F32),
        compiler_params=_cparams(("parallel", "arbitrary")),
        name="proj_out1",
    )(a, w, res)


def _glu_up_body(a_ref, w1_ref, w3_ref, o_ref):
    a = a_ref[...]
    h1 = _dot(a, w1_ref[...].astype(BF16))
    h3 = _dot(a, w3_ref[...].astype(BF16))
    o_ref[...] = (h1 * _sigmoid(h1) * h3).astype(o_ref.dtype)


def _glu_up(a, w1, w3):
    m, k = a.shape
    n = w1.shape[1]
    tm = _pick(m, (1024, 512, 256, 128, 64, 32, 16, 8))
    tn = _pick(n, (256, 128))
    return pl.pallas_call(
        _glu_up_body,
        grid=(m // tm, n // tn),
        in_specs=[pl.BlockSpec((tm, k), lambda i, j: (i, 0)), pl.BlockSpec((k, tn), lambda i, j: (0, j)),
                  pl.BlockSpec((k, tn), lambda i, j: (0, j))],
        out_specs=pl.BlockSpec((tm, tn), lambda i, j: (i, j)),
        out_shape=jax.ShapeDtypeStruct((m, n), BF16),
        compiler_params=_cparams(("parallel", "arbitrary")),
        name="ffn_up",
    )(a, w1, w3)


def _mm_acc_body(a_ref, w_ref, r_ref, o_ref, *, k_total, tk, rc):
    kk = pl.program_id(2)
    ragged = k_total % tk != 0
    valid = k_total - kk * tk
    w = w_ref[...]
    if ragged:
        w = jnp.where(_iota(w.shape, 0) < valid, w, 0.0)
    w = w.astype(BF16)

    @pl.when(kk == 0)
    def _():
        o_ref[...] = r_ref[...]

    a = a_ref[...]
    if ragged:
        a = jnp.where(_iota(a.shape, 1) < valid, a, jnp.zeros_like(a))
    for c0 in range(0, w.shape[1], rc):
        o_ref[:, c0:c0 + rc] += _dot(a, w[:, c0:c0 + rc])


def _mm_acc(a, w, res):
    m, k = a.shape
    n = w.shape[1]
    tm = _pick(m, (2048, 1024, 512, 256, 128, 64, 32, 16, 8))
    tn = _pick(n, (1024, 512, 256, 128))
    tk = 512
    nk = pl.cdiv(k, tk)
    return pl.pallas_call(
        functools.partial(_mm_acc_body, k_total=k, tk=tk, rc=min(tn, 2 * LANES)),
        grid=(m // tm, n // tn, nk),
        in_specs=[pl.BlockSpec((tm, tk), lambda i, j, kk: (i, kk)), pl.BlockSpec((tk, tn), lambda i, j, kk: (kk, j)),
                  pl.BlockSpec((tm, tn), lambda i, j, kk: (i, j))],
        out_specs=pl.BlockSpec((tm, tn), lambda i, j, kk: (i, j)),
        out_shape=jax.ShapeDtypeStruct((m, n), F32),
        compiler_params=_cparams(("parallel", "parallel", "arbitrary")),
        name="proj_down",
    )(a, w, res)


MOE_TM = 512
MOE_BLK = 256
MOE_ALIGN = 16


def _moe_rank_body(comb_ref, rank_ref, cnt_ref):
    comb = comb_ref[...]
    tt = comb.shape[0]
    lane = _iota(comb.shape, 1)
    sel = jnp.where((comb > 0.0) & (lane < N_EXPERTS), 1.0, 0.0)
    below = jnp.where(_iota((tt, tt), 0) > _iota((tt, tt), 1), 1.0, 0.0).astype(BF16)
    rank = _dot(below, sel.astype(BF16))
    rank_ref[...] = rank
    cnt_ref[...] = rank[tt - 1:tt, :] + sel[tt - 1:tt, :]


def _moe_rank(comb, tt):
    m = comb.shape[0]
    nt = m // tt
    return pl.pallas_call(
        _moe_rank_body,
        grid=(nt,),
        in_specs=[pl.BlockSpec((tt, LANES), lambda i: (i, 0))],
        out_specs=[pl.BlockSpec((tt, LANES), lambda i: (i, 0)), pl.BlockSpec((None, 1, LANES), lambda i: (i, 0, 0))],
        out_shape=[jax.ShapeDtypeStruct((m, LANES), F32), jax.ShapeDtypeStruct((nt, 1, LANES), F32)],
        compiler_params=_cparams(("parallel",)),
        name="moe_rank",
    )(comb)


def _moe_dispatch_body(seg_ref, base_ref, x_ref, rank_ref, comb_ref, xg_in_ref, xg_ref, buf, sem, *, tt):
    del xg_in_ref
    i = pl.program_id(0)
    x = x_ref[...]
    rank_t = rank_ref[...].T
    comb_t = comb_ref[...].T
    rr = _iota((MOE_BLK, tt), 0)

    def copy(slot, row0):
        return pltpu.make_async_copy(buf.at[slot], xg_ref.at[pl.ds(pl.multiple_of(row0, MOE_ALIGN), MOE_BLK)], sem.at[slot])

    def block(e, b, slot, base_e):
        hit = (rank_t[e:e + 1, :] == (rr + b * MOE_BLK).astype(F32)) & (comb_t[e:e + 1, :] > 0.0)
        onehot = jnp.where(hit, 1.0, 0.0).astype(BF16)
        buf[slot] = _dot(onehot, x).astype(buf.dtype)
        copy(slot, base_e + b * MOE_BLK).start()

    for e in range(N_EXPERTS):
        slot = e % 2
        seg_e = seg_ref[i * N_EXPERTS + e]
        base_e = base_ref[i * N_EXPERTS + e]
        if e >= 2:
            copy(slot, 0).wait()
        block(e, 0, slot, base_e)

        def extra(b, carry, e=e, slot=slot, base_e=base_e):
            copy(slot, 0).wait()
            block(e, b, slot, base_e)
            return carry

        lax.fori_loop(1, (seg_e + MOE_BLK - 1) // MOE_BLK, extra, 0)
    copy(0, 0).wait()
    copy(1, 0).wait()


def _moe_dispatch(hn, rank, comb, seg, base, p_max, tt):
    m, d = hn.shape
    return pl.pallas_call(
        functools.partial(_moe_dispatch_body, tt=tt),
        grid_spec=pltpu.PrefetchScalarGridSpec(
            num_scalar_prefetch=2,
            grid=(m // tt,),
            in_specs=[pl.BlockSpec((tt, d), lambda i, *_: (i, 0)), pl.BlockSpec((tt, LANES), lambda i, *_: (i, 0)),
                      pl.BlockSpec((tt, LANES), lambda i, *_: (i, 0)), pl.BlockSpec(memory_space=pl.ANY)],
            out_specs=pl.BlockSpec(memory_space=pl.ANY),
            scratch_shapes=[pltpu.VMEM((2, MOE_BLK, d), BF16), pltpu.SemaphoreType.DMA((2,))],
        ),
        out_shape=jax.ShapeDtypeStruct((p_max, d), BF16),
        input_output_aliases={5: 0},
        compiler_params=_cparams(("arbitrary",)),
        name="moe_dispatch",
    )(seg, base, hn, rank, comb, jnp.zeros((p_max, d), BF16))


def _moe_gup_body(te_ref, nv_ref, x_ref, w1_ref, w3_ref, o_ref):
    k = pl.program_id(1)

    @pl.when(k < nv_ref[0])
    def _():
        x = x_ref[...]
        h1 = _dot(x, w1_ref[...].astype(BF16))
        h3 = _dot(x, w3_ref[...].astype(BF16))
        o_ref[...] = (h1 * _sigmoid(h1) * h3).astype(o_ref.dtype)

    @pl.when(k >= nv_ref[0])
    def _():
        o_ref[...] = jnp.zeros_like(o_ref)


def _moe_gdown_body(te_ref, nv_ref, h_ref, w_ref, o_ref):
    k = pl.program_id(1)

    @pl.when(k < nv_ref[0])
    def _():
        o_ref[...] = _dot(h_ref[...], w_ref[...].astype(BF16)).astype(o_ref.dtype)

    @pl.when(k >= nv_ref[0])
    def _():
        o_ref[...] = jnp.zeros_like(o_ref)


def _moe_grouped(xg, te, nv, w1, w3, w2):
    p, d = xg.shape
    _, _, f = w1.shape
    tm = MOE_TM
    nk = p // tm
    tn = 4 * LANES
    rowi = lambda k, nv_: jnp.minimum(k, nv_[0] - 1)
    hid = pl.pallas_call(
        _moe_gup_body,
        grid_spec=pltpu.PrefetchScalarGridSpec(
            num_scalar_prefetch=2,
            grid=(f // tn, nk),
            in_specs=[pl.BlockSpec((tm, d), lambda j, k, te_, nv_: (rowi(k, nv_), 0)),
                      pl.BlockSpec((None, d, tn), lambda j, k, te_, nv_: (te_[rowi(k, nv_)], 0, j)),
                      pl.BlockSpec((None, d, tn), lambda j, k, te_, nv_: (te_[rowi(k, nv_)], 0, j))],
            out_specs=pl.BlockSpec((tm, tn), lambda j, k, te_, nv_: (k, j)),
        ),
        out_shape=jax.ShapeDtypeStruct((p, f), BF16),
        compiler_params=_cparams(("arbitrary", "arbitrary")),
        name="moe_up",
    )(te, nv, xg, w1, w3)
    tn2 = 8 * LANES
    return pl.pallas_call(
        _moe_gdown_body,
        grid_spec=pltpu.PrefetchScalarGridSpec(
            num_scalar_prefetch=2,
            grid=(d // tn2, nk),
            in_specs=[pl.BlockSpec((tm, f), lambda j, k, te_, nv_: (rowi(k, nv_), 0)),
                      pl.BlockSpec((None, f, tn2), lambda j, k, te_, nv_: (te_[rowi(k, nv_)], 0, j))],
            out_specs=pl.BlockSpec((tm, tn2), lambda j, k, te_, nv_: (k, j)),
        ),
        out_shape=jax.ShapeDtypeStruct((p, d), BF16),
        compiler_params=_cparams(("arbitrary", "arbitrary")),
        name="moe_down",
    )(te, nv, hid, w2)


def _moe_combine_body(seg_ref, base_ref, x_ref, rank_ref, comb_ref, g_ref, y_ref, o_ref, buf, sem, *, tt, tile0):
    i = pl.program_id(0) + tile0
    rank = rank_ref[...]
    comb = comb_ref[...]
    cc = _iota((tt, MOE_BLK), 1)

    def copy(slot, row0):
        return pltpu.make_async_copy(y_ref.at[pl.ds(pl.multiple_of(row0, MOE_ALIGN), MOE_BLK)], buf.at[slot], sem.at[slot])

    def gathered(e, b, slot):
        hit = (rank[:, e:e + 1] == (cc + b * MOE_BLK).astype(F32)) & (comb[:, e:e + 1] > 0.0)
        return comb[:, e:e + 1] * _dot(jnp.where(hit, 1.0, 0.0).astype(BF16), buf[slot])

    o_ref[...] = x_ref[...]
    copy(0, base_ref[i * N_EXPERTS]).start()
    for e in range(N_EXPERTS):
        slot = e % 2
        seg_e = seg_ref[i * N_EXPERTS + e]
        base_e = base_ref[i * N_EXPERTS + e]
        if e + 1 < N_EXPERTS:
            copy(1 - slot, base_ref[i * N_EXPERTS + e + 1]).start()
        copy(slot, 0).wait()
        o_ref[...] += gathered(e, 0, slot)

        def extra(b, carry, e=e, slot=slot, base_e=base_e):
            copy(slot, base_e + b * MOE_BLK).start()
            copy(slot, 0).wait()
            o_ref[...] += gathered(e, b, slot)
            return carry

        lax.fori_loop(1, (seg_e + MOE_BLK - 1) // MOE_BLK, extra, 0)
    acc = o_ref[...]
    ms = jnp.mean(acc * acc, axis=-1, keepdims=True)
    o_ref[...] = acc * lax.rsqrt(ms + RMS_EPS) * g_ref[...]


def _moe_combine(x, rank, comb, y, seg, base, tt, norm_g, tile0):
    m, d = x.shape
    return pl.pallas_call(
        functools.partial(_moe_combine_body, tt=tt, tile0=tile0),
        grid_spec=pltpu.PrefetchScalarGridSpec(
            num_scalar_prefetch=2,
            grid=(m // tt,),
            in_specs=[pl.BlockSpec((tt, d), lambda i, *_: (i, 0)), pl.BlockSpec((tt, LANES), lambda i, *_: (i + tile0, 0)),
                      pl.BlockSpec((tt, LANES), lambda i, *_: (i + tile0, 0)), pl.BlockSpec((1, d), lambda i, *_: (0, 0)),
                      pl.BlockSpec(memory_space=pl.ANY)],
            out_specs=pl.BlockSpec((tt, d), lambda i, *_: (i, 0)),
            scratch_shapes=[pltpu.VMEM((2, MOE_BLK, d), BF16), pltpu.SemaphoreType.DMA((2,))],
        ),
        out_shape=jax.ShapeDtypeStruct((m, d), F32),
        compiler_params=_cparams(("arbitrary",)),
        name="moe_combine",
    )(seg, base, x, rank, comb, norm_g.reshape(1, d), y)


def _moe_sparse(xs, hns, combs, w1, w3, w2, norm_g):
    hn = jnp.concatenate(hns, axis=0)
    comb = jnp.concatenate(combs, axis=0)
    m, d = hn.shape
    ne = N_EXPERTS
    tt = _pick(math.gcd(*[x.shape[0] for x in xs]), (512, 256, 128, 64, 32, 16))
    nt = m // tt
    tm = MOE_TM
    rank, cnt = _moe_rank(comb, tt)
    cnt = cnt[:, 0, :ne].astype(jnp.int32)
    seg = (cnt + MOE_ALIGN - 1) // MOE_ALIGN * MOE_ALIGN
    grp = (jnp.sum(seg, axis=0) + MOE_BLK + tm - 1) // tm * tm
    ends = jnp.cumsum(grp)
    base = (ends - grp)[None, :] + jnp.cumsum(seg, axis=0) - seg
    p_max = (2 * m + nt * ne * MOE_ALIGN + ne * (MOE_BLK + tm) + tm - 1) // tm * tm
    nk = p_max // tm
    te = jnp.minimum(jnp.sum(ends[None, :] <= (jnp.arange(nk, dtype=jnp.int32) * tm)[:, None], axis=1), ne - 1).astype(jnp.int32)
    nv = (ends[-1] // tm).astype(jnp.int32).reshape(1)
    seg = seg.reshape(-1)
    base = base.astype(jnp.int32).reshape(-1)
    xg = _moe_dispatch(hn, rank, comb, seg, base, p_max, tt)
    y = _moe_grouped(xg, te, nv, w1, w3, w2)
    outs, tile0 = [], 0
    for x in xs:
        outs.append(_moe_combine(x, rank, comb, y, seg, base, tt, norm_g, tile0))
        tile0 += x.shape[0] // tt
    return outs


def _rwkv_body(zr_ref, zk_ref, zv_ref, zdw_ref, zda_ref, zdg_ref, mu_ref, sh_ref, s0_ref,
               w0_ref, wup_ref, a0_ref, aup_ref, gup_ref, kk_ref, ka_ref, rk_ref, lw_ref, lb_ref,
               y_ref, sT_ref, s_scr, carry_scr, *, bb, rt, chunk, valid, unroll, npp):
    n = pl.program_id(2)
    n_last = pl.num_programs(2) - 1
    L = chunk
    L2 = 2 * L
    H = RWKV_HEAD
    nchunk = rt // L
    nsq = max(1, int(math.ceil(math.log2(L))))

    lane = _iota((1, LANES), 1)
    m0 = (lane < H).astype(F32)
    m1 = 1.0 - m0
    r2 = _iota((L2, L2), 0)
    c2 = _iota((L2, L2), 1)
    rh = jnp.where(r2 >= L, 1, 0)
    ch = jnp.where(c2 >= L, 1, 0)
    same = jnp.where(rh == ch, 1.0, 0.0)
    tdiff = (r2 - L * rh) - (c2 - L * ch)
    strict = same * jnp.where(tdiff > 0, 1.0, 0.0)
    incl = same * jnp.where(tdiff >= 0, 1.0, 0.0)
    rl = _iota((L, L), 0)
    cl = _iota((L, L), 1)
    tril = jnp.where(rl >= cl, 1.0, 0.0).astype(BF16)
    ri = _iota((LANES, LANES), 0)
    ci = _iota((LANES, LANES), 1)
    bd = jnp.where((ri >= H) == (ci >= H), 1.0, 0.0)
    bones = bd.astype(BF16)
    row = _iota((L, 1), 0)

    @pl.when(n == 0)
    def _():
        s_scr[...] = jnp.zeros_like(s_scr)
        carry_scr[...] = jnp.zeros_like(carry_scr)

    lanes = [slice(p * LANES, (p + 1) * LANES) for p in range(npp)]
    pw = [dict(mu=mu_ref[p], w0=w0_ref[:, ls], a0=a0_ref[:, ls], wup=_hp_rhs(wup_ref[:, ls]), aup=_hp_rhs(aup_ref[:, ls]),
               gup=_hp_rhs(gup_ref[:, ls]), kk=kk_ref[:, ls], ka=ka_ref[:, ls], rk=rk_ref[:, ls], lnw=lw_ref[:, ls],
               lnb=lb_ref[:, ls]) for p, ls in enumerate(lanes)]

    def block_sum(x):
        return _dot_const_rhs(x, bones)

    def load(seq, p, r0, is_start):
        rows = pl.ds(r0, L)
        zc = jnp.concatenate([zr_ref[seq, rows, lanes[p]], zk_ref[seq, rows, lanes[p]], zv_ref[seq, rows, lanes[p]],
                              zdw_ref[seq, rows, :], zda_ref[seq, rows, :], zdg_ref[seq, rows, :]], axis=1)
        first = jnp.where(is_start, sh_ref[p, pl.ds(seq, 1), :], carry_scr[seq, p, 0:1, :])
        s0 = s0_ref[seq, 2 * p]
        s1 = s0_ref[seq, 2 * p + 1]
        zz = jnp.zeros((H, H), F32)
        s_init = jnp.concatenate([jnp.concatenate([s0, zz], axis=1), jnp.concatenate([zz, s1], axis=1)], axis=0)
        st = jnp.where(is_start, s_init, s_scr[seq, p])
        return zc, first, st

    def recur(seqs):
        pre = []
        for r, k, v, ka, kb, lw, cum, st in seqs:
            ce = jnp.exp(cum)
            cinv = jnp.exp(-cum)
            at = jnp.exp(cum - lw) * ka
            rt_ = ce * r
            bt = kb * cinv
            kt = k * cinv
            c_last = ce[L - 1:L, :]
            lhs = jnp.concatenate([at * m0, at * m1, rt_ * m0, rt_ * m1], axis=0)
            rhs = jnp.concatenate([bt * m0, bt * m1, kt * m0, kt * m1, st], axis=0)
            bk = jnp.concatenate([bt * c_last, kt * c_last], axis=0)
            pre.append((lhs, rhs, bk, c_last))
        gms = [_dot_x(p_[0], p_[1], NT) for p_ in pre]
        wvs = [_dot_x(jnp.concatenate([gm[0:L2, L2:2 * L2] * strict, gm[L2:, L2:2 * L2] * incl], axis=0),
                      jnp.concatenate([s[2], s[2]], axis=0)) for gm, s in zip(gms, seqs)]
        xs = [gm[0:L2, 2 * L2:] + wv[0:L2] for gm, wv in zip(gms, wvs)]
        ps = [gm[0:L2, 0:L2] * strict for gm in gms]
        for q in range(nsq):
            if q + 1 < nsq:
                xps = [_dot_x(p, jnp.concatenate([x, p], axis=1)) for x, p in zip(xs, ps)]
                xs = [x + xp[:, 0:LANES] for x, xp in zip(xs, xps)]
                ps = [xp[:, LANES:] for xp in xps]
            else:
                xs = [x + _dot_x(p, x) for x, p in zip(xs, ps)]
        us = [x[0:L, :] * m0 + x[L:L2, :] * m1 for x in xs]
        rus = [_dot_x(gm[L2:, 0:L2] * incl, jnp.concatenate([u, u], axis=0)) for gm, u in zip(gms, us)]
        sus = [_dot_x(jnp.concatenate([u, s[2]], axis=0), p_[2], TN) for u, s, p_ in zip(us, seqs, pre)]
        out = []
        for gm, wv, ru, su, s, p_ in zip(gms, wvs, rus, sus, seqs, pre):
            y2 = gm[L2:, 2 * L2:] + wv[L2:] + ru
            out.append((y2[0:L, :] * m0 + y2[L:L2, :] * m1, s[7] * p_[3] + bd * su))
        return out

    part = lambda x, u: x[u * L:(u + 1) * L]

    def prepare(p, loaded):
        nu = len(loaded)
        c = pw[p]
        stack = lambda xs: xs[0] if nu == 1 else jnp.concatenate(xs, axis=0)
        zc = stack([x[0] for x in loaded])
        zprev = stack([jnp.where(row == 0, x[1], pltpu.roll(x[0], 1, 0)) for x in loaded])
        zs = zc + c['mu'] * (zprev - zc)
        r = zs[:, 0:LANES]
        k = zs[:, LANES:2 * LANES]
        v = zs[:, 2 * LANES:3 * LANES]
        dw = zs[:, 3 * LANES:4 * LANES]
        da = zs[:, 4 * LANES:5 * LANES]
        dg = zs[:, 5 * LANES:]

        w_log = -_softplus(-(c['w0'] + _dot(_hp_lhs(jnp.tanh(dw)), c['wup']))) - 0.5
        lw = -jnp.exp(w_log)
        a = _sigmoid(c['a0'] + _dot(_hp_lhs(da), c['aup']))
        g = _dot(_hp_lhs(_sigmoid(dg)), c['gup'])
        kk = k * c['kk']
        k = k * (1.0 + (a - 1.0) * c['ka'])
        rows = nu * L
        sums = block_sum(jnp.concatenate([kk * kk, r * k * c['rk']], axis=0))
        kk = kk / jnp.maximum(jnp.sqrt(sums[0:rows]), 1e-12)
        bonus = sums[rows:] * v
        ka = -kk
        kb = kk * a
        if valid < L:
            ok = stack([row < valid] * nu)
            lw = jnp.where(ok, lw, 0.0)
            r = jnp.where(ok, r, 0.0)
            k = jnp.where(ok, k, 0.0)
            v = jnp.where(ok, v, 0.0)
            ka = jnp.where(ok, ka, 0.0)
            kb = jnp.where(ok, kb, 0.0)

        lw_wide = lw if nu == 1 else jnp.concatenate([part(lw, u) for u in range(nu)], axis=1)
        cum_wide = _dot_const(tril, lw_wide)
        chains = [(part(r, u), part(k, u), part(v, u), part(ka, u), part(kb, u), part(lw, u),
                   cum_wide[:, u * LANES:(u + 1) * LANES], loaded[u][2]) for u in range(nu)]
        return chains, bonus, g

    def finish(p, loaded, res, bonus, g):
        nu = len(loaded)
        c = pw[p]
        y = res[0][0] if nu == 1 else jnp.concatenate([x[0] for x in res], axis=0)
        mean = block_sum(y) * (1.0 / H)
        yc = y - mean
        var = block_sum(yc * yc) * (1.0 / H)
        yn = yc * lax.rsqrt(var + RWKV_LNX_EPS) * c['lnw'] + c['lnb']
        out = (yn + bonus) * g
        return [(part(out, u).astype(y_ref.dtype), res[u][1], loaded[u][0][valid - 1:valid, :]) for u in range(nu)]

    def store(seq, p, r0, is_end, y, st, last_row):
        y_ref[seq, pl.ds(r0, L), lanes[p]] = y
        s_scr[seq, p] = st
        carry_scr[seq, p, 0:1, :] = last_row

        @pl.when(is_end)
        def _():
            sT_ref[seq, 2 * p] = st[0:H, 0:H]
            sT_ref[seq, 2 * p + 1] = st[H:2 * H, H:2 * H]

    def step(it, carry):
        sg = it // nchunk
        ci_ = it - sg * nchunk
        r0 = pl.multiple_of(ci_ * L, L)
        is_start = jnp.logical_and(n == 0, ci_ == 0)
        is_end = jnp.logical_and(n == n_last, ci_ == nchunk - 1)
        seqs = [sg * unroll + u for u in range(unroll)]
        loaded = [[load(s, p, r0, is_start) for s in seqs] for p in range(npp)]
        prepared = [prepare(p, loaded[p]) for p in range(npp)]
        res = recur([ch for pr in prepared for ch in pr[0]])
        for p in range(npp):
            done = finish(p, loaded[p], res[p * unroll:(p + 1) * unroll], prepared[p][1], prepared[p][2])
            for s, d in zip(seqs, done):
                store(s, p, r0, is_end, *d)
        return carry

    lax.fori_loop(0, (bb // unroll) * nchunk, step, 0)


def _dot_x(a, b, dims=NN):
    return _dot(a.astype(BF16), b.astype(BF16), dims)


def _dot_const_rhs(a, c_bf16):
    return _dot(jnp.concatenate(_split3(a), axis=1).astype(BF16), jnp.concatenate([c_bf16] * 3, axis=0))


def _rwkv(z, zt, shift_p, wkv0, mu_p, w0, wup_p, a0, aup_p, g_up, k_k, k_a, r_k, lnx_w, lnx_b,
          *, n_seq, seq_rows, chunk, valid):
    unroll = _pick(n_seq, (4, 2, 1))
    if seq_rows > chunk:
        rt = _pick(seq_rows, (256, 128, 64))
        bb = unroll
    else:
        rt = seq_rows
        bb = _pick(n_seq, (32, 16, 8, 4, 2, 1))
    assert rt % chunk == 0 and bb % unroll == 0 and (bb == n_seq or bb % SUBLANES == 0)
    sblk = lambda c, g, n: (g, c, 0, 0)
    npair = RWKV_HEADS // 2
    nrb = LANES

    def cat(a):
        rows = a.shape[0]
        rkv = a[:, :3 * MIX_A].reshape(rows, 3, npair, LANES).transpose(2, 0, 1, 3).reshape(npair, rows, 3 * LANES)
        tail = jnp.broadcast_to(a[None, :, 3 * MIX_A:], (npair, rows, ZE_U - 3 * MIX_A))
        return jnp.concatenate([rkv, tail], axis=2)

    mu_cat = cat(mu_p.reshape(1, ZE_U))
    sh_cat = cat(shift_p)
    row2 = lambda a: a.reshape(1, MIX_A)
    npp = 2
    nrb = npp * LANES
    nblk = npair // npp
    vec = pl.BlockSpec((1, nrb), lambda c, g, n: (0, c))
    kern = functools.partial(_rwkv_body, bb=bb, rt=rt, chunk=chunk, valid=valid, unroll=unroll, npp=npp)
    y, s_t = pl.pallas_call(
        kern,
        grid=(nblk, n_seq // bb, seq_rows // rt),
        in_specs=[
            pl.BlockSpec((bb, rt, nrb), lambda c, g, n: (g, n, c)),
            pl.BlockSpec((bb, rt, nrb), lambda c, g, n: (g, n, nblk + c)),
            pl.BlockSpec((bb, rt, nrb), lambda c, g, n: (g, n, 2 * nblk + c)),
            pl.BlockSpec((bb, rt, RANK_PAD), lambda c, g, n: (g, n, ZT_DW // RANK_PAD)),
            pl.BlockSpec((bb, rt, RANK_PAD), lambda c, g, n: (g, n, ZT_DA // RANK_PAD)),
            pl.BlockSpec((bb, rt, RWKV_GATE_RANK), lambda c, g, n: (g, n, ZT_DG // RWKV_GATE_RANK)),
            pl.BlockSpec((npp, 1, RW_CAT), lambda c, g, n: (c, 0, 0)),
            pl.BlockSpec((npp, bb, RW_CAT), lambda c, g, n: (c, g, 0)),
            pl.BlockSpec((bb, 2 * npp, RWKV_HEAD, RWKV_HEAD), sblk),
            vec,
            pl.BlockSpec((RANK_PAD, nrb), lambda c, g, n: (0, c)),
            vec,
            pl.BlockSpec((RANK_PAD, nrb), lambda c, g, n: (0, c)),
            pl.BlockSpec((RWKV_GATE_RANK, nrb), lambda c, g, n: (0, c)),
            vec, vec, vec, vec, vec,
        ],
        out_specs=[pl.BlockSpec((bb, rt, nrb), lambda c, g, n: (g, n, c)),
                   pl.BlockSpec((bb, 2 * npp, RWKV_HEAD, RWKV_HEAD), sblk)],
        out_shape=[jax.ShapeDtypeStruct((n_seq, seq_rows, MIX_A), BF16),
                   jax.ShapeDtypeStruct((n_seq, RWKV_HEADS, RWKV_HEAD, RWKV_HEAD), F32)],
        scratch_shapes=[pltpu.VMEM((bb, npp, LANES, LANES), F32), pltpu.VMEM((bb, npp, SUBLANES, RW_CAT), F32)],
        compiler_params=_cparams(("parallel", "parallel", "arbitrary")),
        name="rwkv7",
    )(z, z, z, zt, zt, zt, mu_cat, sh_cat, wkv0,
      row2(w0), wup_p, row2(a0), aup_p, g_up, row2(k_k), row2(k_a), row2(r_k), row2(lnx_w), row2(lnx_b))
    return y, s_t


def _rwkv_cat(a):
    npair = RWKV_HEADS // 2
    rows = a.shape[0]
    rkv = a[:, :3 * MIX_A].reshape(rows, 3, npair, LANES).transpose(2, 0, 1, 3).reshape(npair, rows, 3 * LANES)
    tail = jnp.broadcast_to(a[None, :, 3 * MIX_A:], (npair, rows, ZE_U - 3 * MIX_A))
    return jnp.concatenate([rkv, tail], axis=2)


def _rwkv_step_body(zr_ref, zk_ref, zv_ref, zdw_ref, zda_ref, zdg_ref, mu_ref, sh_ref, s0_ref,
                    w0_ref, wup_ref, a0_ref, aup_ref, gup_ref, kk_ref, ka_ref, rk_ref, lw_ref, lb_ref,
                    y_ref, sT_ref, op_scr, yt_scr, *, steps, nb):
    H = RWKV_HEAD
    rows = steps * nb
    ri = _iota((LANES, LANES), 0)
    ci = _iota((LANES, LANES), 1)
    bones = jnp.where((ri >= H) == (ci >= H), 1.0, 0.0).astype(BF16)
    block_sum = lambda x: _dot_const_rhs(x, bones)

    zc = jnp.concatenate([ref[...].reshape(rows, ref.shape[2]) for ref in (zr_ref, zk_ref, zv_ref, zdw_ref, zda_ref, zdg_ref)],
                         axis=1)
    zprev = jnp.concatenate([sh_ref[...], zc[0:rows - nb]], axis=0) if steps > 1 else sh_ref[...]
    zs = zc + mu_ref[...] * (zprev - zc)
    r = zs[:, 0:LANES]
    k = zs[:, LANES:2 * LANES]
    v = zs[:, 2 * LANES:3 * LANES]
    dw = zs[:, 3 * LANES:4 * LANES]
    da = zs[:, 4 * LANES:5 * LANES]
    dg = zs[:, 5 * LANES:]
    w_log = -_softplus(-(w0_ref[...] + _dot_hp(jnp.tanh(dw), wup_ref[...]))) - 0.5
    decay = jnp.exp(-jnp.exp(w_log))
    a = _sigmoid(a0_ref[...] + _dot_hp(da, aup_ref[...]))
    g = _dot_hp(_sigmoid(dg), gup_ref[...])
    kk = k * kk_ref[...]
    k = k * (1.0 + (a - 1.0) * ka_ref[...])
    sums = block_sum(jnp.concatenate([kk * kk, r * k * rk_ref[...]], axis=0))
    kk = kk / jnp.maximum(jnp.sqrt(sums[0:rows]), 1e-12)
    bonus = sums[rows:] * v

    for qi, x in enumerate((decay, -kk, kk * a, k, r, v)):
        for t in range(steps):
            op_scr[qi, t] = x[t * nb:(t + 1) * nb, :].T

    for t in range(steps):
        src = s0_ref if t == 0 else sT_ref
        for hh in range(2):
            lo = hh * H
            w_t = op_scr[0, t, lo:lo + H, :]
            ka_t = op_scr[1, t, lo:lo + H, :]
            kb_t = op_scr[2, t, lo:lo + H, :]
            k_t = op_scr[3, t, lo:lo + H, :]
            r_t = op_scr[4, t, lo:lo + H, :]

            def value_row(i, carry, src=src, hh=hh, lo=lo, t=t, w_t=w_t, ka_t=ka_t, kb_t=kb_t, k_t=k_t, r_t=r_t):
                s_i = src[hh, i]
                sa = jnp.sum(s_i * ka_t, axis=0, keepdims=True)
                v_i = op_scr[5, t, pl.ds(lo + i, 1), :]
                s_n = s_i * w_t + sa * kb_t + v_i * k_t
                sT_ref[hh, i] = s_n
                yt_scr[t, pl.ds(lo + i, 1), :] = jnp.sum(s_n * r_t, axis=0, keepdims=True)
                return carry

            lax.fori_loop(0, H, value_row, 0, unroll=4)

    y = jnp.concatenate([yt_scr[t].T for t in range(steps)], axis=0)
    mean = block_sum(y) * (1.0 / H)
    yc = y - mean
    var = block_sum(yc * yc) * (1.0 / H)
    yn = yc * lax.rsqrt(var + RWKV_LNX_EPS) * lw_ref[...] + lb_ref[...]
    y_ref[...] = ((yn + bonus) * g).reshape(steps, nb, LANES).astype(y_ref.dtype)


def _rwkv_step(z_tb, zt_tb, shift_p, wkv0_t, mu_p, w0, wup_p, a0, aup_p, g_up, k_k, k_a, r_k, lnx_w, lnx_b):
    steps, nb, _ = z_tb.shape
    assert nb == LANES
    npair = RWKV_HEADS // 2
    row2 = lambda a: a.reshape(1, MIX_A)
    vec = pl.BlockSpec((1, LANES), lambda c: (0, c))
    zspec = lambda width, idx: pl.BlockSpec((steps, nb, width), lambda c: (0, 0, idx(c)))
    sspec = pl.BlockSpec((2, RWKV_HEAD, RWKV_HEAD, nb), lambda c: (c, 0, 0, 0))
    kern = functools.partial(_rwkv_step_body, steps=steps, nb=nb)
    return pl.pallas_call(
        kern,
        grid=(npair,),
        in_specs=[
            zspec(LANES, lambda c: c), zspec(LANES, lambda c: npair + c), zspec(LANES, lambda c: 2 * npair + c),
            zspec(RANK_PAD, lambda c: ZT_DW // RANK_PAD), zspec(RANK_PAD, lambda c: ZT_DA // RANK_PAD),
            zspec(RWKV_GATE_RANK, lambda c: ZT_DG // RWKV_GATE_RANK),
            pl.BlockSpec((None, 1, RW_CAT), lambda c: (c, 0, 0)),
            pl.BlockSpec((None, nb, RW_CAT), lambda c: (c, 0, 0)),
            sspec,
            vec,
            pl.BlockSpec((RANK_PAD, LANES), lambda c: (0, c)),
            vec,
            pl.BlockSpec((RANK_PAD, LANES), lambda c: (0, c)),
            pl.BlockSpec((RWKV_GATE_RANK, LANES), lambda c: (0, c)),
            vec, vec, vec, vec, vec,
        ],
        out_specs=[pl.BlockSpec((steps, nb, LANES), lambda c: (0, 0, c)), sspec],
        out_shape=[jax.ShapeDtypeStruct((steps, nb, MIX_A), BF16),
                   jax.ShapeDtypeStruct((RWKV_HEADS, RWKV_HEAD, RWKV_HEAD, nb), F32)],
        scratch_shapes=[pltpu.VMEM((6, steps, LANES, nb), F32), pltpu.VMEM((steps, LANES, nb), F32)],
        compiler_params=_cparams(("parallel",)),
        name="rwkv7_step",
    )(z_tb, z_tb, z_tb, zt_tb, zt_tb, zt_tb, _rwkv_cat(mu_p.reshape(1, ZE_U)), _rwkv_cat(shift_p), wkv0_t,
      row2(w0), wup_p, row2(a0), aup_p, g_up, row2(k_k), row2(k_a), row2(r_k), row2(lnx_w), row2(lnx_b))


def _s5_prep_body(are_ref, aim_ref, ls_ref, abre_ref, abim_ref, zr_ref, zi_ref):
    lam_re = are_ref[...]
    lam_im = aim_ref[...]
    dt = jnp.exp(ls_ref[...])
    mag = jnp.exp(lam_re * dt)
    ab_re = mag * jnp.cos(lam_im * dt)
    ab_im = mag * jnp.sin(lam_im * dt)
    inv = 1.0 / (lam_re * lam_re + lam_im * lam_im)
    abre_ref[...] = ab_re
    abim_ref[...] = ab_im
    zr_ref[...] = ((ab_re - 1.0) * lam_re + ab_im * lam_im) * inv
    zi_ref[...] = (ab_im * lam_re - (ab_re - 1.0) * lam_im) * inv


def _s5_prep(a_re, a_im, log_step):
    g, n = a_re.shape
    sd = jax.ShapeDtypeStruct((g, n), F32)
    return pl.pallas_call(_s5_prep_body, out_shape=[sd, sd, sd, sd], name="s5_discretise")(
        a_re, a_im, jnp.broadcast_to(log_step.reshape(g, 1), (g, n)))


def _gelu_tanh(x):
    return 0.5 * x * (1.0 + jnp.tanh(math.sqrt(2.0 / math.pi) * (x + 0.044715 * (x * x * x))))


def _s5_body(u_ref, wbr_ref, wbi_ref, wcr_ref, wci_ref, abr_ref, abi_ref, d_ref, h0r_ref, h0i_ref,
             y_ref, hTr_ref, hTi_ref, hr_scr, hi_scr, pr_scr, pi_scr, cr_scr, ci_scr, u_st, y_st, *, rt, seq_rows):
    n = pl.program_id(1)
    long_seq = seq_rows >= rt
    grp = SUBLANES if long_seq else rt // seq_rows
    steps = rt // grp
    ar1 = abr_ref[...]
    ai1 = abi_ref[...]
    ar = jnp.broadcast_to(ar1, (grp, ar1.shape[1]))
    ai = jnp.broadcast_to(ai1, (grp, ai1.shape[1]))
    nhalf = u_ref.shape[1] // LANES
    for hf in range(nhalf):
        u_st[hf] = u_ref[:, hf * LANES:(hf + 1) * LANES]
    u = jnp.concatenate(
        [jnp.concatenate([u_st[hf, pl.ds(j, grp, stride=steps), :] for hf in range(nhalf)], axis=1) for j in range(steps)],
        axis=0)
    ub = u.astype(BF16)
    hr_scr[...] = _dot(ub, wbr_ref[...].astype(BF16))
    hi_scr[...] = _dot(ub, wbi_ref[...].astype(BF16))

    def rows(j):
        return pl.ds(j * grp, grp) if isinstance(j, int) else pl.ds(pl.multiple_of(j * grp, grp), grp)

    def scan_step(j, c):
        hr, hi = c
        nr = ar * hr - ai * hi + hr_scr[rows(j), :]
        ni = ar * hi + ai * hr + hi_scr[rows(j), :]
        hr_scr[rows(j), :] = nr
        hi_scr[rows(j), :] = ni
        return nr, ni

    if long_seq:
        @pl.when(n == 0)
        def _():
            def pw_step(j, c):
                qr, qi = c
                pr_scr[pl.ds(j, 1), :] = qr
                pi_scr[pl.ds(j, 1), :] = qi
                return qr * ar1 - qi * ai1, qr * ai1 + qi * ar1
            lax.fori_loop(0, steps, pw_step, (ar1, ai1))
            cr_scr[...] = jnp.zeros_like(cr_scr)
            ci_scr[...] = jnp.zeros_like(ci_scr)

        b = (n * rt) // seq_rows
        is_start = ((n * rt) % seq_rows) == 0
        h_in_r = jnp.where(is_start, h0r_ref[pl.ds(b, 1), :], cr_scr[0:1, :])
        h_in_i = jnp.where(is_start, h0i_ref[pl.ds(b, 1), :], ci_scr[0:1, :])
        zero = jnp.zeros((grp, hr_scr.shape[1]), F32)
        er, ei = lax.fori_loop(0, steps, scan_step, (zero, zero), unroll=4)
        pS_r = pr_scr[steps - 1:steps, :]
        pS_i = pi_scr[steps - 1:steps, :]
        cr_rows, ci_rows = [h_in_r], [h_in_i]
        for s in range(grp):
            pr_, pi_ = cr_rows[-1], ci_rows[-1]
            cr_rows.append(er[s:s + 1, :] + pS_r * pr_ - pS_i * pi_)
            ci_rows.append(ei[s:s + 1, :] + pS_r * pi_ + pS_i * pr_)
        c_r = jnp.concatenate(cr_rows[:grp], axis=0)
        c_i = jnp.concatenate(ci_rows[:grp], axis=0)
        cr_scr[0:1, :] = cr_rows[grp]
        ci_scr[0:1, :] = ci_rows[grp]
        hTr_ref[pl.ds(b, 1), :] = cr_rows[grp]
        hTi_ref[pl.ds(b, 1), :] = ci_rows[grp]

        def fix_step(j, carry):
            qr = pr_scr[pl.ds(j, 1), :]
            qi = pi_scr[pl.ds(j, 1), :]
            hr_scr[rows(j), :] = hr_scr[rows(j), :] + (qr * c_r - qi * c_i)
            hi_scr[rows(j), :] = hi_scr[rows(j), :] + (qr * c_i + qi * c_r)
            return carry

        lax.fori_loop(0, steps, fix_step, 0, unroll=4)
    else:
        hr, hi = h0r_ref[...], h0i_ref[...]
        for t in range(steps):
            hr, hi = scan_step(t, (hr, hi))
        hTr_ref[...] = hr
        hTi_ref[...] = hi

    y = _dot(hr_scr[...].astype(BF16), wcr_ref[...].astype(BF16)) + _dot(hi_scr[...].astype(BF16), wci_ref[...].astype(BF16))
    y = _gelu_tanh(y + d_ref[...] * u)
    for hf in range(nhalf):
        y_st[hf] = y[:, hf * LANES:(hf + 1) * LANES]
    for hf in range(nhalf):
        for s in range(grp):
            y_ref[s * steps:(s + 1) * steps, hf * LANES:(hf + 1) * LANES] = y_st[hf, pl.ds(s, steps, stride=grp), :]


def _s5_scan(z, col0, wbr, wbi, wcr, wci, ab_re, ab_im, d_skip, h0_re, h0_im, *, n_seq, seq_rows):
    m = z.shape[0]
    st = S5_SLAB_STATE
    if seq_rows >= 8 * SUBLANES:
        rt = _pick(seq_rows, (512, 256, 128, 64))
        hspec = pl.BlockSpec((n_seq, st), lambda s, n: (0, s))
        pw_rows = rt // SUBLANES
    else:
        rt = seq_rows * _pick(n_seq, (128, 64, 32, 16, 8))
        hspec = pl.BlockSpec((rt // seq_rows, st), lambda s, n: (n, s))
        pw_rows = SUBLANES
    win = S5_SLAB_IN
    cb0 = col0 // win
    kern = functools.partial(_s5_body, rt=rt, seq_rows=seq_rows)
    wspec_b = pl.BlockSpec((None, win, st), lambda s, n: (s, 0, 0))
    wspec_c = pl.BlockSpec((None, st, win), lambda s, n: (s, 0, 0))
    vspec = pl.BlockSpec((1, st), lambda s, n: (0, s))
    return pl.pallas_call(
        kern,
        grid=(S5_SLABS, m // rt),
        in_specs=[pl.BlockSpec((rt, win), lambda s, n: (n, cb0 + s)), wspec_b, wspec_b, wspec_c, wspec_c,
                  vspec, vspec, pl.BlockSpec((1, win), lambda s, n: (0, s)), hspec, hspec],
        out_specs=[pl.BlockSpec((rt, win), lambda s, n: (n, s)), hspec, hspec],
        out_shape=[jax.ShapeDtypeStruct((m, MIX_B), F32),
                   jax.ShapeDtypeStruct((n_seq, S5_GROUPS * S5_STATE), F32),
                   jax.ShapeDtypeStruct((n_seq, S5_GROUPS * S5_STATE), F32)],
        scratch_shapes=[pltpu.VMEM((rt, st), F32), pltpu.VMEM((rt, st), F32),
                        pltpu.VMEM((pw_rows, st), F32), pltpu.VMEM((pw_rows, st), F32),
                        pltpu.VMEM((SUBLANES, st), F32), pltpu.VMEM((SUBLANES, st), F32),
                        pltpu.VMEM((win // LANES, rt, LANES), F32), pltpu.VMEM((win // LANES, rt, LANES), F32)],
        compiler_params=_cparams(("parallel", "arbitrary")),
        name="s5_scan",
    )(z, wbr, wbi, wcr, wci, ab_re.reshape(1, -1), ab_im.reshape(1, -1), d_skip.reshape(1, -1), h0_re, h0_im)


def _glu_body(a_ref, w_ref, b_ref, y_ref, o_ref):
    t = _dot(a_ref[...].astype(BF16), w_ref[...].astype(BF16)) + b_ref[...]
    o_ref[...] = (y_ref[...] * _sigmoid(t)).astype(o_ref.dtype)


def _glu(y, w, b):
    m, k = y.shape
    n = w.shape[1]
    tm = _pick(m, (1024, 512, 256, 128, 64, 32, 16, 8))
    tn = _pick(n, (512, 256, 128))
    return pl.pallas_call(
        _glu_body,
        grid=(m // tm, n // tn),
        in_specs=[pl.BlockSpec((tm, k), lambda i, j: (i, 0)), pl.BlockSpec((k, tn), lambda i, j: (0, j)),
                  pl.BlockSpec((1, tn), lambda i, j: (0, j)), pl.BlockSpec((tm, tn), lambda i, j: (i, j))],
        out_specs=pl.BlockSpec((tm, tn), lambda i, j: (i, j)),
        out_shape=jax.ShapeDtypeStruct((m, n), BF16),
        compiler_params=_cparams(("parallel", "arbitrary")),
        name="s5_glu",
    )(y, w, b.reshape(1, n), y)


def _mlstm_body(q_ref, k_ref, v_ref, og_ref, gt_ref, gb_ref, nw_ref, c0_ref, n0_ref, m0_ref,
                y_ref, cT_ref, nT_ref, mT_ref, c_scr, n_scr, m_scr, *, bb, rt, chunk, valid, unroll):
    h = pl.program_id(0)
    n = pl.program_id(2)
    n_last = pl.num_programs(2) - 1
    L = chunk
    nchunk = rt // L
    rl = _iota((L, L), 0)
    cl = _iota((L, L), 1)
    causal = rl >= cl
    tril = jnp.where(causal, 1.0, 0.0).astype(BF16)
    row = _iota((L, 1), 0)
    lane = _iota((L, LANES), 1)
    gbias = gb_ref[...]
    nw = nw_ref[...]
    scale = MLSTM_DQK ** -0.5

    def load(seq, r0, is_start):
        c_old = jnp.where(is_start, c0_ref[seq, 0], c_scr[seq])
        n_old = jnp.where(is_start, n0_ref[seq, 0], n_scr[seq, 0:1, :])
        m_prev = jnp.where(is_start, m0_ref[seq, 0], m_scr[seq, 0:1, 0:1])
        rows = pl.ds(r0, L)
        return (q_ref[seq, rows, :], k_ref[seq, rows, :], v_ref[seq, rows, :], og_ref[seq, rows, :],
                gt_ref[seq, rows, :], c_old, n_old, m_prev)

    def compute_all(loaded):
        nu = len(loaded)
        igs, lfs = [], []
        for q, k, v, og, gt, c_old, n_old, m_prev in loaded:
            gt = gt + gbias
            ig_raw = jnp.sum(jnp.where(lane == h, gt, 0.0), axis=-1, keepdims=True)
            fg_raw = jnp.sum(jnp.where(lane == MLSTM_HEADS + h, gt, 0.0), axis=-1, keepdims=True)
            ig = GATE_CAP * jnp.tanh(ig_raw / GATE_CAP)
            lf = -_softplus(-(GATE_CAP * jnp.tanh(fg_raw / GATE_CAP)))
            if valid < L:
                ok = row < valid
                ig = jnp.where(ok, ig, NEG_BIG)
                lf = jnp.where(ok, lf, 0.0)
            igs.append(ig)
            lfs.append(lf)
        lf_mat = jnp.zeros((L, LANES), F32)
        for u in range(nu):
            lf_mat = jnp.where(lane == u, lfs[u], lf_mat)
        b_mat = _dot_const(tril, lf_mat)
        pack = b_mat
        for u in range(nu):
            pack = jnp.where(lane == nu + u, igs[u], pack)
        pack_t = pack.T
        gates, qbs, kbs, ks = [], [], [], []
        for u, (q, k, v, og, gt, c_old, n_old, m_prev) in enumerate(loaded):
            bcol = b_mat[:, u:u + 1]
            b_row = pack_t[u:u + 1, :]
            ig_row = pack_t[nu + u:nu + u + 1, :]
            log_d = jnp.where(causal, bcol - b_row + ig_row, NEG_BIG)
            log_p = bcol + m_prev
            m_tok = jnp.maximum(log_p, jnp.max(log_d, axis=-1, keepdims=True))
            gates.append((jnp.exp(log_d - m_tok), jnp.exp(log_p - m_tok), m_tok, bcol))
            k = k * scale
            ks.append(k)
            qbs.append(q.astype(BF16))
            kbs.append(k.astype(BF16))
        ss = [_dot(qb, kb, NT) * g_[0] for qb, kb, g_ in zip(qbs, kbs, gates)]
        svs = [_dot(s.astype(BF16), x[2].astype(BF16)) for s, x in zip(ss, loaded)]
        qcs = [_dot(qb, x[5].astype(BF16), NT) for qb, x in zip(qbs, loaded)]
        upd = []
        for u, (q, k, v, og, gt, c_old, n_old, m_prev) in enumerate(loaded):
            d, p, m_tok, bcol = gates[u]
            m_new = m_tok[L - 1:L, :]
            b_last = bcol[L - 1:L, :]
            w_col = jnp.exp(b_last - bcol + igs[u] - m_new)
            cs = jnp.exp(b_last + m_prev - m_new)
            upd.append((w_col, cs, m_new))
        vks = [_dot((x[2] * w_[0]).astype(BF16), kb, TN) for x, w_, kb in zip(loaded, upd, kbs)]
        out = []
        for u, (q, k, v, og, gt, c_old, n_old, m_prev) in enumerate(loaded):
            d, p, m_tok, bcol = gates[u]
            w_col, cs, m_new = upd[u]
            num = svs[u] + p * qcs[u]
            den = jnp.sum(ss[u], axis=-1, keepdims=True) + p * jnp.sum(q * n_old, axis=-1, keepdims=True)
            hh = num / jnp.maximum(jnp.abs(den), jnp.exp(-m_tok))
            c_new = cs * c_old + vks[u]
            n_new = cs * n_old + jnp.sum(w_col * ks[u], axis=0, keepdims=True)
            hn = hh * lax.rsqrt(jnp.mean(hh * hh, axis=-1, keepdims=True) + RMS_EPS)
            out.append(((hn * nw * _sigmoid(og)).astype(y_ref.dtype), c_new, n_new, m_new))
        return out

    def store(seq, r0, is_end, y, c_new, n_new, m_new):
        y_ref[seq, pl.ds(r0, L), :] = y
        c_scr[seq] = c_new
        n_scr[seq, 0:1, :] = n_new
        m_scr[seq, 0:1, :] = jnp.broadcast_to(m_new, (1, LANES))

        @pl.when(is_end)
        def _():
            cT_ref[seq, 0] = c_new
            nT_ref[seq, 0] = n_new
            mT_ref[seq, 0] = m_new

    def step(it, carry):
        sg = it // nchunk
        ci_ = it - sg * nchunk
        r0 = pl.multiple_of(ci_ * L, L)
        is_start = jnp.logical_and(n == 0, ci_ == 0)
        is_end = jnp.logical_and(n == n_last, ci_ == nchunk - 1)
        seqs = [sg * unroll + u for u in range(unroll)]
        loaded = [load(s, r0, is_start) for s in seqs]
        done = compute_all(loaded)
        for s, d in zip(seqs, done):
            store(s, r0, is_end, *d)
        return carry

    @pl.when(n == 0)
    def _():
        c_scr[...] = jnp.zeros_like(c_scr)
        n_scr[...] = jnp.zeros_like(n_scr)
        m_scr[...] = jnp.zeros_like(m_scr)

    lax.fori_loop(0, (bb // unroll) * nchunk, step, 0)


def _mlstm(z, zt, c0, n0, m0, b_i, b_f, norm_w, *, n_seq, seq_rows, chunk, valid):
    nh, dqk, dv = MLSTM_HEADS, MLSTM_DQK, MLSTM_DV
    unroll = _pick(n_seq, (4, 2, 1))
    if seq_rows > chunk:
        rt = _pick(seq_rows, (256, 128, 64))
        bb = unroll
    else:
        rt = seq_rows
        bb = _pick(n_seq, (8, 4, 2, 1))
    assert rt % chunk == 0 and bb % unroll == 0
    sblk = lambda h, g, n: (g, h, 0, 0)
    gbias = jnp.pad(jnp.concatenate([b_i, b_f]).reshape(1, 2 * nh), ((0, 0), (0, LANES - 2 * nh)))
    n0r = n0.reshape(n_seq, nh, 1, dqk)
    m0r = m0.reshape(n_seq, nh, 1, 1)
    kern = functools.partial(_mlstm_body, bb=bb, rt=rt, chunk=chunk, valid=valid, unroll=unroll)
    y, c_t, n_t, m_t = pl.pallas_call(
        kern,
        grid=(nh, n_seq // bb, seq_rows // rt),
        in_specs=[
            pl.BlockSpec((bb, rt, dqk), lambda h, g, n: (g, n, h)),
            pl.BlockSpec((bb, rt, dqk), lambda h, g, n: (g, n, nh + h)),
            pl.BlockSpec((bb, rt, dv), lambda h, g, n: (g, n, (2 * nh * dqk) // dv + h)),
            pl.BlockSpec((bb, rt, dv), lambda h, g, n: (g, n, h)),
            pl.BlockSpec((bb, rt, LANES), lambda h, g, n: (g, n, (ZO_G - ZO_OG) // LANES)),
            pl.BlockSpec((1, LANES), lambda h, g, n: (0, 0)),
            pl.BlockSpec((1, dv), lambda h, g, n: (0, h)),
            pl.BlockSpec((bb, 1, dv, dqk), sblk),
            pl.BlockSpec((bb, 1, 1, dqk), sblk),
            pl.BlockSpec((bb, 1, 1, 1), sblk),
        ],
        out_specs=[pl.BlockSpec((bb, rt, dv), lambda h, g, n: (g, n, h)),
                   pl.BlockSpec((bb, 1, dv, dqk), sblk),
                   pl.BlockSpec((bb, 1, 1, dqk), sblk),
                   pl.BlockSpec((bb, 1, 1, 1), sblk)],
        out_shape=[jax.ShapeDtypeStruct((n_seq, seq_rows, nh * dv), BF16),
                   jax.ShapeDtypeStruct((n_seq, nh, dv, dqk), F32),
                   jax.ShapeDtypeStruct((n_seq, nh, 1, dqk), F32),
                   jax.ShapeDtypeStruct((n_seq, nh, 1, 1), F32)],
        scratch_shapes=[pltpu.VMEM((bb, dv, dqk), F32), pltpu.VMEM((bb, SUBLANES, dqk), F32),
                        pltpu.VMEM((bb, SUBLANES, LANES), F32)],
        compiler_params=_cparams(("parallel", "parallel", "arbitrary")),
        name="mlstm",
    )(z, z, z, zt, zt, gbias, norm_w.reshape(1, nh * dv), c0, n0r, m0r)
    return y, c_t, n_t.reshape(n_seq, nh, dqk), m_t.reshape(n_seq, nh)


def _pad_cols(a, n):
    return jnp.pad(a, ((0, 0), (0, n - a.shape[1])))


def _pad_rows(a, n):
    return jnp.pad(a, ((0, n - a.shape[0]), (0, 0)))


def _rwkv_cols(a):
    c1 = 3 * MIX_A
    c2 = c1 + RWKV_DECAY_RANK
    c3 = c2 + RWKV_ICL_RANK
    return jnp.concatenate([a[:, :c1], _pad_cols(a[:, c1:c2], RANK_PAD), _pad_cols(a[:, c2:c3], RANK_PAD), a[:, c3:]], axis=1)


def _rwkv_cols_inv(a):
    return jnp.concatenate([a[:, :ZE_DW + RWKV_DECAY_RANK], a[:, ZE_DA:ZE_DA + RWKV_ICL_RANK], a[:, ZE_DG:ZE_U]], axis=1)


def _pad_seq(a, n_seq, t, tp):
    a = a.reshape(n_seq, t, a.shape[1])
    return a if tp == t else jnp.pad(a, ((0, 0), (0, tp - t), (0, 0)))


def _unpad_seq(a, n_seq, t, tp):
    a = a if tp == t else a[:, :t]
    return a.reshape(n_seq * t, a.shape[2])


def _s5_block_weights(zr, zi, b_re, b_im, c_re, c_im):
    g, n, c = S5_GROUPS, S5_STATE, S5_GROUP
    sg = S5_SLAB_GROUPS
    bb_re = zr[..., None] * b_re - zi[..., None] * b_im
    bb_im = zr[..., None] * b_im + zi[..., None] * b_re
    eye = jnp.eye(sg, dtype=F32)

    def in_blocks(bb):
        t = bb.reshape(S5_SLABS, sg, n, c)
        return jnp.einsum('sgnc,gh->sgchn', t, eye).reshape(S5_SLABS, sg * c, sg * n)

    def out_blocks(cc):
        t = cc.reshape(S5_SLABS, sg, c, n)
        return jnp.einsum('sgcn,gh->sgnhc', t, eye).reshape(S5_SLABS, sg * n, sg * c)

    return in_blocks(bb_re), in_blocks(bb_im), out_blocks(c_re), out_blocks(-c_im)


def _trunk(x, n_seq, t, st, w):
    tp = t if t >= MLSTM_CHUNK else SUBLANES * ((t + SUBLANES - 1) // SUBLANES)
    chunk = MLSTM_CHUNK if t >= MLSTM_CHUNK else tp
    valid = chunk if tp == t else t
    assert t % chunk == 0 or tp == chunk

    xn = _rmsnorm(x, w['ln_mix_e'], BF16)
    z, zt = _mm_split(xn, w['w_in_e'], ZE_DW, w['w_in_e_tail'])
    rw = (w['mu'], w['w0'], w['w_up'], w['a0'], w['a_up'], w['g_up'], w['k_k'], w['k_a'], w['r_k'], w['lnx_w'], w['lnx_b'])
    if t < chunk and n_seq == LANES:
        tb = lambda a: a.reshape(n_seq, t, a.shape[1]).transpose(1, 0, 2)
        ya, wkv_t = _rwkv_step(tb(z), tb(zt), _rwkv_cols(st['shift']), st['wkv'].transpose(1, 2, 3, 0), *rw)
        ya = ya.transpose(1, 0, 2).reshape(n_seq * t, MIX_A)
        wkv_t = wkv_t.transpose(3, 0, 1, 2)
    else:
        ya, wkv_t = _rwkv(_pad_seq(z, n_seq, t, tp), _pad_seq(zt, n_seq, t, tp), _rwkv_cols(st['shift']), st['wkv'], *rw,
                          n_seq=n_seq, seq_rows=tp, chunk=chunk, valid=valid)
        ya = _unpad_seq(ya, n_seq, t, tp)
    last = lambda a: a.reshape(n_seq, t, a.shape[1])[:, -1]
    shift_t = _rwkv_cols_inv(jnp.concatenate([last(z), last(zt)[:, :ZT_U]], axis=1))
    yb, re_t, im_t = _s5_scan(zt, ZT_U, w['s5_wbr'], w['s5_wbi'], w['s5_wcr'], w['s5_wci'], w['s5_ab_re'], w['s5_ab_im'],
                              w['s5_d'], st['s5_re'], st['s5_im'], n_seq=n_seq, seq_rows=t)
    yb = _glu(yb, w['s5_glu_w'], w['s5_glu_b'])
    x = _mm2_res(ya, yb, w['w_out_e'], x)
    hn = _rmsnorm(x, w['ln_ffn_e'], BF16)
    hid = _glu_up(hn, w['ffn_w1'], w['ffn_w3'])
    x = _mm_acc(hid, w['ffn_w2'], x)

    xn = _rmsnorm(x, w['ln_mix_o'], BF16)
    zo, zot = _mm_split(xn, w['w_in_o'], ZO_OG, w['w_in_o_tail'])
    yc, c_t, n_t, m_t = _mlstm(_pad_seq(zo, n_seq, t, tp), _pad_seq(zot, n_seq, t, tp), st['c'], st['n'], st['m'], w['b_i'], w['b_f'], w['norm_w'],
                               n_seq=n_seq, seq_rows=tp, chunk=chunk, valid=valid)
    yc = _unpad_seq(yc, n_seq, t, tp)
    x = _mm_res(yc, w['w_out_o'], x)
    hn, comb = _rmsnorm_router(x, w['ln_ffn_o'], w['router_w'], w['router_b'])

    g, ns = S5_GROUPS, S5_STATE
    states = (wkv_t[None], shift_t[None], re_t.reshape(1, n_seq, g, ns), im_t.reshape(1, n_seq, g, ns),
              c_t[None], n_t[None], m_t[None])
    return (x, hn, comb), states


def kernel(x_prompt, x_sample, state_rwkv_wkv, state_rwkv_shift, state_s5_re, state_s5_im, state_mlstm_c, state_mlstm_n, state_mlstm_m, ln_mix_e, w_in_e, rwkv_mu, rwkv_w0, rwkv_w_up, rwkv_a0, rwkv_a_up, rwkv_g_up, rwkv_k_k, rwkv_k_a, rwkv_r_k, rwkv_lnx_w, rwkv_lnx_b, s5_a_re, s5_a_im, s5_log_step, s5_b_re, s5_b_im, s5_c_re, s5_c_im, s5_d, s5_glu_w, s5_glu_b, w_out_e, ln_ffn_e, ffn_w1, ffn_w3, ffn_w2, ln_mix_o, w_in_o, mlstm_b_i, mlstm_b_f, mlstm_norm_w, w_out_o, ln_ffn_o, router_w, router_b, exp_w1, exp_w3, exp_w2, final_norm):
    assert ln_mix_e.shape[0] == 1 and ln_mix_o.shape[0] == 1
    d = D_MODEL
    ab_re, ab_im, zr, zi = _s5_prep(s5_a_re[0], s5_a_im[0], s5_log_step[0])
    wbr, wbi, wcr, wci = _s5_block_weights(zr, zi, s5_b_re[0], s5_b_im[0], s5_c_re[0], s5_c_im[0])
    wo = jnp.swapaxes(w_in_o[0], 0, 1)
    we = jnp.swapaxes(w_in_e[0], 0, 1)
    c_dw = 3 * MIX_A
    c_da = c_dw + RWKV_DECAY_RANK
    c_dg = c_da + RWKV_ICL_RANK
    nh = MLSTM_HEADS
    w = {
        'ln_mix_e': ln_mix_e[0],
        'w_in_e': we,
        'w_in_e_tail': jnp.concatenate([_pad_rows(we[c_dw:c_da], RANK_PAD), _pad_rows(we[c_da:c_dg], RANK_PAD),
                                        we[c_dg:]], axis=0),
        'mu': _rwkv_cols(rwkv_mu[0].reshape(1, RWKV_IN)),
        'w0': rwkv_w0[0], 'a0': rwkv_a0[0],
        'w_up': jnp.pad(rwkv_w_up[0], ((0, RANK_PAD - RWKV_DECAY_RANK), (0, 0))),
        'a_up': jnp.pad(rwkv_a_up[0], ((0, RANK_PAD - RWKV_ICL_RANK), (0, 0))),
        'g_up': rwkv_g_up[0], 'k_k': rwkv_k_k[0], 'k_a': rwkv_k_a[0], 'r_k': rwkv_r_k[0].reshape(MIX_A),
        'lnx_w': rwkv_lnx_w[0], 'lnx_b': rwkv_lnx_b[0],
        's5_wbr': wbr, 's5_wbi': wbi, 's5_wcr': wcr, 's5_wci': wci,
        's5_ab_re': ab_re, 's5_ab_im': ab_im, 's5_d': s5_d[0], 's5_glu_w': s5_glu_w[0], 's5_glu_b': s5_glu_b[0],
        'w_out_e': w_out_e[0], 'ln_ffn_e': ln_ffn_e[0],
        'ffn_w1': ffn_w1[0], 'ffn_w3': ffn_w3[0], 'ffn_w2': ffn_w2[0],
        'ln_mix_o': ln_mix_o[0],
        'w_in_o': wo,
        'w_in_o_tail': jnp.concatenate([wo[ZO_OG + 2 * nh:], _pad_rows(wo[ZO_OG:ZO_OG + 2 * nh], ZO_W - ZO_G)], axis=0),
        'b_i': mlstm_b_i[0], 'b_f': mlstm_b_f[0], 'norm_w': mlstm_norm_w[0],
        'w_out_o': w_out_o[0], 'ln_ffn_o': ln_ffn_o[0],
        'router_w': router_w[0], 'router_b': router_b[0],
        'exp_w1': exp_w1[0], 'exp_w3': exp_w3[0], 'exp_w2': exp_w2[0],
        'final_norm': final_norm,
    }
    bp, tp_, _ = x_prompt.shape
    bs, ts_, _ = x_sample.shape
    g, ns = S5_GROUPS, S5_STATE
    zero = lambda *s: jnp.zeros(s, F32)
    st_p = {'wkv': zero(bp, RWKV_HEADS, RWKV_HEAD, RWKV_HEAD), 'shift': zero(bp, RWKV_IN),
            's5_re': zero(bp, g * ns), 's5_im': zero(bp, g * ns),
            'c': zero(bp, nh, MLSTM_DV, MLSTM_DQK), 'n': zero(bp, nh, MLSTM_DQK), 'm': zero(bp, nh)}
    st_s = {'wkv': state_rwkv_wkv[0], 'shift': state_rwkv_shift[0],
            's5_re': state_s5_re[0].reshape(bs, g * ns), 's5_im': state_s5_im[0].reshape(bs, g * ns),
            'c': state_mlstm_c[0], 'n': state_mlstm_n[0], 'm': state_mlstm_m[0]}
    (x_p, hn_p, comb_p), states_p = _trunk(x_prompt.reshape(bp * tp_, d), bp, tp_, st_p, w)
    (x_s, hn_s, comb_s), states_s = _trunk(x_sample.reshape(bs * ts_, d), bs, ts_, st_s, w)
    y_p, y_s = _moe_sparse([x_p, x_s], [hn_p, hn_s], [comb_p, comb_s], w['exp_w1'], w['exp_w3'], w['exp_w2'], w['final_norm'])
    return (y_p.reshape(bp, tp_, d), y_s.reshape(bs, ts_, d)) + tuple(states_p) + tuple(states_s)
```

```python
import functools
import math

import jax
import jax.numpy as jnp
from jax import lax
from jax.experimental import pallas as pl
from jax.experimental.pallas import tpu as pltpu

F32 = jnp.float32
BF16 = jnp.bfloat16

D_MODEL = 4096
MIX_A = D_MODEL // 2
MIX_B = D_MODEL - MIX_A
RWKV_HEAD = 64
RWKV_HEADS = MIX_A // RWKV_HEAD
RWKV_DECAY_RANK = 96
RWKV_ICL_RANK = 96
RWKV_GATE_RANK = 256
RWKV_IN = 3 * MIX_A + RWKV_DECAY_RANK + RWKV_ICL_RANK + RWKV_GATE_RANK
RWKV_LNX_EPS = 1e-5 * RWKV_HEAD
S5_GROUP = 16
S5_GROUPS = MIX_B // S5_GROUP
S5_STATE = 64
MLSTM_HEADS = 8
MLSTM_DQK = D_MODEL // 16
MLSTM_DV = D_MODEL // 8
MLSTM_CHUNK = 64
GATE_CAP = 15.0
FFN_DIM = 11008
N_EXPERTS = 8
EXPERT_FF = D_MODEL // 2
RMS_EPS = 1e-6

LANES = 128
SUBLANES = 8
VMEM_LIMIT = 56 * 1024 * 1024

RANK_PAD = LANES
ZE_DW = 3 * MIX_A
ZE_DA = ZE_DW + RANK_PAD
ZE_DG = ZE_DA + RANK_PAD
ZE_U = ZE_DG + RWKV_GATE_RANK
ZE_W = ZE_U + MIX_B
ZT_DW = 0
ZT_DA = ZT_DW + RANK_PAD
ZT_DG = ZT_DA + RANK_PAD
ZT_U = ZT_DG + RWKV_GATE_RANK
RW_CAT = 3 * LANES + 2 * RANK_PAD + RWKV_GATE_RANK
ZO_OG = 2 * MLSTM_HEADS * MLSTM_DQK + MLSTM_HEADS * MLSTM_DV
ZO_G = ZO_OG + MLSTM_HEADS * MLSTM_DV
ZO_W = ZO_G + 4 * LANES

S5_SLAB_GROUPS = 8
S5_SLABS = S5_GROUPS // S5_SLAB_GROUPS
S5_SLAB_IN = S5_SLAB_GROUPS * S5_GROUP
S5_SLAB_STATE = S5_SLAB_GROUPS * S5_STATE

NN = (((1,), (0,)), ((), ()))
NT = (((1,), (1,)), ((), ()))
TN = (((0,), (0,)), ((), ()))
NEG_BIG = -1e30


def _cparams(sem):
    return pltpu.CompilerParams(dimension_semantics=sem, vmem_limit_bytes=VMEM_LIMIT)


def _pick(n, cands):
    for c in cands:
        if n % c == 0:
            return c
    raise ValueError(f"no tile for {n}")


def _dot(a, b, dims=NN):
    return lax.dot_general(a, b, dims, preferred_element_type=F32)


def _hp_pair(a):
    hi = a.astype(BF16).astype(F32)
    return hi, a - hi


def _hp_lhs(a, axis=1):
    hi, lo = _hp_pair(a)
    return jnp.concatenate([hi, hi, lo], axis=axis).astype(BF16)


def _hp_rhs(b, axis=0):
    hi, lo = _hp_pair(b)
    return jnp.concatenate([hi, lo, hi], axis=axis).astype(BF16)


def _dot_hp(a, b, dims=NN):
    return _dot(_hp_lhs(a, dims[0][0][0]), _hp_rhs(b, dims[0][1][0]), dims)


def _split3(a):
    h0 = a.astype(BF16).astype(F32)
    r1 = a - h0
    h1 = r1.astype(BF16).astype(F32)
    return h0, h1, r1 - h1


def _dot_const(c_bf16, a):
    return _dot(jnp.concatenate([c_bf16] * 3, axis=1), jnp.concatenate(_split3(a), axis=0).astype(BF16))


def _sigmoid(x):
    return 1.0 / (1.0 + jnp.exp(-x))


def _softplus(x):
    return jnp.maximum(x, 0.0) + jnp.log(1.0 + jnp.exp(-jnp.abs(x)))


def _iota(shape, dim):
    return lax.broadcasted_iota(jnp.int32, shape, dim)


def _rmsnorm_body(x_ref, g_ref, o_ref):
    x = x_ref[...]
    ms = jnp.mean(x * x, axis=-1, keepdims=True)
    o_ref[...] = (x * lax.rsqrt(ms + RMS_EPS) * g_ref[...]).astype(o_ref.dtype)


def _rmsnorm(x, g, out_dtype):
    m, d = x.shape
    tm = _pick(m, (256, 128, 64, 32, 16, 8))
    return pl.pallas_call(
        _rmsnorm_body,
        grid=(m // tm,),
        in_specs=[pl.BlockSpec((tm, d), lambda i: (i, 0)), pl.BlockSpec((1, d), lambda i: (0, 0))],
        out_specs=pl.BlockSpec((tm, d), lambda i: (i, 0)),
        out_shape=jax.ShapeDtypeStruct((m, d), out_dtype),
        compiler_params=_cparams(("parallel",)),
        name="rmsnorm",
    )(x, g.reshape(1, d))


def _router_body(x_ref, g_ref, rw_ref, rb_ref, hn_ref, comb_ref):
    x = x_ref[...]
    ms = jnp.mean(x * x, axis=-1, keepdims=True)
    h = x * lax.rsqrt(ms + RMS_EPS) * g_ref[...]
    hn_ref[...] = h.astype(hn_ref.dtype)
    logits = _dot_hp(h, rw_ref[...]) + rb_ref[...]
    lane = _iota(logits.shape, 1)
    logits = jnp.where(lane < N_EXPERTS, logits, NEG_BIG)
    m1 = jnp.max(logits, axis=-1, keepdims=True)
    i1 = jnp.min(jnp.where(logits == m1, lane, LANES), axis=-1, keepdims=True)
    rest = jnp.where(lane == i1, NEG_BIG, logits)
    m2 = jnp.max(rest, axis=-1, keepdims=True)
    i2 = jnp.min(jnp.where(rest == m2, lane, LANES), axis=-1, keepdims=True)
    e = jnp.exp(m2 - m1)
    g1 = 1.0 / (1.0 + e)
    g2 = e / (1.0 + e)
    comb_ref[...] = jnp.where(lane == i1, g1, 0.0) + jnp.where(lane == i2, g2, 0.0)


def _rmsnorm_router(x, g, router_w, router_b):
    m, d = x.shape
    tm = _pick(m, (256, 128, 64, 32, 16, 8))
    rw = jnp.pad(router_w, ((0, 0), (0, LANES - N_EXPERTS)))
    rb = jnp.pad(router_b.reshape(1, N_EXPERTS), ((0, 0), (0, LANES - N_EXPERTS)))
    return pl.pallas_call(
        _router_body,
        grid=(m // tm,),
        in_specs=[pl.BlockSpec((tm, d), lambda i: (i, 0)), pl.BlockSpec((1, d), lambda i: (0, 0)),
                  pl.BlockSpec((d, LANES), lambda i: (0, 0)), pl.BlockSpec((1, LANES), lambda i: (0, 0))],
        out_specs=[pl.BlockSpec((tm, d), lambda i: (i, 0)), pl.BlockSpec((tm, LANES), lambda i: (i, 0))],
        out_shape=[jax.ShapeDtypeStruct((m, d), BF16), jax.ShapeDtypeStruct((m, LANES), F32)],
        compiler_params=_cparams(("parallel",)),
        name="rmsnorm_router",
    )(x, g.reshape(1, d), rw, rb)


def _mm_body(a_ref, w_ref, o_ref):
    o_ref[...] = _dot(a_ref[...], w_ref[...].astype(BF16)).astype(o_ref.dtype)


def _mm(a, w, n_out=None):
    m, k = a.shape
    n = w.shape[1] if n_out is None else n_out
    tm = _pick(m, (1024, 512, 256, 128, 64, 32, 16, 8))
    tn = _pick(n, (512, 256, 128))
    return pl.pallas_call(
        _mm_body,
        grid=(m // tm, n // tn),
        in_specs=[pl.BlockSpec((tm, k), lambda i, j: (i, 0)), pl.BlockSpec((k, tn), lambda i, j: (0, j))],
        out_specs=pl.BlockSpec((tm, tn), lambda i, j: (i, j)),
        out_shape=jax.ShapeDtypeStruct((m, n), F32),
        compiler_params=_cparams(("parallel", "arbitrary")),
        name="proj_in",
    )(a, w)


def _mm_nt_body(a_ref, wt_ref, o_ref):
    o_ref[...] = _dot(a_ref[...], wt_ref[...].astype(BF16), NT)


def _mm_nt(a, wt, n_rows):
    m, k = a.shape
    tm = _pick(m, (1024, 512, 256, 128, 64, 32, 16, 8))
    tn = 4 * LANES
    assert n_rows % tn == 0
    return pl.pallas_call(
        _mm_nt_body,
        grid=(m // tm, n_rows // tn),
        in_specs=[pl.BlockSpec((tm, k), lambda i, j: (i, 0)), pl.BlockSpec((tn, k), lambda i, j: (j, 0))],
        out_specs=pl.BlockSpec((tm, tn), lambda i, j: (i, j)),
        out_shape=jax.ShapeDtypeStruct((m, n_rows), F32),
        compiler_params=_cparams(("parallel", "arbitrary")),
        name="proj_in",
    )(a, wt)


def _mm_split(a, w_t, n_main, w_tail_t):
    return _mm_nt(a, w_t, n_main), _mm_nt(a, w_tail_t, w_tail_t.shape[0])


def _mm2_res_body(a0_ref, a1_ref, w0_ref, w1_ref, r_ref, o_ref):
    acc = _dot(a0_ref[...], w0_ref[...].astype(BF16)) + _dot(a1_ref[...], w1_ref[...].astype(BF16))
    o_ref[...] = r_ref[...] + acc


def _mm2_res(a0, a1, w, res):
    m, k0 = a0.shape
    k1 = a1.shape[1]
    assert k0 == k1
    n = w.shape[1]
    tm = _pick(m, (1024, 512, 256, 128, 64, 32, 16, 8))
    tn = _pick(n, (512, 256, 128))
    return pl.pallas_call(
        _mm2_res_body,
        grid=(m // tm, n // tn),
        in_specs=[pl.BlockSpec((tm, k0), lambda i, j: (i, 0)), pl.BlockSpec((tm, k1), lambda i, j: (i, 0)),
                  pl.BlockSpec((k0, tn), lambda i, j: (0, j)), pl.BlockSpec((k1, tn), lambda i, j: (1, j)),
                  pl.BlockSpec((tm, tn), lambda i, j: (i, j))],
        out_specs=pl.BlockSpec((tm, tn), lambda i, j: (i, j)),
        out_shape=jax.ShapeDtypeStruct((m, n), F32),
        compiler_params=_cparams(("parallel", "arbitrary")),
        name="proj_out",
    )(a0, a1, w, w, res)


def _mm_res_body(a_ref, w_ref, r_ref, o_ref):
    o_ref[...] = r_ref[...] + _dot(a_ref[...], w_ref[...].astype(BF16))


def _mm_res(a, w, res):
    m, k = a.shape
    n = w.shape[1]
    tm = _pick(m, (1024, 512, 256, 128, 64, 32, 16, 8))
    tn = _pick(n, (512, 256, 128))
    return pl.pallas_call(
        _mm_res_body,
        grid=(m // tm, n // tn),
        in_specs=[pl.BlockSpec((tm, k), lambda i, j: (i, 0)), pl.BlockSpec((k, tn), lambda i, j: (0, j)),
                  pl.BlockSpec((tm, tn), lambda i, j: (i, j))],
        out_specs=pl.BlockSpec((tm, tn), lambda i, j: (i, j)),
        out_shape=jax.ShapeDtypeStruct((m, n), F32),
        compiler_params=_cparams(("parallel", "arbitrary")),
        name="proj_out1",
    )(a, w, res)


def _glu_up_body(a_ref, w1_ref, w3_ref, o_ref):
    a = a_ref[...]
    h1 = _dot(a, w1_ref[...].astype(BF16))
    h3 = _dot(a, w3_ref[...].astype(BF16))
    o_ref[...] = (h1 * _sigmoid(h1) * h3).astype(o_ref.dtype)


def _glu_up(a, w1, w3):
    m, k = a.shape
    n = w1.shape[1]
    tm = _pick(m, (1024, 512, 256, 128, 64, 32, 16, 8))
    tn = _pick(n, (256, 128))
    return pl.pallas_call(
        _glu_up_body,
        grid=(m // tm, n // tn),
        in_specs=[pl.BlockSpec((tm, k), lambda i, j: (i, 0)), pl.BlockSpec((k, tn), lambda i, j: (0, j)),
                  pl.BlockSpec((k, tn), lambda i, j: (0, j))],
        out_specs=pl.BlockSpec((tm, tn), lambda i, j: (i, j)),
        out_shape=jax.ShapeDtypeStruct((m, n), BF16),
        compiler_params=_cparams(("parallel", "arbitrary")),
        name="ffn_up",
    )(a, w1, w3)


def _mm_acc_body(a_ref, w_ref, r_ref, o_ref, *, k_total, tk, rc):
    kk = pl.program_id(2)
    ragged = k_total % tk != 0
    valid = k_total - kk * tk
    w = w_ref[...]
    if ragged:
        w = jnp.where(_iota(w.shape, 0) < valid, w, 0.0)
    w = w.astype(BF16)

    @pl.when(kk == 0)
    def _():
        o_ref[...] = r_ref[...]

    a = a_ref[...]
    if ragged:
        a = jnp.where(_iota(a.shape, 1) < valid, a, jnp.zeros_like(a))
    for c0 in range(0, w.shape[1], rc):
        o_ref[:, c0:c0 + rc] += _dot(a, w[:, c0:c0 + rc])


def _mm_acc(a, w, res):
    m, k = a.shape
    n = w.shape[1]
    tm = _pick(m, (2048, 1024, 512, 256, 128, 64, 32, 16, 8))
    tn = _pick(n, (1024, 512, 256, 128))
    tk = 512
    nk = pl.cdiv(k, tk)
    return pl.pallas_call(
        functools.partial(_mm_acc_body, k_total=k, tk=tk, rc=min(tn, 2 * LANES)),
        grid=(m // tm, n // tn, nk),
        in_specs=[pl.BlockSpec((tm, tk), lambda i, j, kk: (i, kk)), pl.BlockSpec((tk, tn), lambda i, j, kk: (kk, j)),
                  pl.BlockSpec((tm, tn), lambda i, j, kk: (i, j))],
        out_specs=pl.BlockSpec((tm, tn), lambda i, j, kk: (i, j)),
        out_shape=jax.ShapeDtypeStruct((m, n), F32),
        compiler_params=_cparams(("parallel", "parallel", "arbitrary")),
        name="proj_down",
    )(a, w, res)


MOE_TM = 512
MOE_BLK = 256
MOE_ALIGN = 16


def _moe_rank_body(comb_ref, rank_ref, cnt_ref):
    comb = comb_ref[...]
    tt = comb.shape[0]
    lane = _iota(comb.shape, 1)
    sel = jnp.where((comb > 0.0) & (lane < N_EXPERTS), 1.0, 0.0)
    below = jnp.where(_iota((tt, tt), 0) > _iota((tt, tt), 1), 1.0, 0.0).astype(BF16)
    rank = _dot(below, sel.astype(BF16))
    rank_ref[...] = rank
    cnt_ref[...] = rank[tt - 1:tt, :] + sel[tt - 1:tt, :]


def _moe_rank(comb, tt):
    m = comb.shape[0]
    nt = m // tt
    return pl.pallas_call(
        _moe_rank_body,
        grid=(nt,),
        in_specs=[pl.BlockSpec((tt, LANES), lambda i: (i, 0))],
        out_specs=[pl.BlockSpec((tt, LANES), lambda i: (i, 0)), pl.BlockSpec((None, 1, LANES), lambda i: (i, 0, 0))],
        out_shape=[jax.ShapeDtypeStruct((m, LANES), F32), jax.ShapeDtypeStruct((nt, 1, LANES), F32)],
        compiler_params=_cparams(("parallel",)),
        name="moe_rank",
    )(comb)


def _moe_dispatch_body(seg_ref, base_ref, x_ref, rank_ref, comb_ref, xg_in_ref, xg_ref, buf, sem, *, tt):
    del xg_in_ref
    i = pl.program_id(0)
    x = x_ref[...]
    rank_t = rank_ref[...].T
    comb_t = comb_ref[...].T
    rr = _iota((MOE_BLK, tt), 0)

    def copy(slot, row0):
        return pltpu.make_async_copy(buf.at[slot], xg_ref.at[pl.ds(pl.multiple_of(row0, MOE_ALIGN), MOE_BLK)], sem.at[slot])

    def block(e, b, slot, base_e):
        hit = (rank_t[e:e + 1, :] == (rr + b * MOE_BLK).astype(F32)) & (comb_t[e:e + 1, :] > 0.0)
        onehot = jnp.where(hit, 1.0, 0.0).astype(BF16)
        buf[slot] = _dot(onehot, x).astype(buf.dtype)
        copy(slot, base_e + b * MOE_BLK).start()

    for e in range(N_EXPERTS):
        slot = e % 2
        seg_e = seg_ref[i * N_EXPERTS + e]
        base_e = base_ref[i * N_EXPERTS + e]
        if e >= 2:
            copy(slot, 0).wait()
        block(e, 0, slot, base_e)

        def extra(b, carry, e=e, slot=slot, base_e=base_e):
            copy(slot, 0).wait()
            block(e, b, slot, base_e)
            return carry

        lax.fori_loop(1, (seg_e + MOE_BLK - 1) // MOE_BLK, extra, 0)
    copy(0, 0).wait()
    copy(1, 0).wait()


def _moe_dispatch(hn, rank, comb, seg, base, p_max, tt):
    m, d = hn.shape
    return pl.pallas_call(
        functools.partial(_moe_dispatch_body, tt=tt),
        grid_spec=pltpu.PrefetchScalarGridSpec(
            num_scalar_prefetch=2,
            grid=(m // tt,),
            in_specs=[pl.BlockSpec((tt, d), lambda i, *_: (i, 0)), pl.BlockSpec((tt, LANES), lambda i, *_: (i, 0)),
                      pl.BlockSpec((tt, LANES), lambda i, *_: (i, 0)), pl.BlockSpec(memory_space=pl.ANY)],
            out_specs=pl.BlockSpec(memory_space=pl.ANY),
            scratch_shapes=[pltpu.VMEM((2, MOE_BLK, d), BF16), pltpu.SemaphoreType.DMA((2,))],
        ),
        out_shape=jax.ShapeDtypeStruct((p_max, d), BF16),
        input_output_aliases={5: 0},
        compiler_params=_cparams(("arbitrary",)),
        name="moe_dispatch",
    )(seg, base, hn, rank, comb, jnp.zeros((p_max, d), BF16))


def _moe_gup_body(te_ref, nv_ref, x_ref, w1_ref, w3_ref, o_ref):
    k = pl.program_id(1)

    @pl.when(k < nv_ref[0])
    def _():
        x = x_ref[...]
        h1 = _dot(x, w1_ref[...].astype(BF16))
        h3 = _dot(x, w3_ref[...].astype(BF16))
        o_ref[...] = (h1 * _sigmoid(h1) * h3).astype(o_ref.dtype)

    @pl.when(k >= nv_ref[0])
    def _():
        o_ref[...] = jnp.zeros_like(o_ref)


def _moe_gdown_body(te_ref, nv_ref, h_ref, w_ref, o_ref):
    k = pl.program_id(1)

    @pl.when(k < nv_ref[0])
    def _():
        o_ref[...] = _dot(h_ref[...], w_ref[...].astype(BF16)).astype(o_ref.dtype)

    @pl.when(k >= nv_ref[0])
    def _():
        o_ref[...] = jnp.zeros_like(o_ref)


def _moe_grouped(xg, te, nv, w1, w3, w2):
    p, d = xg.shape
    _, _, f = w1.shape
    tm = MOE_TM
    nk = p // tm
    tn = 4 * LANES
    rowi = lambda k, nv_: jnp.minimum(k, nv_[0] - 1)
    hid = pl.pallas_call(
        _moe_gup_body,
        grid_spec=pltpu.PrefetchScalarGridSpec(
            num_scalar_prefetch=2,
            grid=(f // tn, nk),
            in_specs=[pl.BlockSpec((tm, d), lambda j, k, te_, nv_: (rowi(k, nv_), 0)),
                      pl.BlockSpec((None, d, tn), lambda j, k, te_, nv_: (te_[rowi(k, nv_)], 0, j)),
                      pl.BlockSpec((None, d, tn), lambda j, k, te_, nv_: (te_[rowi(k, nv_)], 0, j))],
            out_specs=pl.BlockSpec((tm, tn), lambda j, k, te_, nv_: (k, j)),
        ),
        out_shape=jax.ShapeDtypeStruct((p, f), BF16),
        compiler_params=_cparams(("arbitrary", "arbitrary")),
        name="moe_up",
    )(te, nv, xg, w1, w3)
    tn2 = 8 * LANES
    return pl.pallas_call(
        _moe_gdown_body,
        grid_spec=pltpu.PrefetchScalarGridSpec(
            num_scalar_prefetch=2,
            grid=(d // tn2, nk),
            in_specs=[pl.BlockSpec((tm, f), lambda j, k, te_, nv_: (rowi(k, nv_), 0)),
                      pl.BlockSpec((None, f, tn2), lambda j, k, te_, nv_: (te_[rowi(k, nv_)], 0, j))],
            out_specs=pl.BlockSpec((tm, tn2), lambda j, k, te_, nv_: (k, j)),
        ),
        out_shape=jax.ShapeDtypeStruct((p, d), BF16),
        compiler_params=_cparams(("arbitrary", "arbitrary")),
        name="moe_down",
    )(te, nv, hid, w2)


def _moe_combine_body(seg_ref, base_ref, x_ref, rank_ref, comb_ref, g_ref, y_ref, o_ref, buf, sem, *, tt, tile0):
    i = pl.program_id(0) + tile0
    rank = rank_ref[...]
    comb = comb_ref[...]
    cc = _iota((tt, MOE_BLK), 1)

    def copy(slot, row0):
        return pltpu.make_async_copy(y_ref.at[pl.ds(pl.multiple_of(row0, MOE_ALIGN), MOE_BLK)], buf.at[slot], sem.at[slot])

    def gathered(e, b, slot):
        hit = (rank[:, e:e + 1] == (cc + b * MOE_BLK).astype(F32)) & (comb[:, e:e + 1] > 0.0)
        return comb[:, e:e + 1] * _dot(jnp.where(hit, 1.0, 0.0).astype(BF16), buf[slot])

    o_ref[...] = x_ref[...]
    copy(0, base_ref[i * N_EXPERTS]).start()
    for e in range(N_EXPERTS):
        slot = e % 2
        seg_e = seg_ref[i * N_EXPERTS + e]
        base_e = base_ref[i * N_EXPERTS + e]
        if e + 1 < N_EXPERTS:
            copy(1 - slot, base_ref[i * N_EXPERTS + e + 1]).start()
        copy(slot, 0).wait()
        o_ref[...] += gathered(e, 0, slot)

        def extra(b, carry, e=e, slot=slot, base_e=base_e):
            copy(slot, base_e + b * MOE_BLK).start()
            copy(slot, 0).wait()
            o_ref[...] += gathered(e, b, slot)
            return carry

        lax.fori_loop(1, (seg_e + MOE_BLK - 1) // MOE_BLK, extra, 0)
    acc = o_ref[...]
    ms = jnp.mean(acc * acc, axis=-1, keepdims=True)
    o_ref[...] = acc * lax.rsqrt(ms + RMS_EPS) * g_ref[...]


def _moe_combine(x, rank, comb, y, seg, base, tt, norm_g, tile0):
    m, d = x.shape
    return pl.pallas_call(
        functools.partial(_moe_combine_body, tt=tt, tile0=tile0),
        grid_spec=pltpu.PrefetchScalarGridSpec(
            num_scalar_prefetch=2,
            grid=(m // tt,),
            in_specs=[pl.BlockSpec((tt, d), lambda i, *_: (i, 0)), pl.BlockSpec((tt, LANES), lambda i, *_: (i + tile0, 0)),
                      pl.BlockSpec((tt, LANES), lambda i, *_: (i + tile0, 0)), pl.BlockSpec((1, d), lambda i, *_: (0, 0)),
                      pl.BlockSpec(memory_space=pl.ANY)],
            out_specs=pl.BlockSpec((tt, d), lambda i, *_: (i, 0)),
            scratch_shapes=[pltpu.VMEM((2, MOE_BLK, d), BF16), pltpu.SemaphoreType.DMA((2,))],
        ),
        out_shape=jax.ShapeDtypeStruct((m, d), F32),
        compiler_params=_cparams(("arbitrary",)),
        name="moe_combine",
    )(seg, base, x, rank, comb, norm_g.reshape(1, d), y)


def _moe_sparse(xs, hns, combs, w1, w3, w2, norm_g):
    hn = jnp.concatenate(hns, axis=0)
    comb = jnp.concatenate(combs, axis=0)
    m, d = hn.shape
    ne = N_EXPERTS
    tt = _pick(math.gcd(*[x.shape[0] for x in xs]), (512, 256, 128, 64, 32, 16))
    nt = m // tt
    tm = MOE_TM
    rank, cnt = _moe_rank(comb, tt)
    cnt = cnt[:, 0, :ne].astype(jnp.int32)
    seg = (cnt + MOE_ALIGN - 1) // MOE_ALIGN * MOE_ALIGN
    grp = (jnp.sum(seg, axis=0) + MOE_BLK + tm - 1) // tm * tm
    ends = jnp.cumsum(grp)
    base = (ends - grp)[None, :] + jnp.cumsum(seg, axis=0) - seg
    p_max = (2 * m + nt * ne * MOE_ALIGN + ne * (MOE_BLK + tm) + tm - 1) // tm * tm
    nk = p_max // tm
    te = jnp.minimum(jnp.sum(ends[None, :] <= (jnp.arange(nk, dtype=jnp.int32) * tm)[:, None], axis=1), ne - 1).astype(jnp.int32)
    nv = (ends[-1] // tm).astype(jnp.int32).reshape(1)
    seg = seg.reshape(-1)
    base = base.astype(jnp.int32).reshape(-1)
    xg = _moe_dispatch(hn, rank, comb, seg, base, p_max, tt)
    y = _moe_grouped(xg, te, nv, w1, w3, w2)
    outs, tile0 = [], 0
    for x in xs:
        outs.append(_moe_combine(x, rank, comb, y, seg, base, tt, norm_g, tile0))
        tile0 += x.shape[0] // tt
    return outs


def _rwkv_body(zr_ref, zk_ref, zv_ref, zdw_ref, zda_ref, zdg_ref, mu_ref, sh_ref, s0_ref,
               w0_ref, wup_ref, a0_ref, aup_ref, gup_ref, kk_ref, ka_ref, rk_ref, lw_ref, lb_ref,
               y_ref, sT_ref, s_scr, carry_scr, *, bb, rt, chunk, valid, unroll, npp):
    n = pl.program_id(2)
    n_last = pl.num_programs(2) - 1
    L = chunk
    L2 = 2 * L
    H = RWKV_HEAD
    nchunk = rt // L
    nsq = max(1, int(math.ceil(math.log2(L))))

    lane = _iota((1, LANES), 1)
    m0 = (lane < H).astype(F32)
    m1 = 1.0 - m0
    r2 = _iota((L2, L2), 0)
    c2 = _iota((L2, L2), 1)
    rh = jnp.where(r2 >= L, 1, 0)
    ch = jnp.where(c2 >= L, 1, 0)
    same = jnp.where(rh == ch, 1.0, 0.0)
    tdiff = (r2 - L * rh) - (c2 - L * ch)
    strict = same * jnp.where(tdiff > 0, 1.0, 0.0)
    incl = same * jnp.where(tdiff >= 0, 1.0, 0.0)
    rl = _iota((L, L), 0)
    cl = _iota((L, L), 1)
    tril = jnp.where(rl >= cl, 1.0, 0.0).astype(BF16)
    ri = _iota((LANES, LANES), 0)
    ci = _iota((LANES, LANES), 1)
    bd = jnp.where((ri >= H) == (ci >= H), 1.0, 0.0)
    bones = bd.astype(BF16)
    row = _iota((L, 1), 0)

    @pl.when(n == 0)
    def _():
        s_scr[...] = jnp.zeros_like(s_scr)
        carry_scr[...] = jnp.zeros_like(carry_scr)

    lanes = [slice(p * LANES, (p + 1) * LANES) for p in range(npp)]
    pw = [dict(mu=mu_ref[p], w0=w0_ref[:, ls], a0=a0_ref[:, ls], wup=_hp_rhs(wup_ref[:, ls]), aup=_hp_rhs(aup_ref[:, ls]),
               gup=_hp_rhs(gup_ref[:, ls]), kk=kk_ref[:, ls], ka=ka_ref[:, ls], rk=rk_ref[:, ls], lnw=lw_ref[:, ls],
               lnb=lb_ref[:, ls]) for p, ls in enumerate(lanes)]

    def block_sum(x):
        return _dot_const_rhs(x, bones)

    def load(seq, p, r0, is_start):
        rows = pl.ds(r0, L)
        zc = jnp.concatenate([zr_ref[seq, rows, lanes[p]], zk_ref[seq, rows, lanes[p]], zv_ref[seq, rows, lanes[p]],
                              zdw_ref[seq, rows, :], zda_ref[seq, rows, :], zdg_ref[seq, rows, :]], axis=1)
        first = jnp.where(is_start, sh_ref[p, pl.ds(seq, 1), :], carry_scr[seq, p, 0:1, :])
        s0 = s0_ref[seq, 2 * p]
        s1 = s0_ref[seq, 2 * p + 1]
        zz = jnp.zeros((H, H), F32)
        s_init = jnp.concatenate([jnp.concatenate([s0, zz], axis=1), jnp.concatenate([zz, s1], axis=1)], axis=0)
        st = jnp.where(is_start, s_init, s_scr[seq, p])
        return zc, first, st

    def recur(seqs):
        pre = []
        for r, k, v, ka, kb, lw, cum, st in seqs:
            ce = jnp.exp(cum)
            cinv = jnp.exp(-cum)
            at = jnp.exp(cum - lw) * ka
            rt_ = ce * r
            bt = kb * cinv
            kt = k * cinv
            c_last = ce[L - 1:L, :]
            lhs = jnp.concatenate([at * m0, at * m1, rt_ * m0, rt_ * m1], axis=0)
            rhs = jnp.concatenate([bt * m0, bt * m1, kt * m0, kt * m1, st], axis=0)
            bk = jnp.concatenate([bt * c_last, kt * c_last], axis=0)
            pre.append((lhs, rhs, bk, c_last))
        gms = [_dot_x(p_[0], p_[1], NT) for p_ in pre]
        wvs = [_dot_x(jnp.concatenate([gm[0:L2, L2:2 * L2] * strict, gm[L2:, L2:2 * L2] * incl], axis=0),
                      jnp.concatenate([s[2], s[2]], axis=0)) for gm, s in zip(gms, seqs)]
        xs = [gm[0:L2, 2 * L2:] + wv[0:L2] for gm, wv in zip(gms, wvs)]
        ps = [gm[0:L2, 0:L2] * strict for gm in gms]
        for q in range(nsq):
            if q + 1 < nsq:
                xps = [_dot_x(p, jnp.concatenate([x, p], axis=1)) for x, p in zip(xs, ps)]
                xs = [x + xp[:, 0:LANES] for x, xp in zip(xs, xps)]
                ps = [xp[:, LANES:] for xp in xps]
            else:
                xs = [x + _dot_x(p, x) for x, p in zip(xs, ps)]
        us = [x[0:L, :] * m0 + x[L:L2, :] * m1 for x in xs]
        rus = [_dot_x(gm[L2:, 0:L2] * incl, jnp.concatenate([u, u], axis=0)) for gm, u in zip(gms, us)]
        sus = [_dot_x(jnp.concatenate([u, s[2]], axis=0), p_[2], TN) for u, s, p_ in zip(us, seqs, pre)]
        out = []
        for gm, wv, ru, su, s, p_ in zip(gms, wvs, rus, sus, seqs, pre):
            y2 = gm[L2:, 2 * L2:] + wv[L2:] + ru
            out.append((y2[0:L, :] * m0 + y2[L:L2, :] * m1, s[7] * p_[3] + bd * su))
        return out

    part = lambda x, u: x[u * L:(u + 1) * L]

    def prepare(p, loaded):
        nu = len(loaded)
        c = pw[p]
        stack = lambda xs: xs[0] if nu == 1 else jnp.concatenate(xs, axis=0)
        zc = stack([x[0] for x in loaded])
        zprev = stack([jnp.where(row == 0, x[1], pltpu.roll(x[0], 1, 0)) for x in loaded])
        zs = zc + c['mu'] * (zprev - zc)
        r = zs[:, 0:LANES]
        k = zs[:, LANES:2 * LANES]
        v = zs[:, 2 * LANES:3 * LANES]
        dw = zs[:, 3 * LANES:4 * LANES]
        da = zs[:, 4 * LANES:5 * LANES]
        dg = zs[:, 5 * LANES:]

        w_log = -_softplus(-(c['w0'] + _dot(_hp_lhs(jnp.tanh(dw)), c['wup']))) - 0.5
        lw = -jnp.exp(w_log)
        a = _sigmoid(c['a0'] + _dot(_hp_lhs(da), c['aup']))
        g = _dot(_hp_lhs(_sigmoid(dg)), c['gup'])
        kk = k * c['kk']
        k = k * (1.0 + (a - 1.0) * c['ka'])
        rows = nu * L
        sums = block_sum(jnp.concatenate([kk * kk, r * k * c['rk']], axis=0))
        kk = kk / jnp.maximum(jnp.sqrt(sums[0:rows]), 1e-12)
        bonus = sums[rows:] * v
        ka = -kk
        kb = kk * a
        if valid < L:
            ok = stack([row < valid] * nu)
            lw = jnp.where(ok, lw, 0.0)
            r = jnp.where(ok, r, 0.0)
            k = jnp.where(ok, k, 0.0)
            v = jnp.where(ok, v, 0.0)
            ka = jnp.where(ok, ka, 0.0)
            kb = jnp.where(ok, kb, 0.0)

        lw_wide = lw if nu == 1 else jnp.concatenate([part(lw, u) for u in range(nu)], axis=1)
        cum_wide = _dot_const(tril, lw_wide)
        chains = [(part(r, u), part(k, u), part(v, u), part(ka, u), part(kb, u), part(lw, u),
                   cum_wide[:, u * LANES:(u + 1) * LANES], loaded[u][2]) for u in range(nu)]
        return chains, bonus, g

    def finish(p, loaded, res, bonus, g):
        nu = len(loaded)
        c = pw[p]
        y = res[0][0] if nu == 1 else jnp.concatenate([x[0] for x in res], axis=0)
        mean = block_sum(y) * (1.0 / H)
        yc = y - mean
        var = block_sum(yc * yc) * (1.0 / H)
        yn = yc * lax.rsqrt(var + RWKV_LNX_EPS) * c['lnw'] + c['lnb']
        out = (yn + bonus) * g
        return [(part(out, u).astype(y_ref.dtype), res[u][1], loaded[u][0][valid - 1:valid, :]) for u in range(nu)]

    def store(seq, p, r0, is_end, y, st, last_row):
        y_ref[seq, pl.ds(r0, L), lanes[p]] = y
        s_scr[seq, p] = st
        carry_scr[seq, p, 0:1, :] = last_row

        @pl.when(is_end)
        def _():
            sT_ref[seq, 2 * p] = st[0:H, 0:H]
            sT_ref[seq, 2 * p + 1] = st[H:2 * H, H:2 * H]

    def step(it, carry):
        sg = it // nchunk
        ci_ = it - sg * nchunk
        r0 = pl.multiple_of(ci_ * L, L)
        is_start = jnp.logical_and(n == 0, ci_ == 0)
        is_end = jnp.logical_and(n == n_last, ci_ == nchunk - 1)
        seqs = [sg * unroll + u for u in range(unroll)]
        loaded = [[load(s, p, r0, is_start) for s in seqs] for p in range(npp)]
        prepared = [prepare(p, loaded[p]) for p in range(npp)]
        res = recur([ch for pr in prepared for ch in pr[0]])
        for p in range(npp):
            done = finish(p, loaded[p], res[p * unroll:(p + 1) * unroll], prepared[p][1], prepared[p][2])
            for s, d in zip(seqs, done):
                store(s, p, r0, is_end, *d)
        return carry

    lax.fori_loop(0, (bb // unroll) * nchunk, step, 0)


def _dot_x(a, b, dims=NN):
    return _dot(a.astype(BF16), b.astype(BF16), dims)


def _dot_const_rhs(a, c_bf16):
    return _dot(jnp.concatenate(_split3(a), axis=1).astype(BF16), jnp.concatenate([c_bf16] * 3, axis=0))


def _rwkv(z, zt, shift_p, wkv0, mu_p, w0, wup_p, a0, aup_p, g_up, k_k, k_a, r_k, lnx_w, lnx_b,
          *, n_seq, seq_rows, chunk, valid):
    unroll = _pick(n_seq, (4, 2, 1))
    if seq_rows > chunk:
        rt = _pick(seq_rows, (256, 128, 64))
        bb = unroll
    else:
        rt = seq_rows
        bb = _pick(n_seq, (32, 16, 8, 4, 2, 1))
    assert rt % chunk == 0 and bb % unroll == 0 and (bb == n_seq or bb % SUBLANES == 0)
    sblk = lambda c, g, n: (g, c, 0, 0)
    npair = RWKV_HEADS // 2
    nrb = LANES

    def cat(a):
        rows = a.shape[0]
        rkv = a[:, :3 * MIX_A].reshape(rows, 3, npair, LANES).transpose(2, 0, 1, 3).reshape(npair, rows, 3 * LANES)
        tail = jnp.broadcast_to(a[None, :, 3 * MIX_A:], (npair, rows, ZE_U - 3 * MIX_A))
        return jnp.concatenate([rkv, tail], axis=2)

    mu_cat = cat(mu_p.reshape(1, ZE_U))
    sh_cat = cat(shift_p)
    row2 = lambda a: a.reshape(1, MIX_A)
    npp = 2
    nrb = npp * LANES
    nblk = npair // npp
    vec = pl.BlockSpec((1, nrb), lambda c, g, n: (0, c))
    kern = functools.partial(_rwkv_body, bb=bb, rt=rt, chunk=chunk, valid=valid, unroll=unroll, npp=npp)
    y, s_t = pl.pallas_call(
        kern,
        grid=(nblk, n_seq // bb, seq_rows // rt),
        in_specs=[
            pl.BlockSpec((bb, rt, nrb), lambda c, g, n: (g, n, c)),
            pl.BlockSpec((bb, rt, nrb), lambda c, g, n: (g, n, nblk + c)),
            pl.BlockSpec((bb, rt, nrb), lambda c, g, n: (g, n, 2 * nblk + c)),
            pl.BlockSpec((bb, rt, RANK_PAD), lambda c, g, n: (g, n, ZT_DW // RANK_PAD)),
            pl.BlockSpec((bb, rt, RANK_PAD), lambda c, g, n: (g, n, ZT_DA // RANK_PAD)),
            pl.BlockSpec((bb, rt, RWKV_GATE_RANK), lambda c, g, n: (g, n, ZT_DG // RWKV_GATE_RANK)),
            pl.BlockSpec((npp, 1, RW_CAT), lambda c, g, n: (c, 0, 0)),
            pl.BlockSpec((npp, bb, RW_CAT), lambda c, g, n: (c, g, 0)),
            pl.BlockSpec((bb, 2 * npp, RWKV_HEAD, RWKV_HEAD), sblk),
            vec,
            pl.BlockSpec((RANK_PAD, nrb), lambda c, g, n: (0, c)),
            vec,
            pl.BlockSpec((RANK_PAD, nrb), lambda c, g, n: (0, c)),
            pl.BlockSpec((RWKV_GATE_RANK, nrb), lambda c, g, n: (0, c)),
            vec, vec, vec, vec, vec,
        ],
        out_specs=[pl.BlockSpec((bb, rt, nrb), lambda c, g, n: (g, n, c)),
                   pl.BlockSpec((bb, 2 * npp, RWKV_HEAD, RWKV_HEAD), sblk)],
        out_shape=[jax.ShapeDtypeStruct((n_seq, seq_rows, MIX_A), BF16),
                   jax.ShapeDtypeStruct((n_seq, RWKV_HEADS, RWKV_HEAD, RWKV_HEAD), F32)],
        scratch_shapes=[pltpu.VMEM((bb, npp, LANES, LANES), F32), pltpu.VMEM((bb, npp, SUBLANES, RW_CAT), F32)],
        compiler_params=_cparams(("parallel", "parallel", "arbitrary")),
        name="rwkv7",
    )(z, z, z, zt, zt, zt, mu_cat, sh_cat, wkv0,
      row2(w0), wup_p, row2(a0), aup_p, g_up, row2(k_k), row2(k_a), row2(r_k), row2(lnx_w), row2(lnx_b))
    return y, s_t


def _rwkv_cat(a):
    npair = RWKV_HEADS // 2
    rows = a.shape[0]
    rkv = a[:, :3 * MIX_A].reshape(rows, 3, npair, LANES).transpose(2, 0, 1, 3).reshape(npair, rows, 3 * LANES)
    tail = jnp.broadcast_to(a[None, :, 3 * MIX_A:], (npair, rows, ZE_U - 3 * MIX_A))
    return jnp.concatenate([rkv, tail], axis=2)


def _rwkv_step_body(zr_ref, zk_ref, zv_ref, zdw_ref, zda_ref, zdg_ref, mu_ref, sh_ref, s0_ref,
                    w0_ref, wup_ref, a0_ref, aup_ref, gup_ref, kk_ref, ka_ref, rk_ref, lw_ref, lb_ref,
                    y_ref, sT_ref, op_scr, yt_scr, *, steps, nb):
    H = RWKV_HEAD
    rows = steps * nb
    ri = _iota((LANES, LANES), 0)
    ci = _iota((LANES, LANES), 1)
    bones = jnp.where((ri >= H) == (ci >= H), 1.0, 0.0).astype(BF16)
    block_sum = lambda x: _dot_const_rhs(x, bones)

    zc = jnp.concatenate([ref[...].reshape(rows, ref.shape[2]) for ref in (zr_ref, zk_ref, zv_ref, zdw_ref, zda_ref, zdg_ref)],
                         axis=1)
    zprev = jnp.concatenate([sh_ref[...], zc[0:rows - nb]], axis=0) if steps > 1 else sh_ref[...]
    zs = zc + mu_ref[...] * (zprev - zc)
    r = zs[:, 0:LANES]
    k = zs[:, LANES:2 * LANES]
    v = zs[:, 2 * LANES:3 * LANES]
    dw = zs[:, 3 * LANES:4 * LANES]
    da = zs[:, 4 * LANES:5 * LANES]
    dg = zs[:, 5 * LANES:]
    w_log = -_softplus(-(w0_ref[...] + _dot_hp(jnp.tanh(dw), wup_ref[...]))) - 0.5
    decay = jnp.exp(-jnp.exp(w_log))
    a = _sigmoid(a0_ref[...] + _dot_hp(da, aup_ref[...]))
    g = _dot_hp(_sigmoid(dg), gup_ref[...])
    kk = k * kk_ref[...]
    k = k * (1.0 + (a - 1.0) * ka_ref[...])
    sums = block_sum(jnp.concatenate([kk * kk, r * k * rk_ref[...]], axis=0))
    kk = kk / jnp.maximum(jnp.sqrt(sums[0:rows]), 1e-12)
    bonus = sums[rows:] * v

    for qi, x in enumerate((decay, -kk, kk * a, k, r, v)):
        for t in range(steps):
            op_scr[qi, t] = x[t * nb:(t + 1) * nb, :].T

    for t in range(steps):
        src = s0_ref if t == 0 else sT_ref
        for hh in range(2):
            lo = hh * H
            w_t = op_scr[0, t, lo:lo + H, :]
            ka_t = op_scr[1, t, lo:lo + H, :]
            kb_t = op_scr[2, t, lo:lo + H, :]
            k_t = op_scr[3, t, lo:lo + H, :]
            r_t = op_scr[4, t, lo:lo + H, :]

            def value_row(i, carry, src=src, hh=hh, lo=lo, t=t, w_t=w_t, ka_t=ka_t, kb_t=kb_t, k_t=k_t, r_t=r_t):
                s_i = src[hh, i]
                sa = jnp.sum(s_i * ka_t, axis=0, keepdims=True)
                v_i = op_scr[5, t, pl.ds(lo + i, 1), :]
                s_n = s_i * w_t + sa * kb_t + v_i * k_t
                sT_ref[hh, i] = s_n
                yt_scr[t, pl.ds(lo + i, 1), :] = jnp.sum(s_n * r_t, axis=0, keepdims=True)
                return carry

            lax.fori_loop(0, H, value_row, 0, unroll=4)

    y = jnp.concatenate([yt_scr[t].T for t in range(steps)], axis=0)
    mean = block_sum(y) * (1.0 / H)
    yc = y - mean
    var = block_sum(yc * yc) * (1.0 / H)
    yn = yc * lax.rsqrt(var + RWKV_LNX_EPS) * lw_ref[...] + lb_ref[...]
    y_ref[...] = ((yn + bonus) * g).reshape(steps, nb, LANES).astype(y_ref.dtype)


def _rwkv_step(z_tb, zt_tb, shift_p, wkv0_t, mu_p, w0, wup_p, a0, aup_p, g_up, k_k, k_a, r_k, lnx_w, lnx_b):
    steps, nb, _ = z_tb.shape
    assert nb == LANES
    npair = RWKV_HEADS // 2
    row2 = lambda a: a.reshape(1, MIX_A)
    vec = pl.BlockSpec((1, LANES), lambda c: (0, c))
    zspec = lambda width, idx: pl.BlockSpec((steps, nb, width), lambda c: (0, 0, idx(c)))
    sspec = pl.BlockSpec((2, RWKV_HEAD, RWKV_HEAD, nb), lambda c: (c, 0, 0, 0))
    kern = functools.partial(_rwkv_step_body, steps=steps, nb=nb)
    return pl.pallas_call(
        kern,
        grid=(npair,),
        in_specs=[
            zspec(LANES, lambda c: c), zspec(LANES, lambda c: npair + c), zspec(LANES, lambda c: 2 * npair + c),
            zspec(RANK_PAD, lambda c: ZT_DW // RANK_PAD), zspec(RANK_PAD, lambda c: ZT_DA // RANK_PAD),
            zspec(RWKV_GATE_RANK, lambda c: ZT_DG // RWKV_GATE_RANK),
            pl.BlockSpec((None, 1, RW_CAT), lambda c: (c, 0, 0)),
            pl.BlockSpec((None, nb, RW_CAT), lambda c: (c, 0, 0)),
            sspec,
            vec,
            pl.BlockSpec((RANK_PAD, LANES), lambda c: (0, c)),
            vec,
            pl.BlockSpec((RANK_PAD, LANES), lambda c: (0, c)),
            pl.BlockSpec((RWKV_GATE_RANK, LANES), lambda c: (0, c)),
            vec, vec, vec, vec, vec,
        ],
        out_specs=[pl.BlockSpec((steps, nb, LANES), lambda c: (0, 0, c)), sspec],
        out_shape=[jax.ShapeDtypeStruct((steps, nb, MIX_A), BF16),
                   jax.ShapeDtypeStruct((RWKV_HEADS, RWKV_HEAD, RWKV_HEAD, nb), F32)],
        scratch_shapes=[pltpu.VMEM((6, steps, LANES, nb), F32), pltpu.VMEM((steps, LANES, nb), F32)],
        compiler_params=_cparams(("parallel",)),
        name="rwkv7_step",
    )(z_tb, z_tb, z_tb, zt_tb, zt_tb, zt_tb, _rwkv_cat(mu_p.reshape(1, ZE_U)), _rwkv_cat(shift_p), wkv0_t,
      row2(w0), wup_p, row2(a0), aup_p, g_up, row2(k_k), row2(k_a), row2(r_k), row2(lnx_w), row2(lnx_b))


def _s5_prep_body(are_ref, aim_ref, ls_ref, abre_ref, abim_ref, zr_ref, zi_ref):
    lam_re = are_ref[...]
    lam_im = aim_ref[...]
    dt = jnp.exp(ls_ref[...])
    mag = jnp.exp(lam_re * dt)
    ab_re = mag * jnp.cos(lam_im * dt)
    ab_im = mag * jnp.sin(lam_im * dt)
    inv = 1.0 / (lam_re * lam_re + lam_im * lam_im)
    abre_ref[...] = ab_re
    abim_ref[...] = ab_im
    zr_ref[...] = ((ab_re - 1.0) * lam_re + ab_im * lam_im) * inv
    zi_ref[...] = (ab_im * lam_re - (ab_re - 1.0) * lam_im) * inv


def _s5_prep(a_re, a_im, log_step):
    g, n = a_re.shape
    sd = jax.ShapeDtypeStruct((g, n), F32)
    return pl.pallas_call(_s5_prep_body, out_shape=[sd, sd, sd, sd], name="s5_discretise")(
        a_re, a_im, jnp.broadcast_to(log_step.reshape(g, 1), (g, n)))


def _gelu_tanh(x):
    return 0.5 * x * (1.0 + jnp.tanh(math.sqrt(2.0 / math.pi) * (x + 0.044715 * (x * x * x))))


def _s5_body(u_ref, wbr_ref, wbi_ref, wcr_ref, wci_ref, abr_ref, abi_ref, d_ref, h0r_ref, h0i_ref,
             y_ref, hTr_ref, hTi_ref, hr_scr, hi_scr, pr_scr, pi_scr, cr_scr, ci_scr, u_st, y_st, *, rt, seq_rows):
    n = pl.program_id(1)
    long_seq = seq_rows >= rt
    grp = SUBLANES if long_seq else rt // seq_rows
    steps = rt // grp
    ar1 = abr_ref[...]
    ai1 = abi_ref[...]
    ar = jnp.broadcast_to(ar1, (grp, ar1.shape[1]))
    ai = jnp.broadcast_to(ai1, (grp, ai1.shape[1]))
    nhalf = u_ref.shape[1] // LANES
    for hf in range(nhalf):
        u_st[hf] = u_ref[:, hf * LANES:(hf + 1) * LANES]
    u = jnp.concatenate(
        [jnp.concatenate([u_st[hf, pl.ds(j, grp, stride=steps), :] for hf in range(nhalf)], axis=1) for j in range(steps)],
        axis=0)
    ub = u.astype(BF16)
    hr_scr[...] = _dot(ub, wbr_ref[...].astype(BF16))
    hi_scr[...] = _dot(ub, wbi_ref[...].astype(BF16))

    def rows(j):
        return pl.ds(j * grp, grp) if isinstance(j, int) else pl.ds(pl.multiple_of(j * grp, grp), grp)

    def scan_step(j, c):
        hr, hi = c
        nr = ar * hr - ai * hi + hr_scr[rows(j), :]
        ni = ar * hi + ai * hr + hi_scr[rows(j), :]
        hr_scr[rows(j), :] = nr
        hi_scr[rows(j), :] = ni
        return nr, ni

    if long_seq:
        @pl.when(n == 0)
        def _():
            def pw_step(j, c):
                qr, qi = c
                pr_scr[pl.ds(j, 1), :] = qr
                pi_scr[pl.ds(j, 1), :] = qi
                return qr * ar1 - qi * ai1, qr * ai1 + qi * ar1
            lax.fori_loop(0, steps, pw_step, (ar1, ai1))
            cr_scr[...] = jnp.zeros_like(cr_scr)
            ci_scr[...] = jnp.zeros_like(ci_scr)

        b = (n * rt) // seq_rows
        is_start = ((n * rt) % seq_rows) == 0
        h_in_r = jnp.where(is_start, h0r_ref[pl.ds(b, 1), :], cr_scr[0:1, :])
        h_in_i = jnp.where(is_start, h0i_ref[pl.ds(b, 1), :], ci_scr[0:1, :])
        zero = jnp.zeros((grp, hr_scr.shape[1]), F32)
        er, ei = lax.fori_loop(0, steps, scan_step, (zero, zero), unroll=4)
        pS_r = pr_scr[steps - 1:steps, :]
        pS_i = pi_scr[steps - 1:steps, :]
        cr_rows, ci_rows = [h_in_r], [h_in_i]
        for s in range(grp):
            pr_, pi_ = cr_rows[-1], ci_rows[-1]
            cr_rows.append(er[s:s + 1, :] + pS_r * pr_ - pS_i * pi_)
            ci_rows.append(ei[s:s + 1, :] + pS_r * pi_ + pS_i * pr_)
        c_r = jnp.concatenate(cr_rows[:grp], axis=0)
        c_i = jnp.concatenate(ci_rows[:grp], axis=0)
        cr_scr[0:1, :] = cr_rows[grp]
        ci_scr[0:1, :] = ci_rows[grp]
        hTr_ref[pl.ds(b, 1), :] = cr_rows[grp]
        hTi_ref[pl.ds(b, 1), :] = ci_rows[grp]

        def fix_step(j, carry):
            qr = pr_scr[pl.ds(j, 1), :]
            qi = pi_scr[pl.ds(j, 1), :]
            hr_scr[rows(j), :] = hr_scr[rows(j), :] + (qr * c_r - qi * c_i)
            hi_scr[rows(j), :] = hi_scr[rows(j), :] + (qr * c_i + qi * c_r)
            return carry

        lax.fori_loop(0, steps, fix_step, 0, unroll=4)
    else:
        hr, hi = h0r_ref[...], h0i_ref[...]
        for t in range(steps):
            hr, hi = scan_step(t, (hr, hi))
        hTr_ref[...] = hr
        hTi_ref[...] = hi

    y = _dot(hr_scr[...].astype(BF16), wcr_ref[...].astype(BF16)) + _dot(hi_scr[...].astype(BF16), wci_ref[...].astype(BF16))
    y = _gelu_tanh(y + d_ref[...] * u)
    for hf in range(nhalf):
        y_st[hf] = y[:, hf * LANES:(hf + 1) * LANES]
    for hf in range(nhalf):
        for s in range(grp):
            y_ref[s * steps:(s + 1) * steps, hf * LANES:(hf + 1) * LANES] = y_st[hf, pl.ds(s, steps, stride=grp), :]


def _s5_scan(z, col0, wbr, wbi, wcr, wci, ab_re, ab_im, d_skip, h0_re, h0_im, *, n_seq, seq_rows):
    m = z.shape[0]
    st = S5_SLAB_STATE
    if seq_rows >= 8 * SUBLANES:
        rt = _pick(seq_rows, (512, 256, 128, 64))
        hspec = pl.BlockSpec((n_seq, st), lambda s, n: (0, s))
        pw_rows = rt // SUBLANES
    else:
        rt = seq_rows * _pick(n_seq, (128, 64, 32, 16, 8))
        hspec = pl.BlockSpec((rt // seq_rows, st), lambda s, n: (n, s))
        pw_rows = SUBLANES
    win = S5_SLAB_IN
    cb0 = col0 // win
    kern = functools.partial(_s5_body, rt=rt, seq_rows=seq_rows)
    wspec_b = pl.BlockSpec((None, win, st), lambda s, n: (s, 0, 0))
    wspec_c = pl.BlockSpec((None, st, win), lambda s, n: (s, 0, 0))
    vspec = pl.BlockSpec((1, st), lambda s, n: (0, s))
    return pl.pallas_call(
        kern,
        grid=(S5_SLABS, m // rt),
        in_specs=[pl.BlockSpec((rt, win), lambda s, n: (n, cb0 + s)), wspec_b, wspec_b, wspec_c, wspec_c,
                  vspec, vspec, pl.BlockSpec((1, win), lambda s, n: (0, s)), hspec, hspec],
        out_specs=[pl.BlockSpec((rt, win), lambda s, n: (n, s)), hspec, hspec],
        out_shape=[jax.ShapeDtypeStruct((m, MIX_B), F32),
                   jax.ShapeDtypeStruct((n_seq, S5_GROUPS * S5_STATE), F32),
                   jax.ShapeDtypeStruct((n_seq, S5_GROUPS * S5_STATE), F32)],
        scratch_shapes=[pltpu.VMEM((rt, st), F32), pltpu.VMEM((rt, st), F32),
                        pltpu.VMEM((pw_rows, st), F32), pltpu.VMEM((pw_rows, st), F32),
                        pltpu.VMEM((SUBLANES, st), F32), pltpu.VMEM((SUBLANES, st), F32),
                        pltpu.VMEM((win // LANES, rt, LANES), F32), pltpu.VMEM((win // LANES, rt, LANES), F32)],
        compiler_params=_cparams(("parallel", "arbitrary")),
        name="s5_scan",
    )(z, wbr, wbi, wcr, wci, ab_re.reshape(1, -1), ab_im.reshape(1, -1), d_skip.reshape(1, -1), h0_re, h0_im)


def _glu_body(a_ref, w_ref, b_ref, y_ref, o_ref):
    t = _dot(a_ref[...].astype(BF16), w_ref[...].astype(BF16)) + b_ref[...]
    o_ref[...] = (y_ref[...] * _sigmoid(t)).astype(o_ref.dtype)


def _glu(y, w, b):
    m, k = y.shape
    n = w.shape[1]
    tm = _pick(m, (1024, 512, 256, 128, 64, 32, 16, 8))
    tn = _pick(n, (512, 256, 128))
    return pl.pallas_call(
        _glu_body,
        grid=(m // tm, n // tn),
        in_specs=[pl.BlockSpec((tm, k), lambda i, j: (i, 0)), pl.BlockSpec((k, tn), lambda i, j: (0, j)),
                  pl.BlockSpec((1, tn), lambda i, j: (0, j)), pl.BlockSpec((tm, tn), lambda i, j: (i, j))],
        out_specs=pl.BlockSpec((tm, tn), lambda i, j: (i, j)),
        out_shape=jax.ShapeDtypeStruct((m, n), BF16),
        compiler_params=_cparams(("parallel", "arbitrary")),
        name="s5_glu",
    )(y, w, b.reshape(1, n), y)


def _mlstm_body(q_ref, k_ref, v_ref, og_ref, gt_ref, gb_ref, nw_ref, c0_ref, n0_ref, m0_ref,
                y_ref, cT_ref, nT_ref, mT_ref, c_scr, n_scr, m_scr, *, bb, rt, chunk, valid, unroll):
    h = pl.program_id(0)
    n = pl.program_id(2)
    n_last = pl.num_programs(2) - 1
    L = chunk
    nchunk = rt // L
    rl = _iota((L, L), 0)
    cl = _iota((L, L), 1)
    causal = rl >= cl
    tril = jnp.where(causal, 1.0, 0.0).astype(BF16)
    row = _iota((L, 1), 0)
    lane = _iota((L, LANES), 1)
    gbias = gb_ref[...]
    nw = nw_ref[...]
    scale = MLSTM_DQK ** -0.5

    def load(seq, r0, is_start):
        c_old = jnp.where(is_start, c0_ref[seq, 0], c_scr[seq])
        n_old = jnp.where(is_start, n0_ref[seq, 0], n_scr[seq, 0:1, :])
        m_prev = jnp.where(is_start, m0_ref[seq, 0], m_scr[seq, 0:1, 0:1])
        rows = pl.ds(r0, L)
        return (q_ref[seq, rows, :], k_ref[seq, rows, :], v_ref[seq, rows, :], og_ref[seq, rows, :],
                gt_ref[seq, rows, :], c_old, n_old, m_prev)

    def compute_all(loaded):
        nu = len(loaded)
        igs, lfs = [], []
        for q, k, v, og, gt, c_old, n_old, m_prev in loaded:
            gt = gt + gbias
            ig_raw = jnp.sum(jnp.where(lane == h, gt, 0.0), axis=-1, keepdims=True)
            fg_raw = jnp.sum(jnp.where(lane == MLSTM_HEADS + h, gt, 0.0), axis=-1, keepdims=True)
            ig = GATE_CAP * jnp.tanh(ig_raw / GATE_CAP)
            lf = -_softplus(-(GATE_CAP * jnp.tanh(fg_raw / GATE_CAP)))
            if valid < L:
                ok = row < valid
                ig = jnp.where(ok, ig, NEG_BIG)
                lf = jnp.where(ok, lf, 0.0)
            igs.append(ig)
            lfs.append(lf)
        lf_mat = jnp.zeros((L, LANES), F32)
        for u in range(nu):
            lf_mat = jnp.where(lane == u, lfs[u], lf_mat)
        b_mat = _dot_const(tril, lf_mat)
        pack = b_mat
        for u in range(nu):
            pack = jnp.where(lane == nu + u, igs[u], pack)
        pack_t = pack.T
        gates, qbs, kbs, ks = [], [], [], []
        for u, (q, k, v, og, gt, c_old, n_old, m_prev) in enumerate(loaded):
            bcol = b_mat[:, u:u + 1]
            b_row = pack_t[u:u + 1, :]
            ig_row = pack_t[nu + u:nu + u + 1, :]
            log_d = jnp.where(causal, bcol - b_row + ig_row, NEG_BIG)
            log_p = bcol + m_prev
            m_tok = jnp.maximum(log_p, jnp.max(log_d, axis=-1, keepdims=True))
            gates.append((jnp.exp(log_d - m_tok), jnp.exp(log_p - m_tok), m_tok, bcol))
            k = k * scale
            ks.append(k)
            qbs.append(q.astype(BF16))
            kbs.append(k.astype(BF16))
        ss = [_dot(qb, kb, NT) * g_[0] for qb, kb, g_ in zip(qbs, kbs, gates)]
        svs = [_dot(s.astype(BF16), x[2].astype(BF16)) for s, x in zip(ss, loaded)]
        qcs = [_dot(qb, x[5].astype(BF16), NT) for qb, x in zip(qbs, loaded)]
        upd = []
        for u, (q, k, v, og, gt, c_old, n_old, m_prev) in enumerate(loaded):
            d, p, m_tok, bcol = gates[u]
            m_new = m_tok[L - 1:L, :]
            b_last = bcol[L - 1:L, :]
            w_col = jnp.exp(b_last - bcol + igs[u] - m_new)
            cs = jnp.exp(b_last + m_prev - m_new)
            upd.append((w_col, cs, m_new))
        vks = [_dot((x[2] * w_[0]).astype(BF16), kb, TN) for x, w_, kb in zip(loaded, upd, kbs)]
        out = []
        for u, (q, k, v, og, gt, c_old, n_old, m_prev) in enumerate(loaded):
            d, p, m_tok, bcol = gates[u]
            w_col, cs, m_new = upd[u]
            num = svs[u] + p * qcs[u]
            den = jnp.sum(ss[u], axis=-1, keepdims=True) + p * jnp.sum(q * n_old, axis=-1, keepdims=True)
            hh = num / jnp.maximum(jnp.abs(den), jnp.exp(-m_tok))
            c_new = cs * c_old + vks[u]
            n_new = cs * n_old + jnp.sum(w_col * ks[u], axis=0, keepdims=True)
            hn = hh * lax.rsqrt(jnp.mean(hh * hh, axis=-1, keepdims=True) + RMS_EPS)
            out.append(((hn * nw * _sigmoid(og)).astype(y_ref.dtype), c_new, n_new, m_new))
        return out

    def store(seq, r0, is_end, y, c_new, n_new, m_new):
        y_ref[seq, pl.ds(r0, L), :] = y
        c_scr[seq] = c_new
        n_scr[seq, 0:1, :] = n_new
        m_scr[seq, 0:1, :] = jnp.broadcast_to(m_new, (1, LANES))

        @pl.when(is_end)
        def _():
            cT_ref[seq, 0] = c_new
            nT_ref[seq, 0] = n_new
            mT_ref[seq, 0] = m_new

    def step(it, carry):
        sg = it // nchunk
        ci_ = it - sg * nchunk
        r0 = pl.multiple_of(ci_ * L, L)
        is_start = jnp.logical_and(n == 0, ci_ == 0)
        is_end = jnp.logical_and(n == n_last, ci_ == nchunk - 1)
        seqs = [sg * unroll + u for u in range(unroll)]
        loaded = [load(s, r0, is_start) for s in seqs]
        done = compute_all(loaded)
        for s, d in zip(seqs, done):
            store(s, r0, is_end, *d)
        return carry

    @pl.when(n == 0)
    def _():
        c_scr[...] = jnp.zeros_like(c_scr)
        n_scr[...] = jnp.zeros_like(n_scr)
        m_scr[...] = jnp.zeros_like(m_scr)

    lax.fori_loop(0, (bb // unroll) * nchunk, step, 0)


def _mlstm(z, zt, c0, n0, m0, b_i, b_f, norm_w, *, n_seq, seq_rows, chunk, valid):
    nh, dqk, dv = MLSTM_HEADS, MLSTM_DQK, MLSTM_DV
    unroll = _pick(n_seq, (4, 2, 1))
    if seq_rows > chunk:
        rt = _pick(seq_rows, (256, 128, 64))
        bb = unroll
    else:
        rt = seq_rows
        bb = _pick(n_seq, (8, 4, 2, 1))
    assert rt % chunk == 0 and bb % unroll == 0
    sblk = lambda h, g, n: (g, h, 0, 0)
    gbias = jnp.pad(jnp.concatenate([b_i, b_f]).reshape(1, 2 * nh), ((0, 0), (0, LANES - 2 * nh)))
    n0r = n0.reshape(n_seq, nh, 1, dqk)
    m0r = m0.reshape(n_seq, nh, 1, 1)
    kern = functools.partial(_mlstm_body, bb=bb, rt=rt, chunk=chunk, valid=valid, unroll=unroll)
    y, c_t, n_t, m_t = pl.pallas_call(
        kern,
        grid=(nh, n_seq // bb, seq_rows // rt),
        in_specs=[
            pl.BlockSpec((bb, rt, dqk), lambda h, g, n: (g, n, h)),
            pl.BlockSpec((bb, rt, dqk), lambda h, g, n: (g, n, nh + h)),
            pl.BlockSpec((bb, rt, dv), lambda h, g, n: (g, n, (2 * nh * dqk) // dv + h)),
            pl.BlockSpec((bb, rt, dv), lambda h, g, n: (g, n, h)),
            pl.BlockSpec((bb, rt, LANES), lambda h, g, n: (g, n, (ZO_G - ZO_OG) // LANES)),
            pl.BlockSpec((1, LANES), lambda h, g, n: (0, 0)),
            pl.BlockSpec((1, dv), lambda h, g, n: (0, h)),
            pl.BlockSpec((bb, 1, dv, dqk), sblk),
            pl.BlockSpec((bb, 1, 1, dqk), sblk),
            pl.BlockSpec((bb, 1, 1, 1), sblk),
        ],
        out_specs=[pl.BlockSpec((bb, rt, dv), lambda h, g, n: (g, n, h)),
                   pl.BlockSpec((bb, 1, dv, dqk), sblk),
                   pl.BlockSpec((bb, 1, 1, dqk), sblk),
                   pl.BlockSpec((bb, 1, 1, 1), sblk)],
        out_shape=[jax.ShapeDtypeStruct((n_seq, seq_rows, nh * dv), BF16),
                   jax.ShapeDtypeStruct((n_seq, nh, dv, dqk), F32),
                   jax.ShapeDtypeStruct((n_seq, nh, 1, dqk), F32),
                   jax.ShapeDtypeStruct((n_seq, nh, 1, 1), F32)],
        scratch_shapes=[pltpu.VMEM((bb, dv, dqk), F32), pltpu.VMEM((bb, SUBLANES, dqk), F32),
                        pltpu.VMEM((bb, SUBLANES, LANES), F32)],
        compiler_params=_cparams(("parallel", "parallel", "arbitrary")),
        name="mlstm",
    )(z, z, z, zt, zt, gbias, norm_w.reshape(1, nh * dv), c0, n0r, m0r)
    return y, c_t, n_t.reshape(n_seq, nh, dqk), m_t.reshape(n_seq, nh)


def _pad_cols(a, n):
    return jnp.pad(a, ((0, 0), (0, n - a.shape[1])))


def _pad_rows(a, n):
    return jnp.pad(a, ((0, n - a.shape[0]), (0, 0)))


def _rwkv_cols(a):
    c1 = 3 * MIX_A
    c2 = c1 + RWKV_DECAY_RANK
    c3 = c2 + RWKV_ICL_RANK
    return jnp.concatenate([a[:, :c1], _pad_cols(a[:, c1:c2], RANK_PAD), _pad_cols(a[:, c2:c3], RANK_PAD), a[:, c3:]], axis=1)


def _rwkv_cols_inv(a):
    return jnp.concatenate([a[:, :ZE_DW + RWKV_DECAY_RANK], a[:, ZE_DA:ZE_DA + RWKV_ICL_RANK], a[:, ZE_DG:ZE_U]], axis=1)


def _pad_seq(a, n_seq, t, tp):
    a = a.reshape(n_seq, t, a.shape[1])
    return a if tp == t else jnp.pad(a, ((0, 0), (0, tp - t), (0, 0)))


def _unpad_seq(a, n_seq, t, tp):
    a = a if tp == t else a[:, :t]
    return a.reshape(n_seq * t, a.shape[2])


def _s5_block_weights(zr, zi, b_re, b_im, c_re, c_im):
    g, n, c = S5_GROUPS, S5_STATE, S5_GROUP
    sg = S5_SLAB_GROUPS
    bb_re = zr[..., None] * b_re - zi[..., None] * b_im
    bb_im = zr[..., None] * b_im + zi[..., None] * b_re
    eye = jnp.eye(sg, dtype=F32)

    def in_blocks(bb):
        t = bb.reshape(S5_SLABS, sg, n, c)
        return jnp.einsum('sgnc,gh->sgchn', t, eye).reshape(S5_SLABS, sg * c, sg * n)

    def out_blocks(cc):
        t = cc.reshape(S5_SLABS, sg, c, n)
        return jnp.einsum('sgcn,gh->sgnhc', t, eye).reshape(S5_SLABS, sg * n, sg * c)

    return in_blocks(bb_re), in_blocks(bb_im), out_blocks(c_re), out_blocks(-c_im)


def _trunk(x, n_seq, t, st, w):
    tp = t if t >= MLSTM_CHUNK else SUBLANES * ((t + SUBLANES - 1) // SUBLANES)
    chunk = MLSTM_CHUNK if t >= MLSTM_CHUNK else tp
    valid = chunk if tp == t else t
    assert t % chunk == 0 or tp == chunk

    xn = _rmsnorm(x, w['ln_mix_e'], BF16)
    z, zt = _mm_split(xn, w['w_in_e'], ZE_DW, w['w_in_e_tail'])
    rw = (w['mu'], w['w0'], w['w_up'], w['a0'], w['a_up'], w['g_up'], w['k_k'], w['k_a'], w['r_k'], w['lnx_w'], w['lnx_b'])
    if t < chunk and n_seq == LANES:
        tb = lambda a: a.reshape(n_seq, t, a.shape[1]).transpose(1, 0, 2)
        ya, wkv_t = _rwkv_step(tb(z), tb(zt), _rwkv_cols(st['shift']), st['wkv'].transpose(1, 2, 3, 0), *rw)
        ya = ya.transpose(1, 0, 2).reshape(n_seq * t, MIX_A)
        wkv_t = wkv_t.transpose(3, 0, 1, 2)
    else:
        ya, wkv_t = _rwkv(_pad_seq(z, n_seq, t, tp), _pad_seq(zt, n_seq, t, tp), _rwkv_cols(st['shift']), st['wkv'], *rw,
                          n_seq=n_seq, seq_rows=tp, chunk=chunk, valid=valid)
        ya = _unpad_seq(ya, n_seq, t, tp)
    last = lambda a: a.reshape(n_seq, t, a.shape[1])[:, -1]
    shift_t = _rwkv_cols_inv(jnp.concatenate([last(z), last(zt)[:, :ZT_U]], axis=1))
    yb, re_t, im_t = _s5_scan(zt, ZT_U, w['s5_wbr'], w['s5_wbi'], w['s5_wcr'], w['s5_wci'], w['s5_ab_re'], w['s5_ab_im'],
                              w['s5_d'], st['s5_re'], st['s5_im'], n_seq=n_seq, seq_rows=t)
    yb = _glu(yb, w['s5_glu_w'], w['s5_glu_b'])
    x = _mm2_res(ya, yb, w['w_out_e'], x)
    hn = _rmsnorm(x, w['ln_ffn_e'], BF16)
    hid = _glu_up(hn, w['ffn_w1'], w['ffn_w3'])
    x = _mm_acc(hid, w['ffn_w2'], x)

    xn = _rmsnorm(x, w['ln_mix_o'], BF16)
    zo, zot = _mm_split(xn, w['w_in_o'], ZO_OG, w['w_in_o_tail'])
    yc, c_t, n_t, m_t = _mlstm(_pad_seq(zo, n_seq, t, tp), _pad_seq(zot, n_seq, t, tp), st['c'], st['n'], st['m'], w['b_i'], w['b_f'], w['norm_w'],
                               n_seq=n_seq, seq_rows=tp, chunk=chunk, valid=valid)
    yc = _unpad_seq(yc, n_seq, t, tp)
    x = _mm_res(yc, w['w_out_o'], x)
    hn, comb = _rmsnorm_router(x, w['ln_ffn_o'], w['router_w'], w['router_b'])

    g, ns = S5_GROUPS, S5_STATE
    states = (wkv_t[None], shift_t[None], re_t.reshape(1, n_seq, g, ns), im_t.reshape(1, n_seq, g, ns),
              c_t[None], n_t[None], m_t[None])
    return (x, hn, comb), states


def kernel(x_prompt, x_sample, state_rwkv_wkv, state_rwkv_shift, state_s5_re, state_s5_im, state_mlstm_c, state_mlstm_n, state_mlstm_m, ln_mix_e, w_in_e, rwkv_mu, rwkv_w0, rwkv_w_up, rwkv_a0, rwkv_a_up, rwkv_g_up, rwkv_k_k, rwkv_k_a, rwkv_r_k, rwkv_lnx_w, rwkv_lnx_b, s5_a_re, s5_a_im, s5_log_step, s5_b_re, s5_b_im, s5_c_re, s5_c_im, s5_d, s5_glu_w, s5_glu_b, w_out_e, ln_ffn_e, ffn_w1, ffn_w3, ffn_w2, ln_mix_o, w_in_o, mlstm_b_i, mlstm_b_f, mlstm_norm_w, w_out_o, ln_ffn_o, router_w, router_b, exp_w1, exp_w3, exp_w2, final_norm):
    assert ln_mix_e.shape[0] == 1 and ln_mix_o.shape[0] == 1
    d = D_MODEL
    ab_re, ab_im, zr, zi = _s5_prep(s5_a_re[0], s5_a_im[0], s5_log_step[0])
    wbr, wbi, wcr, wci = _s5_block_weights(zr, zi, s5_b_re[0], s5_b_im[0], s5_c_re[0], s5_c_im[0])
    wo = jnp.swapaxes(w_in_o[0], 0, 1)
    we = jnp.swapaxes(w_in_e[0], 0, 1)
    c_dw = 3 * MIX_A
    c_da = c_dw + RWKV_DECAY_RANK
    c_dg = c_da + RWKV_ICL_RANK
    nh = MLSTM_HEADS
    w = {
        'ln_mix_e': ln_mix_e[0],
        'w_in_e': we,
        'w_in_e_tail': jnp.concatenate([_pad_rows(we[c_dw:c_da], RANK_PAD), _pad_rows(we[c_da:c_dg], RANK_PAD),
                                        we[c_dg:]], axis=0),
        'mu': _rwkv_cols(rwkv_mu[0].reshape(1, RWKV_IN)),
        'w0': rwkv_w0[0], 'a0': rwkv_a0[0],
        'w_up': jnp.pad(rwkv_w_up[0], ((0, RANK_PAD - RWKV_DECAY_RANK), (0, 0))),
        'a_up': jnp.pad(rwkv_a_up[0], ((0, RANK_PAD - RWKV_ICL_RANK), (0, 0))),
        'g_up': rwkv_g_up[0], 'k_k': rwkv_k_k[0], 'k_a': rwkv_k_a[0], 'r_k': rwkv_r_k[0].reshape(MIX_A),
        'lnx_w': rwkv_lnx_w[0], 'lnx_b': rwkv_lnx_b[0],
        's5_wbr': wbr, 's5_wbi': wbi, 's5_wcr': wcr, 's5_wci': wci,
        's5_ab_re': ab_re, 's5_ab_im': ab_im, 's5_d': s5_d[0], 's5_glu_w': s5_glu_w[0], 's5_glu_b': s5_glu_b[0],
        'w_out_e': w_out_e[0], 'ln_ffn_e': ln_ffn_e[0],
        'ffn_w1': ffn_w1[0], 'ffn_w3': ffn_w3[0], 'ffn_w2': ffn_w2[0],
        'ln_mix_o': ln_mix_o[0],
        'w_in_o': wo,
        'w_in_o_tail': jnp.concatenate([wo[ZO_OG + 2 * nh:], _pad_rows(wo[ZO_OG:ZO_OG + 2 * nh], ZO_W - ZO_G)], axis=0),
        'b_i': mlstm_b_i[0], 'b_f': mlstm_b_f[0], 'norm_w': mlstm_norm_w[0],
        'w_out_o': w_out_o[0], 'ln_ffn_o': ln_ffn_o[0],
        'router_w': router_w[0], 'router_b': router_b[0],
        'exp_w1': exp_w1[0], 'exp_w3': exp_w3[0], 'exp_w2': exp_w2[0],
        'final_norm': final_norm,
    }
    bp, tp_, _ = x_prompt.shape
    bs, ts_, _ = x_sample.shape
    g, ns = S5_GROUPS, S5_STATE
    zero = lambda *s: jnp.zeros(s, F32)
    st_p = {'wkv': zero(bp, RWKV_HEADS, RWKV_HEAD, RWKV_HEAD), 'shift': zero(bp, RWKV_IN),
            's5_re': zero(bp, g * ns), 's5_im': zero(bp, g * ns),
            'c': zero(bp, nh, MLSTM_DV, MLSTM_DQK), 'n': zero(bp, nh, MLSTM_DQK), 'm': zero(bp, nh)}
    st_s = {'wkv': state_rwkv_wkv[0], 'shift': state_rwkv_shift[0],
            's5_re': state_s5_re[0].reshape(bs, g * ns), 's5_im': state_s5_im[0].reshape(bs, g * ns),
            'c': state_mlstm_c[0], 'n': state_mlstm_n[0], 'm': state_mlstm_m[0]}
    (x_p, hn_p, comb_p), states_p = _trunk(x_prompt.reshape(bp * tp_, d), bp, tp_, st_p, w)
    (x_s, hn_s, comb_s), states_s = _trunk(x_sample.reshape(bs * ts_, d), bs, ts_, st_s, w)
    y_p, y_s = _moe_sparse([x_p, x_s], [hn_p, hn_s], [comb_p, comb_s], w['exp_w1'], w['exp_w3'], w['exp_w2'], w['final_norm'])
    return (y_p.reshape(bp, tp_, d), y_s.reshape(bs, ts_, d)) + tuple(states_p) + tuple(states_s)
```
